```python
import numpy as np
import jax
import jax.numpy as jnp
from jax import lax

D_MODEL = 1024
BATCH = 4
SEQ = 8192
DEPTH = 2

GRID_W = 64
CTX_LEN = 256
D_MIX = D_MODEL
GROUP_W = D_MIX // 4
GLA_HEADS = 4
GLA_DV = GROUP_W // GLA_HEADS
GLA_DK = GLA_DV // 2
GLA_QK_W = GLA_HEADS * GLA_DK
GLA_GATE_RANK = 16
GLA_GATE_NORM = 16.0
GLA_CHUNK = 64
CONF_WIDTH = 31
SC_WIDTH = 3
POOL_WINDOWS = (2, 4, 8, 16)
POOL_GROUPS = 4
POOL_GW = GROUP_W // POOL_GROUPS
N_GROUPS = 4
EXPERTS_PER_GROUP = 8
N_EXPERTS = N_GROUPS * EXPERTS_PER_GROUP
TOP_K_INNER = 2
D_EXPERT = D_MODEL // 2
MOE_BLOCK = 256
NORM_EPS = 1e-6
IN_SPLITS = (GLA_QK_W, GLA_QK_W, GROUP_W, GROUP_W, GLA_GATE_RANK, GLA_GATE_RANK,
             2 * GROUP_W, GROUP_W, GROUP_W, GROUP_W, GROUP_W)
N_IN = 2 * GLA_QK_W + 2 * GLA_GATE_RANK + 8 * GROUP_W

kernel_name = 'hybrid_parallel_group_flow_block'


def rms_norm(x, g):
    xf = x.astype(jnp.float32)
    y = xf * lax.rsqrt(jnp.mean(xf * xf, axis=-1, keepdims=True) + NORM_EPS)
    return (y * g.astype(jnp.float32)).astype(x.dtype)


def layer_norm(x, g, b):
    xf = x.astype(jnp.float32)
    xc = xf - jnp.mean(xf, axis=-1, keepdims=True)
    y = xc * lax.rsqrt(jnp.mean(xc * xc, axis=-1, keepdims=True) + NORM_EPS)
    return (y * g.astype(jnp.float32) + b.astype(jnp.float32)).astype(x.dtype)


def adaln(cvec, w, b):
    m = jax.nn.silu(cvec) @ w + b
    m = m.reshape(m.shape[:-1] + (6, D_MODEL))
    return [m[..., None, i, :] for i in range(6)]


def modulate(x, shift, scale):
    return x * (1 + scale) + shift


def split_columns(h):
    idx = np.cumsum(IN_SPLITS)[:-1].tolist()
    return jnp.split(h, idx, axis=-1)


def dwconv(x, w):
    k = w.shape[0]
    return lax.conv_general_dilated(
        x, w[:, None, :].astype(x.dtype), window_strides=(1,), padding=[(k // 2, k // 2)],
        dimension_numbers=('NWC', 'WIO', 'NWC'), feature_group_count=x.shape[-1])


def axial_dwconv(x, w):
    bsz, length, ch = x.shape
    rows = length // GRID_W
    half = ch // 2
    g = x.reshape(bsz, rows, GRID_W, ch)
    horiz = dwconv(g[..., :half].reshape(bsz * rows, GRID_W, half), w[:, :half])
    horiz = horiz.reshape(bsz, rows, GRID_W, half)
    vert = g[..., half:].transpose(0, 2, 1, 3).reshape(bsz * GRID_W, rows, ch - half)
    vert = dwconv(vert, w[:, half:]).reshape(bsz, GRID_W, rows, ch - half).transpose(0, 2, 1, 3)
    return jnp.concatenate([horiz, vert], axis=-1).reshape(bsz, length, ch)


def centred_mean(x, win):
    length = x.shape[1]
    cs = jnp.pad(jnp.cumsum(x.astype(jnp.float32), axis=1), ((0, 0), (1, 0), (0, 0)))
    t = np.arange(length)
    lo = np.clip(t - win // 2, 0, length)
    hi = np.clip(t + win // 2, 0, length)
    cnt = jnp.asarray((hi - lo)[None, :, None], jnp.float32)
    return ((cs[:, hi] - cs[:, lo]) / cnt).astype(x.dtype)


def row_mean(x, win):
    bsz, length, ch = x.shape
    rows = length // GRID_W
    return centred_mean(x.reshape(bsz * rows, GRID_W, ch), win).reshape(bsz, length, ch)


def to_heads(t, n_heads):
    bsz, length, width = t.shape
    return t.reshape(bsz, length, n_heads, width // n_heads).transpose(0, 2, 1, 3).astype(jnp.float32)


def gla_chunked(q, k, v, log_a, s0):
    bsz, nh, length, dk = q.shape
    dv = v.shape[-1]
    nc = length // GLA_CHUNK
    shp = (bsz, nh, nc, GLA_CHUNK)
    q, k, log_a = (t.reshape(shp + (dk,)) for t in (q, k, log_a))
    v = v.reshape(shp + (dv,))
    b = jnp.cumsum(log_a, axis=3)
    b_last = b[:, :, :, -1:, :]
    q_dec = q * jnp.exp(b)
    k_inv = k * jnp.exp(-b)
    mask = np.tril(np.ones((GLA_CHUNK, GLA_CHUNK), dtype=bool))
    scores = jnp.where(mask, jnp.einsum('bhnid,bhnjd->bhnij', q_dec, k_inv), 0.0)
    o_intra = jnp.einsum('bhnij,bhnjv->bhniv', scores, v)
    chunk_kv = jnp.einsum('bhnjd,bhnjv->bhndv', k * jnp.exp(b_last - b), v)
    chunk_decay = jnp.exp(b_last[:, :, :, 0, :])

    def step(state, inp):
        dec, kv = inp
        return dec[..., None] * state + kv, state

    s_final, s_in = lax.scan(step, s0, (jnp.moveaxis(chunk_decay, 2, 0), jnp.moveaxis(chunk_kv, 2, 0)))
    o_inter = jnp.einsum('bhnid,nbhdv->bhniv', q_dec, s_in)
    return (o_intra + o_inter).reshape(bsz, nh, length, dv), s_final


def gla_bidir(q, k, v, la_f, la_b, s0_f, s0_b):
    o_f, s_f = gla_chunked(q, k, v, la_f, s0_f)
    rev = lambda t: jnp.flip(t, axis=2)
    o_b, s_b = gla_chunked(rev(q), rev(k), rev(v), rev(la_b), s0_b)
    return o_f + rev(o_b), s_f, s_b


def gla_branch(cols, w_gf, b_gf, w_gb, b_gb, g_out, s0_f, s0_b):
    q, k, v, g, z_f, z_b = cols[:6]

    def decay(z, w, b):
        return to_heads(jax.nn.log_sigmoid((z @ w + b).astype(jnp.float32)) / GLA_GATE_NORM, GLA_HEADS)

    o, s_f, s_b = gla_bidir(to_heads(q, GLA_HEADS) * GLA_DK ** -0.5, to_heads(k, GLA_HEADS),
                            to_heads(v, GLA_HEADS), decay(z_f, w_gf, b_gf), decay(z_b, w_gb, b_gb),
                            s0_f, s0_b)
    o = o * lax.rsqrt(jnp.mean(o * o, axis=-1, keepdims=True) + NORM_EPS)
    bsz, nh, length, dv = o.shape
    o = o.transpose(0, 2, 1, 3).reshape(bsz, length, nh * dv) * g_out.astype(jnp.float32)
    return (o * jax.nn.silu(g.astype(jnp.float32))).astype(g.dtype), s_f, s_b


def local_branches(cols, conf_w, conf_b, conf_g, conf_beta, sc_w, pool_w, pool_scale, conv_fn, mean_fn):
    conf_in, sc_b, sc_c, sc_x, pool_in = cols[6:]
    a, gate = jnp.split(conf_in, 2, axis=-1)
    u = conv_fn(a * jax.nn.sigmoid(gate), conf_w) + conf_b
    y_conf = jax.nn.silu(layer_norm(u, conf_g, conf_beta))
    y_sc = sc_b * conv_fn(sc_c * sc_x, sc_w)
    diffs = []
    for i, win in enumerate(POOL_WINDOWS):
        hg = pool_in[..., i * POOL_GW:(i + 1) * POOL_GW]
        diffs.append(mean_fn(hg, win) - hg)
    y_pool = jnp.einsum('blgc,gcd->blgd', jnp.stack(diffs, axis=-2), pool_w).reshape(pool_in.shape) * pool_scale
    return jnp.concatenate([y_conf, y_sc, y_pool], axis=-1)


def grouped_experts(h, eid, gate, w_in, w_out):
    n, d = h.shape
    a = eid.shape[0]
    tok = jnp.arange(a) // TOP_K_INNER
    order = jnp.argsort(eid)
    e_s, tok_s, g_s = eid[order], tok[order], gate[order]
    counts = jnp.zeros((N_EXPERTS,), jnp.int32).at[eid].add(1)
    starts = jnp.cumsum(counts) - counts
    padded = (counts + MOE_BLOCK - 1) // MOE_BLOCK * MOE_BLOCK
    pad_end = jnp.cumsum(padded)
    pad_start = pad_end - padded
    dest = pad_start[e_s] + jnp.arange(a) - starts[e_s]
    n_blocks = -(-a // MOE_BLOCK) + N_EXPERTS
    slots = n_blocks * MOE_BLOCK
    slot_tok = jnp.zeros((slots,), jnp.int32).at[dest].set(tok_s)
    slot_gate = jnp.zeros((slots,), gate.dtype).at[dest].set(g_s)
    blk_expert = jnp.minimum(jnp.searchsorted(pad_end, jnp.arange(n_blocks) * MOE_BLOCK, side='right'),
                             N_EXPERTS - 1)

    def run_block(args):
        e, toks = args
        xb = h[toks]
        gt, up = jnp.split(xb @ w_in[e], 2, axis=-1)
        return (jax.nn.silu(gt) * up) @ w_out[e]

    y = lax.map(run_block, (blk_expert, slot_tok.reshape(n_blocks, MOE_BLOCK)))
    y = y.reshape(slots, d) * slot_gate[:, None].astype(y.dtype)
    return jax.ops.segment_sum(y, slot_tok, num_segments=n)


def hier_moe(h, w_rg, b_rg, w_re, b_re, w_e_in, w_e_out):
    n = h.shape[0]
    rows = jnp.arange(n)
    lg = (h @ w_rg + b_rg).astype(jnp.float32)
    grp = jnp.argmax(lg, axis=-1)
    p_grp = jax.nn.softmax(lg, axis=-1)[rows, grp][:, None]
    le = (h @ w_re + b_re).astype(jnp.float32).reshape(n, N_GROUPS, EXPERTS_PER_GROUP)
    top_v, top_i = lax.top_k(le[rows, grp], TOP_K_INNER)
    w = jax.nn.softmax(top_v, axis=-1) * p_grp
    eid = grp[:, None] * EXPERTS_PER_GROUP + top_i
    return grouped_experts(h, eid.reshape(-1), w.reshape(-1), w_e_in, w_e_out)


def setup_inputs(seed: int = 0) -> dict:
    key = jax.random.key(seed)
    keys = iter(jax.random.split(key, 32))

    def rnd(shape, scale):
        return jax.random.normal(next(keys), shape, jnp.float32) * scale

    L, D = DEPTH, D_MODEL
    return {
        'x': rnd((BATCH, SEQ, D), 1.0),
        'c': rnd((BATCH, D), 1.0),
        'ctx': rnd((BATCH, CTX_LEN, D), 1.0),
        'c_ctx': rnd((D,), 1.0),
        'w_mod': rnd((L, D, 6 * D), 0.5 * D ** -0.5),
        'b_mod': rnd((L, 6 * D), 0.02),
        'g_norm1': 1.0 + rnd((L, D), 0.05),
        'g_norm2': 1.0 + rnd((L, D), 0.05),
        'w_in': rnd((L, D, N_IN), D ** -0.5),
        'gla_w_gate_f': rnd((L, GLA_GATE_RANK, GLA_QK_W), GLA_GATE_RANK ** -0.5),
        'gla_b_gate_f': rnd((L, GLA_QK_W), 0.1),
        'gla_w_gate_b': rnd((L, GLA_GATE_RANK, GLA_QK_W), GLA_GATE_RANK ** -0.5),
        'gla_b_gate_b': rnd((L, GLA_QK_W), 0.1),
        'gla_g_out': 1.0 + rnd((L, GROUP_W), 0.05),
        'conf_w_dw': rnd((L, CONF_WIDTH, GROUP_W), CONF_WIDTH ** -0.5),
        'conf_b_dw': rnd((L, GROUP_W), 0.02),
        'conf_ln_g': 1.0 + rnd((L, GROUP_W), 0.05),
        'conf_ln_b': rnd((L, GROUP_W), 0.02),
        'sc_w_dw': rnd((L, SC_WIDTH, GROUP_W), SC_WIDTH ** -0.5),
        'pool_w': rnd((L, POOL_GROUPS, POOL_GW, POOL_GW), POOL_GW ** -0.5),
        'pool_scale': 1.0 + rnd((L, GROUP_W), 0.1),
        'w_out': rnd((L, D_MIX, D), D_MIX ** -0.5),
        'router_w_group': rnd((L, D, N_GROUPS), D ** -0.5),
        'router_b_group': rnd((L, N_GROUPS), 0.01),
        'router_w_expert': rnd((L, D, N_EXPERTS), D ** -0.5),
        'router_b_expert': rnd((L, N_EXPERTS), 0.01),
        'expert_w_in': rnd((L, N_EXPERTS, D, 2 * D_EXPERT), D ** -0.5),
        'expert_w_out': rnd((L, N_EXPERTS, D_EXPERT, D), D_EXPERT ** -0.5),
        'g_final': 1.0 + rnd((D,), 0.05),
    }


def reference(x, c, ctx, c_ctx, w_mod, b_mod, g_norm1, g_norm2, w_in, gla_w_gate_f, gla_b_gate_f,
              gla_w_gate_b, gla_b_gate_b, gla_g_out, conf_w_dw, conf_b_dw, conf_ln_g, conf_ln_b,
              sc_w_dw, pool_w, pool_scale, w_out, router_w_group, router_b_group, router_w_expert,
              router_b_expert, expert_w_in, expert_w_out, g_final):
    bsz, seq, d = x.shape
    n_lat = bsz * seq
    s0 = jnp.zeros((bsz, GLA_HEADS, GLA_DK, GLA_DV), jnp.float32)
    for l in range(DEPTH):
        last = l == DEPTH - 1
        sh1, sc1, ga1, sh2, sc2, ga2 = adaln(c, w_mod[l], b_mod[l])
        csh1, csc1, cga1, csh2, csc2, cga2 = adaln(c_ctx, w_mod[l], b_mod[l])
        gla_p = (gla_w_gate_f[l], gla_b_gate_f[l], gla_w_gate_b[l], gla_b_gate_b[l], gla_g_out[l])
        loc_p = (conf_w_dw[l], conf_b_dw[l], conf_ln_g[l], conf_ln_b[l], sc_w_dw[l], pool_w[l], pool_scale[l])
        moe_p = (router_w_group[l], router_b_group[l], router_w_expert[l], router_b_expert[l],
                 expert_w_in[l], expert_w_out[l])

        cols_c = split_columns(modulate(rms_norm(ctx, g_norm1[l]), csh1, csc1) @ w_in[l])
        cols_x = split_columns(modulate(rms_norm(x, g_norm1[l]), sh1, sc1) @ w_in[l])
        y_gla_c, s_f, s_b = gla_branch(cols_c, *gla_p, s0, s0)
        y_gla_x, _, _ = gla_branch(cols_x, *gla_p, s_f, s_b)
        y_loc_x = local_branches(cols_x, *loc_p, axial_dwconv, row_mean)
        x = x + ga1 * (jnp.concatenate([y_gla_x, y_loc_x], axis=-1) @ w_out[l])
        if not last:
            y_loc_c = local_branches(cols_c, *loc_p, dwconv, centred_mean)
            ctx = ctx + cga1 * (jnp.concatenate([y_gla_c, y_loc_c], axis=-1) @ w_out[l])

        m_x = modulate(rms_norm(x, g_norm2[l]), sh2, sc2).reshape(n_lat, d)
        if last:
            x = x + ga2 * hier_moe(m_x, *moe_p).reshape(bsz, seq, d)
        else:
            m_c = modulate(rms_norm(ctx, g_norm2[l]), csh2, csc2).reshape(-1, d)
            y = hier_moe(jnp.concatenate([m_x, m_c], axis=0), *moe_p)
            x = x + ga2 * y[:n_lat].reshape(bsz, seq, d)
            ctx = ctx + cga2 * y[n_lat:].reshape(ctx.shape)
    return rms_norm(x, g_final)
```

```python
import functools

import numpy as np
import jax
import jax.numpy as jnp
from jax import lax
from jax.experimental import pallas as pl
from jax.experimental.pallas import tpu as pltpu

F32 = jnp.float32
BF16 = jnp.bfloat16

D_MODEL = 1024
GRID_W = 64
GROUP_W = 256
GLA_HEADS = 4
GLA_DV = 64
GLA_DK = 32
GLA_QK_W = 128
GLA_GATE_RANK = 16
GLA_GATE_NORM = 16.0
GLA_CHUNK = 64
GLA_GROUP = 256
CONF_WIDTH = 31
SC_WIDTH = 3
POOL_WINDOWS = (2, 4, 8, 16)
POOL_GW = 64
N_GROUPS = 4
EXPERTS_PER_GROUP = 8
N_EXPERTS = 32
D_EXPERT = 512
MOE_BLOCK = 256
NORM_EPS = 1e-6
IN_SPLITS = (128, 128, 256, 256, 16, 16, 512, 256, 256, 256, 256)
N_IN_PAD = 2432
LANE = 128
ROUTE_W = LANE

VMEM_LIMIT = 48 << 20


def _cp(sem, vmem=VMEM_LIMIT):
    return pltpu.CompilerParams(dimension_semantics=sem, vmem_limit_bytes=vmem)


def _sigmoid(x):
    return 1.0 / (1.0 + jnp.exp(-x))


def _silu(x):
    return x * _sigmoid(x)


def _log_sigmoid(x):
    return jnp.minimum(x, 0.0) - jnp.log1p(jnp.exp(-jnp.abs(x)))


def _dot(a, b):
    return jnp.dot(a.astype(BF16), b.astype(BF16), preferred_element_type=F32)


def _rms(x, g):
    return x * lax.rsqrt(jnp.mean(x * x, axis=-1, keepdims=True) + NORM_EPS) * g


def _adaln_kernel(cv_ref, w_ref, b_ref, o_ref):
    o_ref[0] = _dot(_silu(cv_ref[...]), w_ref[0]) + b_ref[0]


def _adaln(cv, w_mod, b_mod):
    depth, d, n = w_mod.shape
    tn = 1024
    return pl.pallas_call(
        _adaln_kernel,
        grid=(depth, n // tn),
        in_specs=[pl.BlockSpec((8, d), lambda l, j: (0, 0)),
                  pl.BlockSpec((1, d, tn), lambda l, j: (l, 0, j)),
                  pl.BlockSpec((1, 1, tn), lambda l, j: (l, 0, j))],
        out_specs=pl.BlockSpec((1, 8, tn), lambda l, j: (l, 0, j)),
        out_shape=jax.ShapeDtypeStruct((depth, 8, n), F32),
        compiler_params=_cp(("parallel", "parallel")),
        name="adaln",
    )(cv, w_mod, b_mod.reshape(depth, 1, n))


def _inproj_kernel(x_ref, mod_ref, g1_ref, w_ref, wg_ref, bg_ref,
                   qkvg_ref, la_ref, u_ref, sc_ref, pool_ref):
    x = x_ref[0]
    mod = mod_ref[0]
    m = _rms(x, g1_ref[...]) * (1.0 + mod[1:2]) + mod[0:1]
    h = _dot(m, w_ref[...])
    qkvg_ref[0] = h[:, 0:768]
    u_ref[0] = h[:, 768:1024] * _sigmoid(h[:, 1024:1280])
    sc_ref[0, :, 0:256] = h[:, 1280:1536]
    sc_ref[0, :, 256:512] = h[:, 1536:1792] * h[:, 1792:2048]
    pool_ref[0] = h[:, 2048:2304]
    z = _dot(h[:, 2304:2432], wg_ref[...]) + bg_ref[...]
    la_ref[0] = _log_sigmoid(z) / GLA_GATE_NORM


def _inproj(x, mod, mod_row, g1, w_in_p, w_gate, b_gate, tm):
    bsz, seq, d = x.shape
    row = (lambda b: b) if mod_row is None else (lambda b: mod_row)
    tok = lambda w: pl.BlockSpec((1, tm, w), lambda b, i: (b, i, 0))
    full = lambda a: pl.BlockSpec(a.shape, lambda b, i: (0,) * a.ndim)
    outs = (768, 256, 256, 512, 256)
    return pl.pallas_call(
        _inproj_kernel,
        grid=(bsz, seq // tm),
        in_specs=[tok(d),
                  pl.BlockSpec((1, 6, d), lambda b, i: (row(b), 0, 0)),
                  full(g1), full(w_in_p), full(w_gate), full(b_gate)],
        out_specs=[tok(w) for w in outs],
        out_shape=[jax.ShapeDtypeStruct((bsz, seq, w), F32) for w in outs],
        compiler_params=_cp(("parallel", "parallel")),
        name="inproj",
    )(x, mod, g1, w_in_p, w_gate, b_gate)


def _gla_direction(q, k, v, la, state, reverse):
    blk = q.shape[0]
    g = GLA_GROUP
    nchunk = g // GLA_CHUNK
    ti = lax.broadcasted_iota(jnp.int32, (g, g), 0)
    tj = lax.broadcasted_iota(jnp.int32, (g, g), 1)
    same = (ti // GLA_CHUNK) == (tj // GLA_CHUNK)
    cmask = same & ((tj >= ti) if reverse else (tj <= ti))
    tri = jnp.where(cmask, 1.0, 0.0).astype(BF16)
    qk_head = lax.broadcasted_iota(jnp.int32, (g, GLA_QK_W), 1) // GLA_DK
    v_head = lax.broadcasted_iota(jnp.int32, (g, GROUP_W), 1) // GLA_DV
    bd = (lax.broadcasted_iota(jnp.int32, (GLA_QK_W, GROUP_W), 0) // GLA_DK
          == lax.broadcasted_iota(jnp.int32, (GLA_QK_W, GROUP_W), 1) // GLA_DV)
    tok_chunk = lax.broadcasted_iota(jnp.int32, (GLA_QK_W, g), 1) // GLA_CHUNK
    scale = GLA_DK ** -0.5
    outs = [None] * (blk // g)
    groups = range(blk // g)
    for gi in (reversed(groups) if reverse else groups):
        sl = slice(gi * g, (gi + 1) * g)
        qg, kg, vg, lg = q[sl], k[sl], v[sl], la[sl]
        l0 = lg.astype(BF16)
        r0 = lg - l0.astype(F32)
        l1 = r0.astype(BF16)
        l2 = (r0 - l1.astype(F32)).astype(BF16)
        b = (jnp.dot(tri, l0, preferred_element_type=F32)
             + jnp.dot(tri, l1, preferred_element_type=F32)
             + jnp.dot(tri, l2, preferred_element_type=F32))
        last = 0 if reverse else GLA_CHUNK - 1
        blast = jnp.concatenate(
            [jnp.broadcast_to(b[c * GLA_CHUNK + last:c * GLA_CHUNK + last + 1], (GLA_CHUNK, GLA_QK_W))
             for c in range(nchunk)], axis=0)
        qd = (qg * scale) * jnp.exp(b)
        ki = (kg * jnp.exp(-b)).astype(BF16)
        kw_t = (kg * jnp.exp(blast - b)).T
        b_t = b.T
        vb = vg.astype(BF16)
        o = jnp.zeros((g, GROUP_W), F32)
        for h in range(GLA_HEADS):
            qh = jnp.where(qk_head == h, qd, 0.0).astype(BF16)
            s = lax.dot_general(qh, ki, (((1,), (1,)), ((), ())), preferred_element_type=F32)
            s = jnp.where(cmask, s, 0.0).astype(BF16)
            oh = jnp.dot(s, vb, preferred_element_type=F32)
            o = o + jnp.where(v_head == h, oh, 0.0)
        qdb = qd.astype(BF16)
        inter = [None] * nchunk
        chunks = range(nchunk)
        for c in (reversed(chunks) if reverse else chunks):
            rows = slice(c * GLA_CHUNK, (c + 1) * GLA_CHUNK)
            inter[c] = jnp.dot(qdb[rows], state.astype(BF16), preferred_element_type=F32)
            col = c * GLA_CHUNK + last
            dec = jnp.exp(b_t[:, col:col + 1])
            kv = jnp.dot(jnp.where(tok_chunk == c, kw_t, 0.0).astype(BF16), vb,
                         preferred_element_type=F32)
            state = dec * state + jnp.where(bd, kv, 0.0)
        outs[gi] = o + jnp.concatenate(inter, axis=0)
    return jnp.concatenate(outs, axis=0) if len(outs) > 1 else outs[0], state


def _gla_kernel(qk_f, v_f, la_f, qk_b, v_b, la_b, s0_ref, of_ref, ob_ref, sfin_ref, st_ref):
    i = pl.program_id(1)

    @pl.when(i == 0)
    def _():
        st_ref[...] = s0_ref[0]

    qk = qk_f[0]
    o, s = _gla_direction(qk[:, 0:128], qk[:, 128:256], v_f[0], la_f[0], st_ref[0], False)
    of_ref[0] = o
    st_ref[0] = s
    qk = qk_b[0]
    o, s = _gla_direction(qk[:, 0:128], qk[:, 128:256], v_b[0], la_b[0], st_ref[1], True)
    ob_ref[0] = o
    st_ref[1] = s

    @pl.when(i == pl.num_programs(1) - 1)
    def _():
        sfin_ref[0] = st_ref[...]


def _gla(qkvg, la, s0, blk):
    bsz, seq, _ = qkvg.shape
    nb = seq // blk
    fwd = lambda j: pl.BlockSpec((1, blk, 256), lambda b, i: (b, i, j))
    bwd = lambda j: pl.BlockSpec((1, blk, 256), lambda b, i: (b, nb - 1 - i, j))
    la_f = pl.BlockSpec((1, blk, 128), lambda b, i: (b, i, 0))
    la_b = pl.BlockSpec((1, blk, 128), lambda b, i: (b, nb - 1 - i, 1))
    st = pl.BlockSpec((1, 2, GLA_QK_W, GROUP_W), lambda b, i: (b, 0, 0, 0))
    return pl.pallas_call(
        _gla_kernel,
        grid=(bsz, nb),
        in_specs=[fwd(0), fwd(1), la_f, bwd(0), bwd(1), la_b, st],
        out_specs=[pl.BlockSpec((1, blk, GROUP_W), lambda b, i: (b, i, 0)),
                   pl.BlockSpec((1, blk, GROUP_W), lambda b, i: (b, nb - 1 - i, 0)),
                   st],
        out_shape=[jax.ShapeDtypeStruct((bsz, seq, GROUP_W), F32),
                   jax.ShapeDtypeStruct((bsz, seq, GROUP_W), F32),
                   jax.ShapeDtypeStruct((bsz, 2, GLA_QK_W, GROUP_W), F32)],
        scratch_shapes=[pltpu.VMEM((2, GLA_QK_W, GROUP_W), F32)],
        compiler_params=_cp(("parallel", "arbitrary")),
        name="gla",
    )(qkvg, qkvg, la, qkvg, qkvg, la, s0)


def _row_shift(x, s, col, row_len):
    if s == 0:
        return x
    n = x.shape[0]
    y = pltpu.roll(x, (-s) % n, 0)
    valid = (col < row_len - s) if s > 0 else (col >= -s)
    return jnp.where(valid, y, 0.0)


def _row_conv(x, w, col, row_len):
    k = w.shape[0]
    acc = None
    for j in range(k):
        term = _row_shift(x, j - k // 2, col, row_len) * w[j:j + 1]
        acc = term if acc is None else acc + term
    return acc


def _col_conv(pad_ref, w, t0, n):
    k = w.shape[0]
    acc = None
    for j in range(k):
        start = pl.multiple_of(t0 + j * GRID_W, GRID_W)
        term = pad_ref[pl.ds(start, n), :] * w[j:j + 1]
        acc = term if acc is None else acc + term
    return acc


def _pool_diffs(x, col, row_len):
    lane = lax.broadcasted_iota(jnp.int32, x.shape, 1)
    colf = col.astype(F32)
    mean = jnp.zeros_like(x)
    for gi, win in enumerate(POOL_WINDOWS):
        half = win // 2
        acc = None
        for s in range(-half, half):
            t = _row_shift(x, s, col, row_len)
            acc = t if acc is None else acc + t
        cnt = jnp.minimum(colf + half, float(row_len)) - jnp.maximum(colf - half, 0.0)
        mean = jnp.where(lane // POOL_GW == gi, acc / cnt, mean)
    return mean - x


def _local_tail(u, scb, conv_sc, pool, col, row_len, cb, lg, lb, pw, ps, out_ref):
    u = u + cb
    uc = u - jnp.mean(u, axis=-1, keepdims=True)
    ln = uc * lax.rsqrt(jnp.mean(uc * uc, axis=-1, keepdims=True) + NORM_EPS) * lg + lb
    out_ref[0, :, 0:256] = _silu(ln)
    out_ref[0, :, 256:512] = scb * conv_sc
    out_ref[0, :, 512:768] = _dot(_pool_diffs(pool, col, row_len), pw) * ps


def _local_axial_kernel(uh_ref, uv_ref, scb_ref, sch_ref, scv_ref, pool_ref,
                        cw_ref, cb_ref, lg_ref, lb_ref, sw_ref, pw_ref, ps_ref,
                        out_ref, upad, spad):
    i = pl.program_id(1)
    blk = uh_ref.shape[1]
    seq = uv_ref.shape[1]
    pu = (CONF_WIDTH // 2) * GRID_W
    psc = (SC_WIDTH // 2) * GRID_W

    @pl.when(i == 0)
    def _():
        upad[0:pu, :] = jnp.zeros((pu, LANE), F32)
        upad[pu:pu + seq, :] = uv_ref[0]
        upad[pu + seq:pu + seq + pu, :] = jnp.zeros((pu, LANE), F32)
        spad[0:psc, :] = jnp.zeros((psc, LANE), F32)
        spad[psc:psc + seq, :] = scv_ref[0]
        spad[psc + seq:psc + seq + psc, :] = jnp.zeros((psc, LANE), F32)

    t0 = i * blk
    col = lax.broadcasted_iota(jnp.int32, (blk, LANE), 0) % GRID_W
    cw = cw_ref[...]
    sw = sw_ref[...]
    u = jnp.concatenate([_row_conv(uh_ref[0], cw[:, 0:128], col, GRID_W),
                         _col_conv(upad, cw[:, 128:256], t0, blk)], axis=-1)
    csc = jnp.concatenate([_row_conv(sch_ref[0], sw[:, 0:128], col, GRID_W),
                           _col_conv(spad, sw[:, 128:256], t0, blk)], axis=-1)
    col2 = jnp.concatenate([col, col], axis=-1)
    _local_tail(u, scb_ref[0], csc, pool_ref[0], col2, GRID_W, cb_ref[...], lg_ref[...],
                lb_ref[...], pw_ref[...], ps_ref[...], out_ref)


def _local_seq_kernel(u_ref, sc_ref, pool_ref, cw_ref, cb_ref, lg_ref, lb_ref, sw_ref,
                      pw_ref, ps_ref, out_ref):
    seq = u_ref.shape[1]
    col = lax.broadcasted_iota(jnp.int32, (seq, GROUP_W), 0)
    sc = sc_ref[0]
    u = _row_conv(u_ref[0], cw_ref[...], col, seq)
    csc = _row_conv(sc[:, 256:512], sw_ref[...], col, seq)
    _local_tail(u, sc[:, 0:256], csc, pool_ref[0], col, seq, cb_ref[...], lg_ref[...],
                lb_ref[...], pw_ref[...], ps_ref[...], out_ref)


def _local_params(p):
    return (p["conf_w"], p["conf_b"], p["conf_g"], p["conf_beta"], p["sc_w"], p["pool_bd"],
            p["pool_scale"])


def _local_axial(u, sc, pool, p, blk):
    bsz, seq, _ = u.shape
    cw, cb, lg, lb, sw, pw, ps = _local_params(p)
    blkspec = lambda w, j: pl.BlockSpec((1, blk, w), lambda b, i: (b, i, j))
    seqspec = lambda j: pl.BlockSpec((1, seq, LANE), lambda b, i: (b, 0, j))
    full = lambda a: pl.BlockSpec(a.shape, lambda b, i: (0,) * a.ndim)
    pu = (CONF_WIDTH // 2) * GRID_W
    psc = (SC_WIDTH // 2) * GRID_W
    return pl.pallas_call(
        _local_axial_kernel,
        grid=(bsz, seq // blk),
        in_specs=[blkspec(LANE, 0), seqspec(1), blkspec(256, 0), blkspec(LANE, 2), seqspec(3),
                  blkspec(256, 0), full(cw), full(cb), full(lg), full(lb), full(sw), full(pw),
                  full(ps)],
        out_specs=pl.BlockSpec((1, blk, 768), lambda b, i: (b, i, 0)),
        out_shape=jax.ShapeDtypeStruct((bsz, seq, 768), F32),
        scratch_shapes=[pltpu.VMEM((seq + 2 * pu, LANE), F32),
                        pltpu.VMEM((seq + 2 * psc, LANE), F32)],
        compiler_params=_cp(("parallel", "arbitrary")),
        name="local_axial",
    )(u, u, sc, sc, sc, pool, cw, cb, lg, lb, sw, pw, ps)


def _local_seq(u, sc, pool, p):
    bsz, seq, _ = u.shape
    cw, cb, lg, lb, sw, pw, ps = _local_params(p)
    tok = lambda w: pl.BlockSpec((1, seq, w), lambda b: (b, 0, 0))
    full = lambda a: pl.BlockSpec(a.shape, lambda b: (0,) * a.ndim)
    return pl.pallas_call(
        _local_seq_kernel,
        grid=(bsz,),
        in_specs=[tok(256), tok(512), tok(256), full(cw), full(cb), full(lg), full(lb),
                  full(sw), full(pw), full(ps)],
        out_specs=tok(768),
        out_shape=jax.ShapeDtypeStruct((bsz, seq, 768), F32),
        compiler_params=_cp(("parallel",)),
        name="local_seq",
    )(u, sc, pool, cw, cb, lg, lb, sw, pw, ps)


def _route(logits):
    lane = lax.broadcasted_iota(jnp.int32, logits.shape, 1)
    neg = -jnp.inf
    big = ROUTE_W

    def first_max(vals):
        mx = jnp.max(vals, axis=-1, keepdims=True)
        idx = jnp.min(jnp.where(vals == mx, lane, big), axis=-1, keepdims=True)
        return mx, idx

    lg = jnp.where(lane < N_GROUPS, logits, neg)
    gmx, grp = first_max(lg)
    p_grp = 1.0 / jnp.sum(jnp.exp(lg - gmx), axis=-1, keepdims=True)
    lo = N_GROUPS + grp * EXPERTS_PER_GROUP
    le = jnp.where((lane >= lo) & (lane < lo + EXPERTS_PER_GROUP), logits, neg)
    v1, i1 = first_max(le)
    v2, i2 = first_max(jnp.where(lane == i1, neg, le))
    e2 = jnp.exp(v2 - v1)
    w1 = p_grp / (1.0 + e2)
    w2 = p_grp * e2 / (1.0 + e2)
    rec = jnp.where(lane == 0, (i1 - N_GROUPS).astype(F32), 0.0)
    rec = jnp.where(lane == 1, (i2 - N_GROUPS).astype(F32), rec)
    rec = jnp.where(lane == 2, w1, rec)
    return jnp.where(lane == 3, w2, rec)


def _outproj_kernel(of_ref, ob_ref, g_ref, loc_ref, x_ref, mod_ref, gout_ref, hsum_ref,
                    wo_ref, g2_ref, wr_ref, br_ref, xo_ref, m_ref, route_ref):
    mod = mod_ref[0]
    o = of_ref[0] + ob_ref[0]
    o2 = o * o
    hi = o2.astype(BF16)
    lo = (o2 - hi.astype(F32)).astype(BF16)
    ms = (jnp.dot(hi, hsum_ref[...], preferred_element_type=F32)
          + jnp.dot(lo, hsum_ref[...], preferred_element_type=F32)) / GLA_DV
    y_gla = o * lax.rsqrt(ms + NORM_EPS) * gout_ref[...] * _silu(g_ref[0])
    wo = wo_ref[...]
    proj = _dot(y_gla, wo[0:256]) + _dot(loc_ref[0], wo[256:1024])
    x = x_ref[0] + mod[2:3] * proj
    xo_ref[0] = x
    m = _rms(x, g2_ref[...]) * (1.0 + mod[4:5]) + mod[3:4]
    m_ref[0] = m
    route_ref[0] = _route(_dot(m, wr_ref[...]) + br_ref[...])


def _outproj(o_f, o_b, qkvg, y_loc, x, mod, mod_row, p, tm):
    bsz, seq, d = x.shape
    row = (lambda b: b) if mod_row is None else (lambda b: mod_row)
    tok = lambda w, j=0: pl.BlockSpec((1, tm, w), lambda b, i: (b, i, j))
    full = lambda a: pl.BlockSpec(a.shape, lambda b, i: (0,) * a.ndim)
    consts = (p["g_out"], p["head_sum"], p["w_out"], p["g_norm2"], p["w_router"], p["b_router"])
    return pl.pallas_call(
        _outproj_kernel,
        grid=(bsz, seq // tm),
        in_specs=[tok(256), tok(256), tok(256, 2), tok(768), tok(d),
                  pl.BlockSpec((1, 6, d), lambda b, i: (row(b), 0, 0))] + [full(a) for a in consts],
        out_specs=[tok(d), tok(d), tok(ROUTE_W)],
        out_shape=[jax.ShapeDtypeStruct((bsz, seq, d), F32),
                   jax.ShapeDtypeStruct((bsz, seq, d), F32),
                   jax.ShapeDtypeStruct((bsz, seq, ROUTE_W), F32)],
        compiler_params=_cp(("parallel", "parallel")),
        name="outproj",
    )(o_f, o_b, qkvg, y_loc, x, mod, *consts)


def _moe_kernel(blk_e_ref, nvalid_ref, nused_ref, slot_tok_ref, slot_dst_ref,
                m_hbm, gate_ref, w1_ref, w2_ref, ya_hbm, xbuf, ybuf, gsem, ssem):
    i = pl.program_id(0)
    nused = nused_ref[0]
    nrow = MOE_BLOCK

    def gather_copy(blk, slot, r):
        tok = slot_tok_ref[blk * nrow + r]
        return pltpu.make_async_copy(m_hbm.at[pl.ds(tok, 1)], xbuf.at[slot, pl.ds(r, 1)],
                                     gsem.at[slot])

    def scatter_copy(blk, slot, r):
        dst = slot_dst_ref[blk * nrow + r]
        return pltpu.make_async_copy(ybuf.at[slot, pl.ds(r, 1)], ya_hbm.at[pl.ds(dst, 1)],
                                     ssem.at[slot])

    def for_rows(n, fn):
        def body(r, carry):
            fn(r)
            return carry
        lax.fori_loop(0, n, body, 0)

    @pl.when((i == 0) & (nused > 0))
    def _():
        for_rows(nrow, lambda r: gather_copy(0, 0, r).start())

    @pl.when(i + 1 < nused)
    def _():
        for_rows(nrow, lambda r: gather_copy(i + 1, (i + 1) % 2, r).start())

    @pl.when(i < nused)
    def _():
        slot = i % 2
        for_rows(nrow, lambda r: gather_copy(i, slot, r).wait())

        @pl.when(i >= 2)
        def _():
            for_rows(nvalid_ref[i - 2], lambda r: scatter_copy(i - 2, slot, r).wait())

        h = _dot(xbuf[slot], w1_ref[0])
        act = _silu(h[:, :D_EXPERT]) * h[:, D_EXPERT:]
        ybuf[slot] = _dot(act, w2_ref[0]) * gate_ref[0]
        for_rows(nvalid_ref[i], lambda r: scatter_copy(i, slot, r).start())

    @pl.when(i == pl.num_programs(0) - 1)
    def _():
        @pl.when(nused >= 1)
        def _():
            for_rows(nvalid_ref[nused - 1],
                     lambda r: scatter_copy(nused - 1, (nused - 1) % 2, r).wait())

        @pl.when(nused >= 2)
        def _():
            for_rows(nvalid_ref[nused - 2],
                     lambda r: scatter_copy(nused - 2, (nused - 2) % 2, r).wait())


def _moe(m_all, blk_expert, blk_valid, nused, slot_tok, slot_dst, slot_gate, w1, w2, n_rows_out):
    n_blocks = blk_expert.shape[0]
    d = m_all.shape[1]
    grid_spec = pltpu.PrefetchScalarGridSpec(
        num_scalar_prefetch=5,
        grid=(n_blocks,),
        in_specs=[pl.BlockSpec(memory_space=pl.ANY),
                  pl.BlockSpec((1, MOE_BLOCK, 1), lambda i, be, *_: (i, 0, 0)),
                  pl.BlockSpec((1, d, 2 * D_EXPERT), lambda i, be, *_: (be[i], 0, 0)),
                  pl.BlockSpec((1, D_EXPERT, d), lambda i, be, *_: (be[i], 0, 0))],
        out_specs=pl.BlockSpec(memory_space=pl.ANY),
        scratch_shapes=[pltpu.VMEM((2, MOE_BLOCK, d), F32),
                        pltpu.VMEM((2, MOE_BLOCK, d), F32),
                        pltpu.SemaphoreType.DMA((2,)),
                        pltpu.SemaphoreType.DMA((2,))],
    )
    return pl.pallas_call(
        _moe_kernel,
        grid_spec=grid_spec,
        out_shape=jax.ShapeDtypeStruct((n_rows_out, d), F32),
        compiler_params=_cp(("arbitrary",)),
        name="moe_experts",
    )(blk_expert, blk_valid, nused, slot_tok, slot_dst, m_all,
      slot_gate.reshape(n_blocks, MOE_BLOCK, 1), w1, w2)


def _dispatch(route, n_tok):
    eid = route[:, 0:2].astype(jnp.int32).reshape(-1)
    gate = route[:, 2:4].reshape(-1)
    a = eid.shape[0]
    order = jnp.argsort(eid)
    e_s = eid[order]
    counts = jnp.zeros((N_EXPERTS,), jnp.int32).at[eid].add(1)
    starts = jnp.cumsum(counts) - counts
    padded = (counts + MOE_BLOCK - 1) // MOE_BLOCK * MOE_BLOCK
    pad_end = jnp.cumsum(padded)
    pad_start = pad_end - padded
    dest = pad_start[e_s] + jnp.arange(a, dtype=jnp.int32) - starts[e_s]
    n_blocks = a // MOE_BLOCK + N_EXPERTS
    slots = n_blocks * MOE_BLOCK
    tok_s = order // 2
    row_s = (order % 2) * n_tok + tok_s
    slot_tok = jnp.zeros((slots,), jnp.int32).at[dest].set(tok_s.astype(jnp.int32))
    slot_dst = jnp.zeros((slots,), jnp.int32).at[dest].set(row_s.astype(jnp.int32))
    slot_gate = jnp.zeros((slots,), F32).at[dest].set(gate[order])
    blk_expert = jnp.minimum(
        jnp.searchsorted(pad_end, jnp.arange(n_blocks, dtype=jnp.int32) * MOE_BLOCK, side="right"),
        N_EXPERTS - 1).astype(jnp.int32)
    blk_start = jnp.arange(n_blocks, dtype=jnp.int32) * MOE_BLOCK
    blk_valid = jnp.clip(pad_start[blk_expert] + counts[blk_expert] - blk_start, 0, MOE_BLOCK)
    nused = (pad_end[-1] // MOE_BLOCK).astype(jnp.int32).reshape(1)
    blk_valid = jnp.where(blk_start < pad_end[-1], blk_valid, 0).astype(jnp.int32)
    return blk_expert, blk_valid, nused, slot_tok, slot_dst, slot_gate


def _combine_kernel(x_ref, y0_ref, y1_ref, mod_ref, gf_ref, o_ref, *, final):
    x = x_ref[0] + mod_ref[0][5:6] * (y0_ref[...] + y1_ref[...])
    o_ref[0] = _rms(x, gf_ref[...]) if final else x


def _combine(x, ya, tok_off, n_tok, mod, mod_row, g_final, final, tm):
    bsz, seq, d = x.shape
    nt = seq // tm
    row = (lambda b: b) if mod_row is None else (lambda b: mod_row)
    assert tok_off % tm == 0 and n_tok % tm == 0 and seq % tm == 0
    b0 = tok_off // tm
    b1 = (n_tok + tok_off) // tm
    return pl.pallas_call(
        functools.partial(_combine_kernel, final=final),
        grid=(bsz, nt),
        in_specs=[pl.BlockSpec((1, tm, d), lambda b, i: (b, i, 0)),
                  pl.BlockSpec((tm, d), lambda b, i: (b0 + b * nt + i, 0)),
                  pl.BlockSpec((tm, d), lambda b, i: (b1 + b * nt + i, 0)),
                  pl.BlockSpec((1, 6, d), lambda b, i: (row(b), 0, 0)),
                  pl.BlockSpec((1, d), lambda b, i: (0, 0))],
        out_specs=pl.BlockSpec((1, tm, d), lambda b, i: (b, i, 0)),
        out_shape=jax.ShapeDtypeStruct((bsz, seq, d), F32),
        compiler_params=_cp(("parallel", "parallel")),
        name="combine",
    )(x, ya, ya, mod, g_final)


def _layer_params(l, w_in, gla_w_gate_f, gla_b_gate_f, gla_w_gate_b, gla_b_gate_b, gla_g_out,
                  conf_w_dw, conf_b_dw, conf_ln_g, conf_ln_b, sc_w_dw, pool_w, pool_scale, w_out,
                  router_w_group, router_b_group, router_w_expert, router_b_expert,
                  expert_w_in, expert_w_out, g_norm1, g_norm2):
    d = D_MODEL
    idx = np.cumsum((0,) + IN_SPLITS)
    q, k, v, g, zf, zb, conf, scb, scc, scx, pool = [w_in[l][:, idx[j]:idx[j + 1]] for j in range(11)]
    pad = jnp.zeros((d, N_IN_PAD - idx[-1]), F32)
    w_in_p = jnp.concatenate([q, k, v, g, conf, scb, scc, scx, pool, zf, zb, pad], axis=1).astype(BF16)
    r = GLA_GATE_RANK
    w_gate = jnp.zeros((LANE, 2 * GLA_QK_W), F32)
    w_gate = w_gate.at[0:r, 0:GLA_QK_W].set(gla_w_gate_f[l]).at[r:2 * r, GLA_QK_W:].set(gla_w_gate_b[l])
    b_gate = jnp.concatenate([gla_b_gate_f[l], gla_b_gate_b[l]])[None]
    pool_bd = jnp.zeros((GROUP_W, GROUP_W), F32)
    for gi in range(len(POOL_WINDOWS)):
        s = slice(gi * POOL_GW, (gi + 1) * POOL_GW)
        pool_bd = pool_bd.at[s, s].set(pool_w[l, gi])
    head = np.arange(GROUP_W) // GLA_DV
    head_sum = jnp.asarray(head[:, None] == head[None, :], BF16)
    w_router = jnp.zeros((d, ROUTE_W), F32)
    w_router = w_router.at[:, 0:N_GROUPS].set(router_w_group[l])
    w_router = w_router.at[:, N_GROUPS:N_GROUPS + N_EXPERTS].set(router_w_expert[l])
    b_router = jnp.zeros((1, ROUTE_W), F32)
    b_router = b_router.at[0, 0:N_GROUPS].set(router_b_group[l])
    b_router = b_router.at[0, N_GROUPS:N_GROUPS + N_EXPERTS].set(router_b_expert[l])
    return dict(
        w_in=w_in_p, w_gate=w_gate.astype(BF16), b_gate=b_gate,
        g_norm1=g_norm1[l][None], g_norm2=g_norm2[l][None],
        g_out=gla_g_out[l][None], head_sum=head_sum,
        conf_w=conf_w_dw[l], conf_b=conf_b_dw[l][None], conf_g=conf_ln_g[l][None],
        conf_beta=conf_ln_b[l][None], sc_w=sc_w_dw[l], pool_bd=pool_bd.astype(BF16),
        pool_scale=pool_scale[l][None], w_out=w_out[l].astype(BF16),
        w_router=w_router.astype(BF16), b_router=b_router,
        w_e_in=expert_w_in[l].astype(BF16), w_e_out=expert_w_out[l].astype(BF16))


def kernel(x, c, ctx, c_ctx, w_mod, b_mod, g_norm1, g_norm2, w_in, gla_w_gate_f, gla_b_gate_f,
           gla_w_gate_b, gla_b_gate_b, gla_g_out, conf_w_dw, conf_b_dw, conf_ln_g, conf_ln_b,
           sc_w_dw, pool_w, pool_scale, w_out, router_w_group, router_b_group, router_w_expert,
           router_b_expert, expert_w_in, expert_w_out, g_final):
    bsz, seq, d = x.shape
    ctx_len = ctx.shape[1]
    depth = w_mod.shape[0]
    n_lat = bsz * seq
    n_ctx = bsz * ctx_len
    assert d == D_MODEL and seq % 1024 == 0 and ctx_len % GLA_GROUP == 0 and bsz <= 7
    tm = 512
    tm_c = 256

    cv = jnp.zeros((8, d), F32).at[0:bsz].set(c).at[bsz].set(c_ctx)
    mods = _adaln(cv, w_mod, b_mod).reshape(depth, 8, 6, d)
    gf = g_final[None]
    zero_state = jnp.zeros((bsz, 2, GLA_QK_W, GROUP_W), F32)

    for l in range(depth):
        last = l == depth - 1
        p = _layer_params(l, w_in, gla_w_gate_f, gla_b_gate_f, gla_w_gate_b, gla_b_gate_b,
                          gla_g_out, conf_w_dw, conf_b_dw, conf_ln_g, conf_ln_b, sc_w_dw, pool_w,
                          pool_scale, w_out, router_w_group, router_b_group, router_w_expert,
                          router_b_expert, expert_w_in, expert_w_out, g_norm1, g_norm2)
        mod = mods[l]

        c_qkvg, c_la, c_u, c_sc, c_pool = _inproj(ctx, mod, bsz, p["g_norm1"], p["w_in"],
                                                  p["w_gate"], p["b_gate"], tm_c)
        c_of, c_ob, s_ctx = _gla(c_qkvg, c_la, zero_state, ctx_len)
        x_qkvg, x_la, x_u, x_sc, x_pool = _inproj(x, mod, None, p["g_norm1"], p["w_in"],
                                                  p["w_gate"], p["b_gate"], tm)
        x_of, x_ob, _ = _gla(x_qkvg, x_la, s_ctx, 512)
        x_loc = _local_axial(x_u, x_sc, x_pool, p, 1024)
        x, m_x, r_x = _outproj(x_of, x_ob, x_qkvg, x_loc, x, mod, None, p, tm)

        if last:
            m_all = m_x.reshape(n_lat, d)
            route = r_x.reshape(n_lat, ROUTE_W)
            n_tok = n_lat
        else:
            c_loc = _local_seq(c_u, c_sc, c_pool, p)
            ctx, m_c, r_c = _outproj(c_of, c_ob, c_qkvg, c_loc, ctx, mod, bsz, p, tm_c)
            m_all = jnp.concatenate([m_x.reshape(n_lat, d), m_c.reshape(n_ctx, d)], axis=0)
            route = jnp.concatenate([r_x.reshape(n_lat, ROUTE_W), r_c.reshape(n_ctx, ROUTE_W)], axis=0)
            n_tok = n_lat + n_ctx

        blk_expert, blk_valid, nused, slot_tok, slot_dst, slot_gate = _dispatch(route, n_tok)
        ya = _moe(m_all, blk_expert, blk_valid, nused, slot_tok, slot_dst, slot_gate,
                  p["w_e_in"], p["w_e_out"], 2 * n_tok)
        x = _combine(x, ya, 0, n_tok, mod, None, gf, last, tm if n_tok % tm == 0 else tm_c)
        if not last:
            ctx = _combine(ctx, ya, n_lat, n_tok, mod, bsz, gf, False, tm_c)
    return x
```

```python
import functools

import numpy as np
import jax
import jax.numpy as jnp
from jax import lax
from jax.experimental import pallas as pl
from jax.experimental.pallas import tpu as pltpu

F32 = jnp.float32
BF16 = jnp.bfloat16

D_MODEL = 1024
GRID_W = 64
GROUP_W = 256
GLA_HEADS = 4
GLA_DV = 64
GLA_DK = 32
GLA_QK_W = 128
GLA_GATE_RANK = 16
GLA_GATE_NORM = 16.0
GLA_CHUNK = 64
GLA_GROUP = 256
CONF_WIDTH = 31
SC_WIDTH = 3
POOL_WINDOWS = (2, 4, 8, 16)
POOL_GW = 64
N_GROUPS = 4
EXPERTS_PER_GROUP = 8
N_EXPERTS = 32
D_EXPERT = 512
MOE_BLOCK = 256
NORM_EPS = 1e-6
IN_SPLITS = (128, 128, 256, 256, 16, 16, 512, 256, 256, 256, 256)
N_IN_PAD = 2432
LANE = 128
ROUTE_W = LANE

VMEM_LIMIT = 48 << 20


def _cp(sem, vmem=VMEM_LIMIT):
    return pltpu.CompilerParams(dimension_semantics=sem, vmem_limit_bytes=vmem)


def _sigmoid(x):
    return 1.0 / (1.0 + jnp.exp(-x))


def _silu(x):
    return x * _sigmoid(x)


def _log_sigmoid(x):
    return jnp.minimum(x, 0.0) - jnp.log1p(jnp.exp(-jnp.abs(x)))


def _dot(a, b):
    return jnp.dot(a.astype(BF16), b.astype(BF16), preferred_element_type=F32)


def _rms(x, g):
    return x * lax.rsqrt(jnp.mean(x * x, axis=-1, keepdims=True) + NORM_EPS) * g


def _adaln_kernel(cv_ref, w_ref, b_ref, o_ref):
    o_ref[0] = _dot(_silu(cv_ref[...]), w_ref[0]) + b_ref[0]


def _adaln(cv, w_mod, b_mod):
    depth, d, n = w_mod.shape
    tn = 1024
    return pl.pallas_call(
        _adaln_kernel,
        grid=(depth, n // tn),
        in_specs=[pl.BlockSpec((8, d), lambda l, j: (0, 0)),
                  pl.BlockSpec((1, d, tn), lambda l, j: (l, 0, j)),
                  pl.BlockSpec((1, 1, tn), lambda l, j: (l, 0, j))],
        out_specs=pl.BlockSpec((1, 8, tn), lambda l, j: (l, 0, j)),
        out_shape=jax.ShapeDtypeStruct((depth, 8, n), F32),
        compiler_params=_cp(("parallel", "parallel")),
        name="adaln",
    )(cv, w_mod, b_mod.reshape(depth, 1, n))


def _inproj_kernel(x_ref, mod_ref, g1_ref, w_ref, wg_ref, bg_ref,
                   qkvg_ref, la_ref, u_ref, sc_ref, pool_ref):
    x = x_ref[0]
    mod = mod_ref[0]
    m = _rms(x, g1_ref[...]) * (1.0 + mod[1:2]) + mod[0:1]
    h = _dot(m, w_ref[...])
    qkvg_ref[0] = h[:, 0:768]
    u_ref[0] = h[:, 768:1024] * _sigmoid(h[:, 1024:1280])
    sc_ref[0, :, 0:256] = h[:, 1280:1536]
    sc_ref[0, :, 256:512] = h[:, 1536:1792] * h[:, 1792:2048]
    pool_ref[0] = h[:, 2048:2304]
    z = _dot(h[:, 2304:2432], wg_ref[...]) + bg_ref[...]
    la_ref[0] = _log_sigmoid(z) / GLA_GATE_NORM


def _inproj(x, mod, mod_row, g1, w_in_p, w_gate, b_gate, tm):
    bsz, seq, d = x.shape
    row = (lambda b: b) if mod_row is None else (lambda b: mod_row)
    tok = lambda w: pl.BlockSpec((1, tm, w), lambda b, i: (b, i, 0))
    full = lambda a: pl.BlockSpec(a.shape, lambda b, i: (0,) * a.ndim)
    outs = (768, 256, 256, 512, 256)
    return pl.pallas_call(
        _inproj_kernel,
        grid=(bsz, seq // tm),
        in_specs=[tok(d),
                  pl.BlockSpec((1, 6, d), lambda b, i: (row(b), 0, 0)),
                  full(g1), full(w_in_p), full(w_gate), full(b_gate)],
        out_specs=[tok(w) for w in outs],
        out_shape=[jax.ShapeDtypeStruct((bsz, seq, w), F32) for w in outs],
        compiler_params=_cp(("parallel", "parallel")),
        name="inproj",
    )(x, mod, g1, w_in_p, w_gate, b_gate)


def _gla_direction(q, k, v, la, state, reverse):
    blk = q.shape[0]
    g = GLA_GROUP
    nchunk = g // GLA_CHUNK
    ti = lax.broadcasted_iota(jnp.int32, (g, g), 0)
    tj = lax.broadcasted_iota(jnp.int32, (g, g), 1)
    same = (ti // GLA_CHUNK) == (tj // GLA_CHUNK)
    cmask = same & ((tj >= ti) if reverse else (tj <= ti))
    tri = jnp.where(cmask, 1.0, 0.0).astype(BF16)
    qk_head = lax.broadcasted_iota(jnp.int32, (g, GLA_QK_W), 1) // GLA_DK
    v_head = lax.broadcasted_iota(jnp.int32, (g, GROUP_W), 1) // GLA_DV
    bd = (lax.broadcasted_iota(jnp.int32, (GLA_QK_W, GROUP_W), 0) // GLA_DK
          == lax.broadcasted_iota(jnp.int32, (GLA_QK_W, GROUP_W), 1) // GLA_DV)
    tok_chunk = lax.broadcasted_iota(jnp.int32, (GLA_QK_W, g), 1) // GLA_CHUNK
    scale = GLA_DK ** -0.5
    outs = [None] * (blk // g)
    groups = range(blk // g)
    for gi in (reversed(groups) if reverse else groups):
        sl = slice(gi * g, (gi + 1) * g)
        qg, kg, vg, lg = q[sl], k[sl], v[sl], la[sl]
        l0 = lg.astype(BF16)
        r0 = lg - l0.astype(F32)
        l1 = r0.astype(BF16)
        l2 = (r0 - l1.astype(F32)).astype(BF16)
        b = (jnp.dot(tri, l0, preferred_element_type=F32)
             + jnp.dot(tri, l1, preferred_element_type=F32)
             + jnp.dot(tri, l2, preferred_element_type=F32))
        last = 0 if reverse else GLA_CHUNK - 1
        blast = jnp.concatenate(
            [jnp.broadcast_to(b[c * GLA_CHUNK + last:c * GLA_CHUNK + last + 1], (GLA_CHUNK, GLA_QK_W))
             for c in range(nchunk)], axis=0)
        qd = (qg * scale) * jnp.exp(b)
        ki = (kg * jnp.exp(-b)).astype(BF16)
        kw_t = (kg * jnp.exp(blast - b)).T
        b_t = b.T
        vb = vg.astype(BF16)
        o = jnp.zeros((g, GROUP_W), F32)
        for h in range(GLA_HEADS):
            qh = jnp.where(qk_head == h, qd, 0.0).astype(BF16)
            s = lax.dot_general(qh, ki, (((1,), (1,)), ((), ())), preferred_element_type=F32)
            s = jnp.where(cmask, s, 0.0).astype(BF16)
            oh = jnp.dot(s, vb, preferred_element_type=F32)
            o = o + jnp.where(v_head == h, oh, 0.0)
        qdb = qd.astype(BF16)
        inter = [None] * nchunk
        chunks = range(nchunk)
        for c in (reversed(chunks) if reverse else chunks):
            rows = slice(c * GLA_CHUNK, (c + 1) * GLA_CHUNK)
            inter[c] = jnp.dot(qdb[rows], state.astype(BF16), preferred_element_type=F32)
            col = c * GLA_CHUNK + last
            dec = jnp.exp(b_t[:, col:col + 1])
            kv = jnp.dot(jnp.where(tok_chunk == c, kw_t, 0.0).astype(BF16), vb,
                         preferred_element_type=F32)
            state = dec * state + jnp.where(bd, kv, 0.0)
        outs[gi] = o + jnp.concatenate(inter, axis=0)
    return jnp.concatenate(outs, axis=0) if len(outs) > 1 else outs[0], state


def _gla_kernel(qk_f, v_f, la_f, qk_b, v_b, la_b, s0_ref, of_ref, ob_ref, sfin_ref, st_ref):
    i = pl.program_id(1)

    @pl.when(i == 0)
    def _():
        st_ref[...] = s0_ref[0]

    qk = qk_f[0]
    o, s = _gla_direction(qk[:, 0:128], qk[:, 128:256], v_f[0], la_f[0], st_ref[0], False)
    of_ref[0] = o
    st_ref[0] = s
    qk = qk_b[0]
    o, s = _gla_direction(qk[:, 0:128], qk[:, 128:256], v_b[0], la_b[0], st_ref[1], True)
    ob_ref[0] = o
    st_ref[1] = s

    @pl.when(i == pl.num_programs(1) - 1)
    def _():
        sfin_ref[0] = st_ref[...]


def _gla(qkvg, la, s0, blk):
    bsz, seq, _ = qkvg.shape
    nb = seq // blk
    fwd = lambda j: pl.BlockSpec((1, blk, 256), lambda b, i: (b, i, j))
    bwd = lambda j: pl.BlockSpec((1, blk, 256), lambda b, i: (b, nb - 1 - i, j))
    la_f = pl.BlockSpec((1, blk, 128), lambda b, i: (b, i, 0))
    la_b = pl.BlockSpec((1, blk, 128), lambda b, i: (b, nb - 1 - i, 1))
    st = pl.BlockSpec((1, 2, GLA_QK_W, GROUP_W), lambda b, i: (b, 0, 0, 0))
    return pl.pallas_call(
        _gla_kernel,
        grid=(bsz, nb),
        in_specs=[fwd(0), fwd(1), la_f, bwd(0), bwd(1), la_b, st],
        out_specs=[pl.BlockSpec((1, blk, GROUP_W), lambda b, i: (b, i, 0)),
                   pl.BlockSpec((1, blk, GROUP_W), lambda b, i: (b, nb - 1 - i, 0)),
                   st],
        out_shape=[jax.ShapeDtypeStruct((bsz, seq, GROUP_W), F32),
                   jax.ShapeDtypeStruct((bsz, seq, GROUP_W), F32),
                   jax.ShapeDtypeStruct((bsz, 2, GLA_QK_W, GROUP_W), F32)],
        scratch_shapes=[pltpu.VMEM((2, GLA_QK_W, GROUP_W), F32)],
        compiler_params=_cp(("parallel", "arbitrary")),
        name="gla",
    )(qkvg, qkvg, la, qkvg, qkvg, la, s0)


def _row_shift(x, s, col, row_len):
    if s == 0:
        return x
    n = x.shape[0]
    y = pltpu.roll(x, (-s) % n, 0)
    valid = (col < row_len - s) if s > 0 else (col >= -s)
    return jnp.where(valid, y, 0.0)


def _row_conv(x, w, col, row_len):
    k = w.shape[0]
    acc = None
    for j in range(k):
        term = _row_shift(x, j - k // 2, col, row_len) * w[j:j + 1]
        acc = term if acc is None else acc + term
    return acc


def _col_conv(pad_ref, w, t0, n):
    k = w.shape[0]
    acc = None
    for j in range(k):
        start = pl.multiple_of(t0 + j * GRID_W, GRID_W)
        term = pad_ref[pl.ds(start, n), :] * w[j:j + 1]
        acc = term if acc is None else acc + term
    return acc


def _pool_diffs(x, col, row_len):
    lane = lax.broadcasted_iota(jnp.int32, x.shape, 1)
    colf = col.astype(F32)
    mean = jnp.zeros_like(x)
    for gi, win in enumerate(POOL_WINDOWS):
        half = win // 2
        acc = None
        for s in range(-half, half):
            t = _row_shift(x, s, col, row_len)
            acc = t if acc is None else acc + t
        cnt = jnp.minimum(colf + half, float(row_len)) - jnp.maximum(colf - half, 0.0)
        mean = jnp.where(lane // POOL_GW == gi, acc / cnt, mean)
    return mean - x


def _local_tail(u, scb, conv_sc, pool, col, row_len, cb, lg, lb, pw, ps, out_ref):
    u = u + cb
    uc = u - jnp.mean(u, axis=-1, keepdims=True)
    ln = uc * lax.rsqrt(jnp.mean(uc * uc, axis=-1, keepdims=True) + NORM_EPS) * lg + lb
    out_ref[0, :, 0:256] = _silu(ln)
    out_ref[0, :, 256:512] = scb * conv_sc
    out_ref[0, :, 512:768] = _dot(_pool_diffs(pool, col, row_len), pw) * ps


def _local_axial_kernel(uh_ref, uv_ref, scb_ref, sch_ref, scv_ref, pool_ref,
                        cw_ref, cb_ref, lg_ref, lb_ref, sw_ref, pw_ref, ps_ref,
                        out_ref, upad, spad):
    i = pl.program_id(1)
    blk = uh_ref.shape[1]
    seq = uv_ref.shape[1]
    pu = (CONF_WIDTH // 2) * GRID_W
    psc = (SC_WIDTH // 2) * GRID_W

    @pl.when(i == 0)
    def _():
        upad[0:pu, :] = jnp.zeros((pu, LANE), F32)
        upad[pu:pu + seq, :] = uv_ref[0]
        upad[pu + seq:pu + seq + pu, :] = jnp.zeros((pu, LANE), F32)
        spad[0:psc, :] = jnp.zeros((psc, LANE), F32)
        spad[psc:psc + seq, :] = scv_ref[0]
        spad[psc + seq:psc + seq + psc, :] = jnp.zeros((psc, LANE), F32)

    t0 = i * blk
    col = lax.broadcasted_iota(jnp.int32, (blk, LANE), 0) % GRID_W
    cw = cw_ref[...]
    sw = sw_ref[...]
    u = jnp.concatenate([_row_conv(uh_ref[0], cw[:, 0:128], col, GRID_W),
                         _col_conv(upad, cw[:, 128:256], t0, blk)], axis=-1)
    csc = jnp.concatenate([_row_conv(sch_ref[0], sw[:, 0:128], col, GRID_W),
                           _col_conv(spad, sw[:, 128:256], t0, blk)], axis=-1)
    col2 = jnp.concatenate([col, col], axis=-1)
    _local_tail(u, scb_ref[0], csc, pool_ref[0], col2, GRID_W, cb_ref[...], lg_ref[...],
                lb_ref[...], pw_ref[...], ps_ref[...], out_ref)


def _local_seq_kernel(u_ref, sc_ref, pool_ref, cw_ref, cb_ref, lg_ref, lb_ref, sw_ref,
                      pw_ref, ps_ref, out_ref):
    seq = u_ref.shape[1]
    col = lax.broadcasted_iota(jnp.int32, (seq, GROUP_W), 0)
    sc = sc_ref[0]
    u = _row_conv(u_ref[0], cw_ref[...], col, seq)
    csc = _row_conv(sc[:, 256:512], sw_ref[...], col, seq)
    _local_tail(u, sc[:, 0:256], csc, pool_ref[0], col, seq, cb_ref[...], lg_ref[...],
                lb_ref[...], pw_ref[...], ps_ref[...], out_ref)


def _local_params(p):
    return (p["conf_w"], p["conf_b"], p["conf_g"], p["conf_beta"], p["sc_w"], p["pool_bd"],
            p["pool_scale"])


def _local_axial(u, sc, pool, p, blk):
    bsz, seq, _ = u.shape
    cw, cb, lg, lb, sw, pw, ps = _local_params(p)
    blkspec = lambda w, j: pl.BlockSpec((1, blk, w), lambda b, i: (b, i, j))
    seqspec = lambda j: pl.BlockSpec((1, seq, LANE), lambda b, i: (b, 0, j))
    full = lambda a: pl.BlockSpec(a.shape, lambda b, i: (0,) * a.ndim)
    pu = (CONF_WIDTH // 2) * GRID_W
    psc = (SC_WIDTH // 2) * GRID_W
    return pl.pallas_call(
        _local_axial_kernel,
        grid=(bsz, seq // blk),
        in_specs=[blkspec(LANE, 0), seqspec(1), blkspec(256, 0), blkspec(LANE, 2), seqspec(3),
                  blkspec(256, 0), full(cw), full(cb), full(lg), full(lb), full(sw), full(pw),
                  full(ps)],
        out_specs=pl.BlockSpec((1, blk, 768), lambda b, i: (b, i, 0)),
        out_shape=jax.ShapeDtypeStruct((bsz, seq, 768), F32),
        scratch_shapes=[pltpu.VMEM((seq + 2 * pu, LANE), F32),
                        pltpu.VMEM((seq + 2 * psc, LANE), F32)],
        compiler_params=_cp(("parallel", "arbitrary")),
        name="local_axial",
    )(u, u, sc, sc, sc, pool, cw, cb, lg, lb, sw, pw, ps)


def _local_seq(u, sc, pool, p):
    bsz, seq, _ = u.shape
    cw, cb, lg, lb, sw, pw, ps = _local_params(p)
    tok = lambda w: pl.BlockSpec((1, seq, w), lambda b: (b, 0, 0))
    full = lambda a: pl.BlockSpec(a.shape, lambda b: (0,) * a.ndim)
    return pl.pallas_call(
        _local_seq_kernel,
        grid=(bsz,),
        in_specs=[tok(256), tok(512), tok(256), full(cw), full(cb), full(lg), full(lb),
                  full(sw), full(pw), full(ps)],
        out_specs=tok(768),
        out_shape=jax.ShapeDtypeStruct((bsz, seq, 768), F32),
        compiler_params=_cp(("parallel",)),
        name="local_seq",
    )(u, sc, pool, cw, cb, lg, lb, sw, pw, ps)


def _route(logits):
    lane = lax.broadcasted_iota(jnp.int32, logits.shape, 1)
    neg = -jnp.inf
    big = ROUTE_W

    def first_max(vals):
        mx = jnp.max(vals, axis=-1, keepdims=True)
        idx = jnp.min(jnp.where(vals == mx, lane, big), axis=-1, keepdims=True)
        return mx, idx

    lg = jnp.where(lane < N_GROUPS, logits, neg)
    gmx, grp = first_max(lg)
    p_grp = 1.0 / jnp.sum(jnp.exp(lg - gmx), axis=-1, keepdims=True)
    lo = N_GROUPS + grp * EXPERTS_PER_GROUP
    le = jnp.where((lane >= lo) & (lane < lo + EXPERTS_PER_GROUP), logits, neg)
    v1, i1 = first_max(le)
    v2, i2 = first_max(jnp.where(lane == i1, neg, le))
    e2 = jnp.exp(v2 - v1)
    w1 = p_grp / (1.0 + e2)
    w2 = p_grp * e2 / (1.0 + e2)
    rec = jnp.where(lane == 0, (i1 - N_GROUPS).astype(F32), 0.0)
    rec = jnp.where(lane == 1, (i2 - N_GROUPS).astype(F32), rec)
    rec = jnp.where(lane == 2, w1, rec)
    return jnp.where(lane == 3, w2, rec)


def _outproj_kernel(of_ref, ob_ref, g_ref, loc_ref, x_ref, mod_ref, gout_ref, hsum_ref,
                    wo_ref, g2_ref, wr_ref, br_ref, xo_ref, m_ref, route_ref):
    mod = mod_ref[0]
    o = of_ref[0] + ob_ref[0]
    o2 = o * o
    hi = o2.astype(BF16)
    lo = (o2 - hi.astype(F32)).astype(BF16)
    ms = (jnp.dot(hi, hsum_ref[...], preferred_element_type=F32)
          + jnp.dot(lo, hsum_ref[...], preferred_element_type=F32)) / GLA_DV
    y_gla = o * lax.rsqrt(ms + NORM_EPS) * gout_ref[...] * _silu(g_ref[0])
    wo = wo_ref[...]
    proj = _dot(y_gla, wo[0:256]) + _dot(loc_ref[0], wo[256:1024])
    x = x_ref[0] + mod[2:3] * proj
    xo_ref[0] = x
    m = _rms(x, g2_ref[...]) * (1.0 + mod[4:5]) + mod[3:4]
    m_ref[0] = m
    route_ref[0] = _route(_dot(m, wr_ref[...]) + br_ref[...])


def _outproj(o_f, o_b, qkvg, y_loc, x, mod, mod_row, p, tm):
    bsz, seq, d = x.shape
    row = (lambda b: b) if mod_row is None else (lambda b: mod_row)
    tok = lambda w, j=0: pl.BlockSpec((1, tm, w), lambda b, i: (b, i, j))
    full = lambda a: pl.BlockSpec(a.shape, lambda b, i: (0,) * a.ndim)
    consts = (p["g_out"], p["head_sum"], p["w_out"], p["g_norm2"], p["w_router"], p["b_router"])
    return pl.pallas_call(
        _outproj_kernel,
        grid=(bsz, seq // tm),
        in_specs=[tok(256), tok(256), tok(256, 2), tok(768), tok(d),
                  pl.BlockSpec((1, 6, d), lambda b, i: (row(b), 0, 0))] + [full(a) for a in consts],
        out_specs=[tok(d), tok(d), tok(ROUTE_W)],
        out_shape=[jax.ShapeDtypeStruct((bsz, seq, d), F32),
                   jax.ShapeDtypeStruct((bsz, seq, d), F32),
                   jax.ShapeDtypeStruct((bsz, seq, ROUTE_W), F32)],
        compiler_params=_cp(("parallel", "parallel")),
        name="outproj",
    )(o_f, o_b, qkvg, y_loc, x, mod, *consts)


RANK_ROWS = 512


def _rank_kernel(route_ref, rank_ref, cnt_ref, carry):
    i = pl.program_id(0)

    @pl.when(i == 0)
    def _():
        carry[...] = jnp.zeros_like(carry)

    rec = route_ref[...]
    tm = rec.shape[0]
    lane = lax.broadcasted_iota(jnp.int32, rec.shape, 1).astype(F32)
    oh0 = jnp.where(lane == rec[:, 0:1], 1.0, 0.0)
    oh1 = jnp.where(lane == rec[:, 1:2], 1.0, 0.0)
    ti = lax.broadcasted_iota(jnp.int32, (tm, tm), 0)
    tj = lax.broadcasted_iota(jnp.int32, (tm, tm), 1)
    before = jnp.where(tj < ti, 1.0, 0.0).astype(BF16)
    tot0 = jnp.sum(oh0, axis=0, keepdims=True)
    tot1 = jnp.sum(oh1, axis=0, keepdims=True)
    base = carry[...]
    pre0 = jnp.dot(before, oh0.astype(BF16), preferred_element_type=F32) + base
    pre1 = jnp.dot(before, oh1.astype(BF16), preferred_element_type=F32) + (base + tot0)
    r0 = jnp.sum(oh0 * pre0, axis=-1, keepdims=True)
    r1 = jnp.sum(oh1 * pre1, axis=-1, keepdims=True)
    lane_i = lax.broadcasted_iota(jnp.int32, rec.shape, 1)
    rank_ref[...] = jnp.where(lane_i == 0, r0, jnp.where(lane_i == 1, r1, 0.0)).astype(jnp.int32)
    carry[...] = base + tot0 + tot1
    cnt_ref[...] = carry[...]


def _rank(route):
    n_tok = route.shape[0]
    tm = RANK_ROWS if n_tok % RANK_ROWS == 0 else 256
    return pl.pallas_call(
        _rank_kernel,
        grid=(n_tok // tm,),
        in_specs=[pl.BlockSpec((tm, ROUTE_W), lambda i: (i, 0))],
        out_specs=[pl.BlockSpec((tm, ROUTE_W), lambda i: (i, 0)),
                   pl.BlockSpec((1, ROUTE_W), lambda i: (0, 0))],
        out_shape=[jax.ShapeDtypeStruct((n_tok, ROUTE_W), jnp.int32),
                   jax.ShapeDtypeStruct((1, ROUTE_W), F32)],
        scratch_shapes=[pltpu.VMEM((1, ROUTE_W), F32)],
        compiler_params=_cp(("arbitrary",)),
        name="rank",
    )(route)


def _plan(route):
    n_tok = route.shape[0]
    rank, cnt = _rank(route)
    counts = cnt[0, 0:N_EXPERTS].astype(jnp.int32)
    padded = (counts + MOE_BLOCK - 1) // MOE_BLOCK * MOE_BLOCK
    pad_end = jnp.cumsum(padded)
    pad_start = pad_end - padded
    eid = route[:, 0:2].astype(jnp.int32)
    onehot = eid[:, :, None] == jnp.arange(N_EXPERTS, dtype=jnp.int32)[None, None, :]
    dest = (jnp.sum(jnp.where(onehot, pad_start[None, None, :], 0), axis=-1) + rank[:, 0:2]).reshape(-1)
    n_blocks = 2 * n_tok // MOE_BLOCK + N_EXPERTS
    blk_start = jnp.arange(n_blocks, dtype=jnp.int32) * MOE_BLOCK
    nused = pad_end[-1] // MOE_BLOCK
    blk = jnp.minimum(jnp.arange(n_blocks, dtype=jnp.int32), nused - 1)
    blk_expert = jnp.sum((blk[:, None] * MOE_BLOCK >= pad_end[None, :]).astype(jnp.int32), axis=-1)
    fill_start = pad_start + counts
    fill_n = padded - counts
    return dict(dest=dest.astype(jnp.int32), blk=blk.astype(jnp.int32),
                blk_expert=jnp.minimum(blk_expert, N_EXPERTS - 1).astype(jnp.int32),
                nused=nused.astype(jnp.int32).reshape(1), fill_start=fill_start.astype(jnp.int32),
                fill_n=fill_n.astype(jnp.int32), n_blocks=n_blocks)


DISPATCH_ROWS = 512
DMA_UNROLL = 4


def _dispatch_kernel(dest_ref, fstart_ref, fn_ref, nused_ref, m_hbm, zero_hbm, xs_hbm, sem, fsem,
                     *, tr):
    i = pl.program_id(0)
    nsteps = pl.num_programs(0)

    def row_copy(t, k, slot):
        return pltpu.make_async_copy(m_hbm.at[pl.ds(t, 1)], xs_hbm.at[pl.ds(dest_ref[2 * t + k], 1)],
                                     sem.at[slot])

    def issue(j, carry):
        for u in range(DMA_UNROLL):
            t = i * tr + j * DMA_UNROLL + u
            row_copy(t, 0, i % 2).start()
            row_copy(t, 1, i % 2).start()
        return carry

    lax.fori_loop(0, tr // DMA_UNROLL, issue, 0)

    def wait_step(slot):
        pltpu.make_async_copy(m_hbm.at[pl.ds(0, 2 * tr)], xs_hbm.at[pl.ds(0, 2 * tr)],
                              sem.at[slot]).wait()

    @pl.when(i >= 1)
    def _():
        wait_step((i - 1) % 2)

    @pl.when(i == nsteps - 1)
    def _():
        wait_step(i % 2)

        def fill_copy(e, r):
            return pltpu.make_async_copy(zero_hbm.at[pl.ds(0, 1)],
                                         xs_hbm.at[pl.ds(fstart_ref[e] + r, 1)], fsem.at[0])

        def per_expert(e, carry):
            lax.fori_loop(0, fn_ref[e], lambda r, c: (fill_copy(e, r).start(), c)[1], 0)
            return carry

        def per_expert_wait(e, carry):
            lax.fori_loop(0, fn_ref[e], lambda r, c: (fill_copy(e, r).wait(), c)[1], 0)
            return carry

        lax.fori_loop(0, N_EXPERTS, per_expert, 0)
        lax.fori_loop(0, N_EXPERTS, per_expert_wait, 0)

        def tail_copy(b):
            start = pl.multiple_of(b * MOE_BLOCK, MOE_BLOCK)
            return pltpu.make_async_copy(zero_hbm, xs_hbm.at[pl.ds(start, MOE_BLOCK)], fsem.at[0])

        nblk = xs_hbm.shape[0] // MOE_BLOCK
        lax.fori_loop(nused_ref[0], nblk, lambda b, c: (tail_copy(b).start(), c)[1], 0)
        lax.fori_loop(nused_ref[0], nblk, lambda b, c: (tail_copy(b).wait(), c)[1], 0)


def _dispatch(m_all, plan):
    n_tok, d = m_all.shape
    tr = DISPATCH_ROWS if n_tok % DISPATCH_ROWS == 0 else DISPATCH_ROWS // 2
    assert n_tok % tr == 0
    slots = plan["n_blocks"] * MOE_BLOCK
    grid_spec = pltpu.PrefetchScalarGridSpec(
        num_scalar_prefetch=4,
        grid=(n_tok // tr,),
        in_specs=[pl.BlockSpec(memory_space=pl.ANY), pl.BlockSpec(memory_space=pl.ANY)],
        out_specs=pl.BlockSpec(memory_space=pl.ANY),
        scratch_shapes=[pltpu.SemaphoreType.DMA((2,)), pltpu.SemaphoreType.DMA((1,))],
    )
    return pl.pallas_call(
        functools.partial(_dispatch_kernel, tr=tr),
        grid_spec=grid_spec,
        out_shape=jax.ShapeDtypeStruct((slots, d), F32),
        compiler_params=_cp(("arbitrary",)),
        name="dispatch",
    )(plan["dest"], plan["fill_start"], plan["fill_n"], plan["nused"], m_all,
      jnp.zeros((MOE_BLOCK, d), F32))


def _expert_kernel(blk_ref, be_ref, nused_ref, xs_ref, w1_ref, w2_ref, ys_ref):
    used = pl.program_id(0) < nused_ref[0]

    @pl.when(used)
    def _():
        h = _dot(xs_ref[...], w1_ref[0])
        act = _silu(h[:, :D_EXPERT]) * h[:, D_EXPERT:]
        ys_ref[...] = _dot(act, w2_ref[0])

    @pl.when(jnp.logical_not(used))
    def _():
        ys_ref[...] = jnp.zeros_like(ys_ref)


def _experts(xs, plan, w1, w2):
    slots, d = xs.shape
    grid_spec = pltpu.PrefetchScalarGridSpec(
        num_scalar_prefetch=3,
        grid=(plan["n_blocks"],),
        in_specs=[pl.BlockSpec((MOE_BLOCK, d), lambda i, blk, be, nu: (blk[i], 0)),
                  pl.BlockSpec((1, d, 2 * D_EXPERT), lambda i, blk, be, nu: (be[i], 0, 0)),
                  pl.BlockSpec((1, D_EXPERT, d), lambda i, blk, be, nu: (be[i], 0, 0))],
        out_specs=pl.BlockSpec((MOE_BLOCK, d), lambda i, blk, be, nu: (i, 0)),
    )
    return pl.pallas_call(
        _expert_kernel,
        grid_spec=grid_spec,
        out_shape=jax.ShapeDtypeStruct((slots, d), F32),
        compiler_params=_cp(("arbitrary",)),
        name="experts",
    )(plan["blk"], plan["blk_expert"], plan["nused"], xs, w1, w2)


COMBINE_ROWS = 256


def _combine_kernel(dest_ref, x_ref, route_ref, mod_ref, gf_ref, ys_hbm, o_ref, y0buf, y1buf, sem,
                    *, final, tok_off):
    i = pl.program_id(0)
    nsteps = pl.num_programs(0)
    tc = COMBINE_ROWS

    def row_copy(step, r, k, slot):
        a = 2 * (tok_off + step * tc + r) + k
        buf = y0buf if k == 0 else y1buf
        return pltpu.make_async_copy(ys_hbm.at[pl.ds(dest_ref[a], 1)], buf.at[slot, pl.ds(r, 1)],
                                     sem.at[slot])

    def issue_step(step):
        def body(j, carry):
            for u in range(DMA_UNROLL):
                r = j * DMA_UNROLL + u
                row_copy(step, r, 0, step % 2).start()
                row_copy(step, r, 1, step % 2).start()
            return carry
        lax.fori_loop(0, tc // DMA_UNROLL, body, 0)

    @pl.when(i == 0)
    def _():
        issue_step(0)

    @pl.when(i + 1 < nsteps)
    def _():
        issue_step(i + 1)

    slot = i % 2
    pltpu.make_async_copy(ys_hbm.at[pl.ds(0, tc)], y0buf.at[slot], sem.at[slot]).wait()
    pltpu.make_async_copy(ys_hbm.at[pl.ds(0, tc)], y1buf.at[slot], sem.at[slot]).wait()
    rec = route_ref[...]
    y = rec[:, 2:3] * y0buf[slot] + rec[:, 3:4] * y1buf[slot]
    x = x_ref[0] + mod_ref[0][5:6] * y
    o_ref[0] = _rms(x, gf_ref[...]) if final else x


def _combine(x, ys, route, plan, tok_off, mod, mod_row, g_final, final):
    bsz, seq, d = x.shape
    tc = COMBINE_ROWS
    nt = seq // tc
    assert seq % tc == 0 and tok_off % tc == 0
    row = (lambda i: i // nt) if mod_row is None else (lambda i: mod_row)
    grid_spec = pltpu.PrefetchScalarGridSpec(
        num_scalar_prefetch=1,
        grid=(bsz * nt,),
        in_specs=[pl.BlockSpec((1, tc, d), lambda i, de: (i // nt, i % nt, 0)),
                  pl.BlockSpec((tc, ROUTE_W), lambda i, de: (tok_off // tc + i, 0)),
                  pl.BlockSpec((1, 6, d), lambda i, de: (row(i), 0, 0)),
                  pl.BlockSpec((1, d), lambda i, de: (0, 0)),
                  pl.BlockSpec(memory_space=pl.ANY)],
        out_specs=pl.BlockSpec((1, tc, d), lambda i, de: (i // nt, i % nt, 0)),
        scratch_shapes=[pltpu.VMEM((2, tc, d), F32), pltpu.VMEM((2, tc, d), F32),
                        pltpu.SemaphoreType.DMA((2,))],
    )
    return pl.pallas_call(
        functools.partial(_combine_kernel, final=final, tok_off=tok_off),
        grid_spec=grid_spec,
        out_shape=jax.ShapeDtypeStruct((bsz, seq, d), F32),
        compiler_params=_cp(("arbitrary",)),
        name="combine",
    )(plan["dest"], x, route, mod, g_final, ys)


def _layer_params(l, w_in, gla_w_gate_f, gla_b_gate_f, gla_w_gate_b, gla_b_gate_b, gla_g_out,
                  conf_w_dw, conf_b_dw, conf_ln_g, conf_ln_b, sc_w_dw, pool_w, pool_scale, w_out,
                  router_w_group, router_b_group, router_w_expert, router_b_expert,
                  expert_w_in, expert_w_out, g_norm1, g_norm2):
    d = D_MODEL
    idx = np.cumsum((0,) + IN_SPLITS)
    q, k, v, g, zf, zb, conf, scb, scc, scx, pool = [w_in[l][:, idx[j]:idx[j + 1]] for j in range(11)]
    pad = jnp.zeros((d, N_IN_PAD - idx[-1]), F32)
    w_in_p = jnp.concatenate([q, k, v, g, conf, scb, scc, scx, pool, zf, zb, pad], axis=1).astype(BF16)
    r = GLA_GATE_RANK
    w_gate = jnp.zeros((LANE, 2 * GLA_QK_W), F32)
    w_gate = w_gate.at[0:r, 0:GLA_QK_W].set(gla_w_gate_f[l]).at[r:2 * r, GLA_QK_W:].set(gla_w_gate_b[l])
    b_gate = jnp.concatenate([gla_b_gate_f[l], gla_b_gate_b[l]])[None]
    pool_bd = jnp.zeros((GROUP_W, GROUP_W), F32)
    for gi in range(len(POOL_WINDOWS)):
        s = slice(gi * POOL_GW, (gi + 1) * POOL_GW)
        pool_bd = pool_bd.at[s, s].set(pool_w[l, gi])
    head = np.arange(GROUP_W) // GLA_DV
    head_sum = jnp.asarray(head[:, None] == head[None, :], BF16)
    w_router = jnp.zeros((d, ROUTE_W), F32)
    w_router = w_router.at[:, 0:N_GROUPS].set(router_w_group[l])
    w_router = w_router.at[:, N_GROUPS:N_GROUPS + N_EXPERTS].set(router_w_expert[l])
    b_router = jnp.zeros((1, ROUTE_W), F32)
    b_router = b_router.at[0, 0:N_GROUPS].set(router_b_group[l])
    b_router = b_router.at[0, N_GROUPS:N_GROUPS + N_EXPERTS].set(router_b_expert[l])
    return dict(
        w_in=w_in_p, w_gate=w_gate.astype(BF16), b_gate=b_gate,
        g_norm1=g_norm1[l][None], g_norm2=g_norm2[l][None],
        g_out=gla_g_out[l][None], head_sum=head_sum,
        conf_w=conf_w_dw[l], conf_b=conf_b_dw[l][None], conf_g=conf_ln_g[l][None],
        conf_beta=conf_ln_b[l][None], sc_w=sc_w_dw[l], pool_bd=pool_bd.astype(BF16),
        pool_scale=pool_scale[l][None], w_out=w_out[l].astype(BF16),
        w_router=w_router.astype(BF16), b_router=b_router,
        w_e_in=expert_w_in[l].astype(BF16), w_e_out=expert_w_out[l].astype(BF16))


def kernel(x, c, ctx, c_ctx, w_mod, b_mod, g_norm1, g_norm2, w_in, gla_w_gate_f, gla_b_gate_f,
           gla_w_gate_b, gla_b_gate_b, gla_g_out, conf_w_dw, conf_b_dw, conf_ln_g, conf_ln_b,
           sc_w_dw, pool_w, pool_scale, w_out, router_w_group, router_b_group, router_w_expert,
           router_b_expert, expert_w_in, expert_w_out, g_final):
    bsz, seq, d = x.shape
    ctx_len = ctx.shape[1]
    depth = w_mod.shape[0]
    n_lat = bsz * seq
    n_ctx = bsz * ctx_len
    assert d == D_MODEL and seq % 1024 == 0 and ctx_len % GLA_GROUP == 0 and bsz <= 7
    tm = 512
    tm_c = 256

    cv = jnp.zeros((8, d), F32).at[0:bsz].set(c).at[bsz].set(c_ctx)
    mods = _adaln(cv, w_mod, b_mod).reshape(depth, 8, 6, d)
    gf = g_final[None]
    zero_state = jnp.zeros((bsz, 2, GLA_QK_W, GROUP_W), F32)

    for l in range(depth):
        last = l == depth - 1
        p = _layer_params(l, w_in, gla_w_gate_f, gla_b_gate_f, gla_w_gate_b, gla_b_gate_b,
                          gla_g_out, conf_w_dw, conf_b_dw, conf_ln_g, conf_ln_b, sc_w_dw, pool_w,
                          pool_scale, w_out, router_w_group, router_b_group, router_w_expert,
                          router_b_expert, expert_w_in, expert_w_out, g_norm1, g_norm2)
        mod = mods[l]

        c_qkvg, c_la, c_u, c_sc, c_pool = _inproj(ctx, mod, bsz, p["g_norm1"], p["w_in"],
                                                  p["w_gate"], p["b_gate"], tm_c)
        c_of, c_ob, s_ctx = _gla(c_qkvg, c_la, zero_state, ctx_len)
        x_qkvg, x_la, x_u, x_sc, x_pool = _inproj(x, mod, None, p["g_norm1"], p["w_in"],
                                                  p["w_gate"], p["b_gate"], tm)
        x_of, x_ob, _ = _gla(x_qkvg, x_la, s_ctx, 512)
        x_loc = _local_axial(x_u, x_sc, x_pool, p, 1024)
        x, m_x, r_x = _outproj(x_of, x_ob, x_qkvg, x_loc, x, mod, None, p, tm)

        if last:
            m_all = m_x.reshape(n_lat, d)
            route = r_x.reshape(n_lat, ROUTE_W)
            n_tok = n_lat
        else:
            c_loc = _local_seq(c_u, c_sc, c_pool, p)
            ctx, m_c, r_c = _outproj(c_of, c_ob, c_qkvg, c_loc, ctx, mod, bsz, p, tm_c)
            m_all = jnp.concatenate([m_x.reshape(n_lat, d), m_c.reshape(n_ctx, d)], axis=0)
            route = jnp.concatenate([r_x.reshape(n_lat, ROUTE_W), r_c.reshape(n_ctx, ROUTE_W)], axis=0)
            n_tok = n_lat + n_ctx

        plan = _plan(route)
        ys = _experts(_dispatch(m_all, plan), plan, p["w_e_in"], p["w_e_out"])
        x = _combine(x, ys, route, plan, 0, mod, None, gf, last)
        if not last:
            ctx = _combine(ctx, ys, route, plan, n_lat, mod, bsz, gf, False)
    return x
```

```python
import functools

import numpy as np
import jax
import jax.numpy as jnp
from jax import lax
from jax.experimental import pallas as pl
from jax.experimental.pallas import tpu as pltpu

F32 = jnp.float32
BF16 = jnp.bfloat16

D_MODEL = 1024
GRID_W = 64
GROUP_W = 256
GLA_HEADS = 4
GLA_DV = 64
GLA_DK = 32
GLA_QK_W = 128
GLA_GATE_RANK = 16
GLA_GATE_NORM = 16.0
GLA_CHUNK = 64
GLA_GROUP = 256
CONF_WIDTH = 31
SC_WIDTH = 3
POOL_WINDOWS = (2, 4, 8, 16)
POOL_GW = 64
N_GROUPS = 4
EXPERTS_PER_GROUP = 8
N_EXPERTS = 32
D_EXPERT = 512
MOE_BLOCK = 256
NORM_EPS = 1e-6
IN_SPLITS = (128, 128, 256, 256, 16, 16, 512, 256, 256, 256, 256)
N_IN_PAD = 2432
LANE = 128
ROUTE_W = LANE

VMEM_LIMIT = 48 << 20


def _cp(sem, vmem=VMEM_LIMIT):
    return pltpu.CompilerParams(dimension_semantics=sem, vmem_limit_bytes=vmem)


def _sigmoid(x):
    return 1.0 / (1.0 + jnp.exp(-x))


def _silu(x):
    return x * _sigmoid(x)


def _log_sigmoid(x):
    return jnp.minimum(x, 0.0) - jnp.log1p(jnp.exp(-jnp.abs(x)))


def _dot(a, b):
    return jnp.dot(a.astype(BF16), b.astype(BF16), preferred_element_type=F32)


def _rms(x, g):
    return x * lax.rsqrt(jnp.mean(x * x, axis=-1, keepdims=True) + NORM_EPS) * g


def _adaln_kernel(cv_ref, w_ref, b_ref, o_ref):
    o_ref[0] = _dot(_silu(cv_ref[...]), w_ref[0]) + b_ref[0]


def _adaln(cv, w_mod, b_mod):
    depth, d, n = w_mod.shape
    tn = 1024
    return pl.pallas_call(
        _adaln_kernel,
        grid=(depth, n // tn),
        in_specs=[pl.BlockSpec((8, d), lambda l, j: (0, 0)),
                  pl.BlockSpec((1, d, tn), lambda l, j: (l, 0, j)),
                  pl.BlockSpec((1, 1, tn), lambda l, j: (l, 0, j))],
        out_specs=pl.BlockSpec((1, 8, tn), lambda l, j: (l, 0, j)),
        out_shape=jax.ShapeDtypeStruct((depth, 8, n), F32),
        compiler_params=_cp(("parallel", "parallel")),
        name="adaln",
    )(cv, w_mod, b_mod.reshape(depth, 1, n))


def _inproj_kernel(x_ref, mod_ref, g1_ref, w_ref, wg_ref, bg_ref,
                   qkvg_ref, la_ref, u_ref, sc_ref, pool_ref):
    x = x_ref[0]
    mod = mod_ref[0]
    m = _rms(x, g1_ref[...]) * (1.0 + mod[1:2]) + mod[0:1]
    h = _dot(m, w_ref[...])
    qkvg_ref[0] = h[:, 0:768]
    u_ref[0] = h[:, 768:1024] * _sigmoid(h[:, 1024:1280])
    sc_ref[0, :, 0:256] = h[:, 1280:1536]
    sc_ref[0, :, 256:512] = h[:, 1536:1792] * h[:, 1792:2048]
    pool_ref[0] = h[:, 2048:2304]
    z = _dot(h[:, 2304:2432], wg_ref[...]) + bg_ref[...]
    la_ref[0] = _log_sigmoid(z) / GLA_GATE_NORM


def _inproj(x, mod, mod_row, g1, w_in_p, w_gate, b_gate, tm):
    bsz, seq, d = x.shape
    row = (lambda b: b) if mod_row is None else (lambda b: mod_row)
    tok = lambda w: pl.BlockSpec((1, tm, w), lambda b, i: (b, i, 0))
    full = lambda a: pl.BlockSpec(a.shape, lambda b, i: (0,) * a.ndim)
    outs = (768, 256, 256, 512, 256)
    return pl.pallas_call(
        _inproj_kernel,
        grid=(bsz, seq // tm),
        in_specs=[tok(d),
                  pl.BlockSpec((1, 6, d), lambda b, i: (row(b), 0, 0)),
                  full(g1), full(w_in_p), full(w_gate), full(b_gate)],
        out_specs=[tok(w) for w in outs],
        out_shape=[jax.ShapeDtypeStruct((bsz, seq, w), F32) for w in outs],
        compiler_params=_cp(("parallel", "parallel")),
        name="inproj",
    )(x, mod, g1, w_in_p, w_gate, b_gate)


def _gla_direction(q, k, v, la, state, reverse):
    blk = q.shape[0]
    g = GLA_GROUP
    nchunk = g // GLA_CHUNK
    ti = lax.broadcasted_iota(jnp.int32, (g, g), 0)
    tj = lax.broadcasted_iota(jnp.int32, (g, g), 1)
    same = (ti // GLA_CHUNK) == (tj // GLA_CHUNK)
    cmask = same & ((tj >= ti) if reverse else (tj <= ti))
    tri = jnp.where(cmask, 1.0, 0.0).astype(BF16)
    qk_head = lax.broadcasted_iota(jnp.int32, (g, GLA_QK_W), 1) // GLA_DK
    v_head = lax.broadcasted_iota(jnp.int32, (g, GROUP_W), 1) // GLA_DV
    bd = (lax.broadcasted_iota(jnp.int32, (GLA_QK_W, GROUP_W), 0) // GLA_DK
          == lax.broadcasted_iota(jnp.int32, (GLA_QK_W, GROUP_W), 1) // GLA_DV)
    tok_chunk = lax.broadcasted_iota(jnp.int32, (GLA_QK_W, g), 1) // GLA_CHUNK
    scale = GLA_DK ** -0.5
    outs = [None] * (blk // g)
    groups = range(blk // g)
    for gi in (reversed(groups) if reverse else groups):
        sl = slice(gi * g, (gi + 1) * g)
        qg, kg, vg, lg = q[sl], k[sl], v[sl], la[sl]
        l0 = lg.astype(BF16)
        r0 = lg - l0.astype(F32)
        l1 = r0.astype(BF16)
        l2 = (r0 - l1.astype(F32)).astype(BF16)
        b = (jnp.dot(tri, l0, preferred_element_type=F32)
             + jnp.dot(tri, l1, preferred_element_type=F32)
             + jnp.dot(tri, l2, preferred_element_type=F32))
        last = 0 if reverse else GLA_CHUNK - 1
        blast = jnp.concatenate(
            [jnp.broadcast_to(b[c * GLA_CHUNK + last:c * GLA_CHUNK + last + 1], (GLA_CHUNK, GLA_QK_W))
             for c in range(nchunk)], axis=0)
        qd = (qg * scale) * jnp.exp(b)
        ki = (kg * jnp.exp(-b)).astype(BF16)
        kw_t = (kg * jnp.exp(blast - b)).T
        b_t = b.T
        vb = vg.astype(BF16)
        o = jnp.zeros((g, GROUP_W), F32)
        for h in range(GLA_HEADS):
            qh = jnp.where(qk_head == h, qd, 0.0).astype(BF16)
            s = lax.dot_general(qh, ki, (((1,), (1,)), ((), ())), preferred_element_type=F32)
            s = jnp.where(cmask, s, 0.0).astype(BF16)
            oh = jnp.dot(s, vb, preferred_element_type=F32)
            o = o + jnp.where(v_head == h, oh, 0.0)
        qdb = qd.astype(BF16)
        inter = [None] * nchunk
        chunks = range(nchunk)
        for c in (reversed(chunks) if reverse else chunks):
            rows = slice(c * GLA_CHUNK, (c + 1) * GLA_CHUNK)
            inter[c] = jnp.dot(qdb[rows], state.astype(BF16), preferred_element_type=F32)
            col = c * GLA_CHUNK + last
            dec = jnp.exp(b_t[:, col:col + 1])
            kv = jnp.dot(jnp.where(tok_chunk == c, kw_t, 0.0).astype(BF16), vb,
                         preferred_element_type=F32)
            state = dec * state + jnp.where(bd, kv, 0.0)
        outs[gi] = o + jnp.concatenate(inter, axis=0)
    return jnp.concatenate(outs, axis=0) if len(outs) > 1 else outs[0], state


def _gla_kernel(qk_f, v_f, la_f, qk_b, v_b, la_b, s0_ref, of_ref, ob_ref, sfin_ref, st_ref):
    i = pl.program_id(1)

    @pl.when(i == 0)
    def _():
        st_ref[...] = s0_ref[0]

    qk = qk_f[0]
    o, s = _gla_direction(qk[:, 0:128], qk[:, 128:256], v_f[0], la_f[0], st_ref[0], False)
    of_ref[0] = o
    st_ref[0] = s
    qk = qk_b[0]
    o, s = _gla_direction(qk[:, 0:128], qk[:, 128:256], v_b[0], la_b[0], st_ref[1], True)
    ob_ref[0] = o
    st_ref[1] = s

    @pl.when(i == pl.num_programs(1) - 1)
    def _():
        sfin_ref[0] = st_ref[...]


def _gla(qkvg, la, s0, blk):
    bsz, seq, _ = qkvg.shape
    nb = seq // blk
    fwd = lambda j: pl.BlockSpec((1, blk, 256), lambda b, i: (b, i, j))
    bwd = lambda j: pl.BlockSpec((1, blk, 256), lambda b, i: (b, nb - 1 - i, j))
    la_f = pl.BlockSpec((1, blk, 128), lambda b, i: (b, i, 0))
    la_b = pl.BlockSpec((1, blk, 128), lambda b, i: (b, nb - 1 - i, 1))
    st = pl.BlockSpec((1, 2, GLA_QK_W, GROUP_W), lambda b, i: (b, 0, 0, 0))
    return pl.pallas_call(
        _gla_kernel,
        grid=(bsz, nb),
        in_specs=[fwd(0), fwd(1), la_f, bwd(0), bwd(1), la_b, st],
        out_specs=[pl.BlockSpec((1, blk, GROUP_W), lambda b, i: (b, i, 0)),
                   pl.BlockSpec((1, blk, GROUP_W), lambda b, i: (b, nb - 1 - i, 0)),
                   st],
        out_shape=[jax.ShapeDtypeStruct((bsz, seq, GROUP_W), F32),
                   jax.ShapeDtypeStruct((bsz, seq, GROUP_W), F32),
                   jax.ShapeDtypeStruct((bsz, 2, GLA_QK_W, GROUP_W), F32)],
        scratch_shapes=[pltpu.VMEM((2, GLA_QK_W, GROUP_W), F32)],
        compiler_params=_cp(("parallel", "arbitrary")),
        name="gla",
    )(qkvg, qkvg, la, qkvg, qkvg, la, s0)


def _row_shift(x, s, col, row_len):
    if s == 0:
        return x
    n = x.shape[0]
    y = pltpu.roll(x, (-s) % n, 0)
    valid = (col < row_len - s) if s > 0 else (col >= -s)
    return jnp.where(valid, y, 0.0)


def _row_conv(x, w, col, row_len):
    k = w.shape[0]
    acc = None
    for j in range(k):
        term = _row_shift(x, j - k // 2, col, row_len) * w[j:j + 1]
        acc = term if acc is None else acc + term
    return acc


def _col_conv(pad_ref, w, t0, n):
    k = w.shape[0]
    acc = None
    for j in range(k):
        start = pl.multiple_of(t0 + j * GRID_W, GRID_W)
        term = pad_ref[pl.ds(start, n), :] * w[j:j + 1]
        acc = term if acc is None else acc + term
    return acc


def _pool_diffs(x, col, row_len):
    lane = lax.broadcasted_iota(jnp.int32, x.shape, 1)
    colf = col.astype(F32)
    mean = jnp.zeros_like(x)
    for gi, win in enumerate(POOL_WINDOWS):
        half = win // 2
        acc = None
        for s in range(-half, half):
            t = _row_shift(x, s, col, row_len)
            acc = t if acc is None else acc + t
        cnt = jnp.minimum(colf + half, float(row_len)) - jnp.maximum(colf - half, 0.0)
        mean = jnp.where(lane // POOL_GW == gi, acc / cnt, mean)
    return mean - x


def _local_tail(u, scb, conv_sc, pool, col, row_len, cb, lg, lb, pw, ps, out_ref):
    u = u + cb
    uc = u - jnp.mean(u, axis=-1, keepdims=True)
    ln = uc * lax.rsqrt(jnp.mean(uc * uc, axis=-1, keepdims=True) + NORM_EPS) * lg + lb
    out_ref[0, :, 0:256] = _silu(ln).astype(BF16)
    out_ref[0, :, 256:512] = (scb * conv_sc).astype(BF16)
    out_ref[0, :, 512:768] = (_dot(_pool_diffs(pool, col, row_len), pw) * ps).astype(BF16)


def _local_axial_kernel(uh_ref, uv_ref, scb_ref, sch_ref, scv_ref, pool_ref,
                        cw_ref, cb_ref, lg_ref, lb_ref, sw_ref, pw_ref, ps_ref,
                        out_ref, upad, spad):
    i = pl.program_id(1)
    blk = uh_ref.shape[1]
    seq = uv_ref.shape[1]
    pu = (CONF_WIDTH // 2) * GRID_W
    psc = (SC_WIDTH // 2) * GRID_W

    @pl.when(i == 0)
    def _():
        upad[0:pu, :] = jnp.zeros((pu, LANE), F32)
        upad[pu:pu + seq, :] = uv_ref[0]
        upad[pu + seq:pu + seq + pu, :] = jnp.zeros((pu, LANE), F32)
        spad[0:psc, :] = jnp.zeros((psc, LANE), F32)
        spad[psc:psc + seq, :] = scv_ref[0]
        spad[psc + seq:psc + seq + psc, :] = jnp.zeros((psc, LANE), F32)

    t0 = i * blk
    col = lax.broadcasted_iota(jnp.int32, (blk, LANE), 0) % GRID_W
    cw = cw_ref[...]
    sw = sw_ref[...]
    u = jnp.concatenate([_row_conv(uh_ref[0], cw[:, 0:128], col, GRID_W),
                         _col_conv(upad, cw[:, 128:256], t0, blk)], axis=-1)
    csc = jnp.concatenate([_row_conv(sch_ref[0], sw[:, 0:128], col, GRID_W),
                           _col_conv(spad, sw[:, 128:256], t0, blk)], axis=-1)
    col2 = jnp.concatenate([col, col], axis=-1)
    _local_tail(u, scb_ref[0], csc, pool_ref[0], col2, GRID_W, cb_ref[...], lg_ref[...],
                lb_ref[...], pw_ref[...], ps_ref[...], out_ref)


def _local_seq_kernel(u_ref, sc_ref, pool_ref, cw_ref, cb_ref, lg_ref, lb_ref, sw_ref,
                      pw_ref, ps_ref, out_ref):
    seq = u_ref.shape[1]
    col = lax.broadcasted_iota(jnp.int32, (seq, GROUP_W), 0)
    sc = sc_ref[0]
    u = _row_conv(u_ref[0], cw_ref[...], col, seq)
    csc = _row_conv(sc[:, 256:512], sw_ref[...], col, seq)
    _local_tail(u, sc[:, 0:256], csc, pool_ref[0], col, seq, cb_ref[...], lg_ref[...],
                lb_ref[...], pw_ref[...], ps_ref[...], out_ref)


def _local_params(p):
    return (p["conf_w"], p["conf_b"], p["conf_g"], p["conf_beta"], p["sc_w"], p["pool_bd"],
            p["pool_scale"])


def _local_axial(u, sc, pool, p, blk):
    bsz, seq, _ = u.shape
    cw, cb, lg, lb, sw, pw, ps = _local_params(p)
    blkspec = lambda w, j: pl.BlockSpec((1, blk, w), lambda b, i: (b, i, j))
    seqspec = lambda j: pl.BlockSpec((1, seq, LANE), lambda b, i: (b, 0, j))
    full = lambda a: pl.BlockSpec(a.shape, lambda b, i: (0,) * a.ndim)
    pu = (CONF_WIDTH // 2) * GRID_W
    psc = (SC_WIDTH // 2) * GRID_W
    return pl.pallas_call(
        _local_axial_kernel,
        grid=(bsz, seq // blk),
        in_specs=[blkspec(LANE, 0), seqspec(1), blkspec(256, 0), blkspec(LANE, 2), seqspec(3),
                  blkspec(256, 0), full(cw), full(cb), full(lg), full(lb), full(sw), full(pw),
                  full(ps)],
        out_specs=pl.BlockSpec((1, blk, 768), lambda b, i: (b, i, 0)),
        out_shape=jax.ShapeDtypeStruct((bsz, seq, 768), BF16),
        scratch_shapes=[pltpu.VMEM((seq + 2 * pu, LANE), F32),
                        pltpu.VMEM((seq + 2 * psc, LANE), F32)],
        compiler_params=_cp(("parallel", "arbitrary")),
        name="local_axial",
    )(u, u, sc, sc, sc, pool, cw, cb, lg, lb, sw, pw, ps)


def _local_seq(u, sc, pool, p):
    bsz, seq, _ = u.shape
    cw, cb, lg, lb, sw, pw, ps = _local_params(p)
    tok = lambda w: pl.BlockSpec((1, seq, w), lambda b: (b, 0, 0))
    full = lambda a: pl.BlockSpec(a.shape, lambda b: (0,) * a.ndim)
    return pl.pallas_call(
        _local_seq_kernel,
        grid=(bsz,),
        in_specs=[tok(256), tok(512), tok(256), full(cw), full(cb), full(lg), full(lb),
                  full(sw), full(pw), full(ps)],
        out_specs=tok(768),
        out_shape=jax.ShapeDtypeStruct((bsz, seq, 768), BF16),
        compiler_params=_cp(("parallel",)),
        name="local_seq",
    )(u, sc, pool, cw, cb, lg, lb, sw, pw, ps)


def _route(logits):
    lane = lax.broadcasted_iota(jnp.int32, logits.shape, 1)
    neg = -jnp.inf
    big = ROUTE_W

    def first_max(vals):
        mx = jnp.max(vals, axis=-1, keepdims=True)
        idx = jnp.min(jnp.where(vals == mx, lane, big), axis=-1, keepdims=True)
        return mx, idx

    lg = jnp.where(lane < N_GROUPS, logits, neg)
    gmx, grp = first_max(lg)
    p_grp = 1.0 / jnp.sum(jnp.exp(lg - gmx), axis=-1, keepdims=True)
    lo = N_GROUPS + grp * EXPERTS_PER_GROUP
    le = jnp.where((lane >= lo) & (lane < lo + EXPERTS_PER_GROUP), logits, neg)
    v1, i1 = first_max(le)
    v2, i2 = first_max(jnp.where(lane == i1, neg, le))
    e2 = jnp.exp(v2 - v1)
    w1 = p_grp / (1.0 + e2)
    w2 = p_grp * e2 / (1.0 + e2)
    rec = jnp.where(lane == 0, (i1 - N_GROUPS).astype(F32), 0.0)
    rec = jnp.where(lane == 1, (i2 - N_GROUPS).astype(F32), rec)
    rec = jnp.where(lane == 2, w1, rec)
    return jnp.where(lane == 3, w2, rec)


def _outproj_kernel(of_ref, ob_ref, g_ref, loc_ref, x_ref, mod_ref, gout_ref, hsum_ref,
                    wo_ref, g2_ref, wr_ref, br_ref, xo_ref, m_ref, route_ref):
    mod = mod_ref[0]
    o = of_ref[0] + ob_ref[0]
    o2 = o * o
    hi = o2.astype(BF16)
    lo = (o2 - hi.astype(F32)).astype(BF16)
    ms = (jnp.dot(hi, hsum_ref[...], preferred_element_type=F32)
          + jnp.dot(lo, hsum_ref[...], preferred_element_type=F32)) / GLA_DV
    y_gla = o * lax.rsqrt(ms + NORM_EPS) * gout_ref[...] * _silu(g_ref[0])
    wo = wo_ref[...]
    proj = _dot(y_gla, wo[0:256]) + _dot(loc_ref[0], wo[256:1024])
    x = x_ref[0] + mod[2:3] * proj
    xo_ref[0] = x
    m = _rms(x, g2_ref[...]) * (1.0 + mod[4:5]) + mod[3:4]
    m_ref[0] = m
    route_ref[0] = _route(_dot(m, wr_ref[...]) + br_ref[...])


def _outproj(o_f, o_b, qkvg, y_loc, x, mod, mod_row, p, tm):
    bsz, seq, d = x.shape
    row = (lambda b: b) if mod_row is None else (lambda b: mod_row)
    tok = lambda w, j=0: pl.BlockSpec((1, tm, w), lambda b, i: (b, i, j))
    full = lambda a: pl.BlockSpec(a.shape, lambda b, i: (0,) * a.ndim)
    consts = (p["g_out"], p["head_sum"], p["w_out"], p["g_norm2"], p["w_router"], p["b_router"])
    return pl.pallas_call(
        _outproj_kernel,
        grid=(bsz, seq // tm),
        in_specs=[tok(256), tok(256), tok(256, 2), tok(768), tok(d),
                  pl.BlockSpec((1, 6, d), lambda b, i: (row(b), 0, 0))] + [full(a) for a in consts],
        out_specs=[tok(d), tok(d), tok(ROUTE_W)],
        out_shape=[jax.ShapeDtypeStruct((bsz, seq, d), F32),
                   jax.ShapeDtypeStruct((bsz, seq, d), F32),
                   jax.ShapeDtypeStruct((bsz, seq, ROUTE_W), F32)],
        compiler_params=_cp(("parallel", "parallel")),
        name="outproj",
    )(o_f, o_b, qkvg, y_loc, x, mod, *consts)


RANK_ROWS = 512


def _rank_kernel(route_ref, rank_ref, cnt_ref, carry):
    i = pl.program_id(0)

    @pl.when(i == 0)
    def _():
        carry[...] = jnp.zeros_like(carry)

    rec = route_ref[...]
    tm = rec.shape[0]
    lane = lax.broadcasted_iota(jnp.int32, rec.shape, 1).astype(F32)
    oh0 = jnp.where(lane == rec[:, 0:1], 1.0, 0.0)
    oh1 = jnp.where(lane == rec[:, 1:2], 1.0, 0.0)
    ti = lax.broadcasted_iota(jnp.int32, (tm, tm), 0)
    tj = lax.broadcasted_iota(jnp.int32, (tm, tm), 1)
    before = jnp.where(tj < ti, 1.0, 0.0).astype(BF16)
    tot0 = jnp.sum(oh0, axis=0, keepdims=True)
    tot1 = jnp.sum(oh1, axis=0, keepdims=True)
    base = carry[...]
    pre0 = jnp.dot(before, oh0.astype(BF16), preferred_element_type=F32) + base
    pre1 = jnp.dot(before, oh1.astype(BF16), preferred_element_type=F32) + (base + tot0)
    r0 = jnp.sum(oh0 * pre0, axis=-1, keepdims=True)
    r1 = jnp.sum(oh1 * pre1, axis=-1, keepdims=True)
    lane_i = lax.broadcasted_iota(jnp.int32, rec.shape, 1)
    rank_ref[...] = jnp.where(lane_i == 0, r0, jnp.where(lane_i == 1, r1, 0.0)).astype(jnp.int32)
    carry[...] = base + tot0 + tot1
    cnt_ref[...] = carry[...]


def _rank(route):
    n_tok = route.shape[0]
    tm = RANK_ROWS if n_tok % RANK_ROWS == 0 else 256
    return pl.pallas_call(
        _rank_kernel,
        grid=(n_tok // tm,),
        in_specs=[pl.BlockSpec((tm, ROUTE_W), lambda i: (i, 0))],
        out_specs=[pl.BlockSpec((tm, ROUTE_W), lambda i: (i, 0)),
                   pl.BlockSpec((1, ROUTE_W), lambda i: (0, 0))],
        out_shape=[jax.ShapeDtypeStruct((n_tok, ROUTE_W), jnp.int32),
                   jax.ShapeDtypeStruct((1, ROUTE_W), F32)],
        scratch_shapes=[pltpu.VMEM((1, ROUTE_W), F32)],
        compiler_params=_cp(("arbitrary",)),
        name="rank",
    )(route)


def _plan(route):
    n_tok = route.shape[0]
    rank, cnt = _rank(route)
    counts = cnt[0, 0:N_EXPERTS].astype(jnp.int32)
    padded = (counts + MOE_BLOCK - 1) // MOE_BLOCK * MOE_BLOCK
    pad_end = jnp.cumsum(padded)
    pad_start = pad_end - padded
    eid = route[:, 0:2].astype(jnp.int32)
    onehot = eid[:, :, None] == jnp.arange(N_EXPERTS, dtype=jnp.int32)[None, None, :]
    dest = (jnp.sum(jnp.where(onehot, pad_start[None, None, :], 0), axis=-1) + rank[:, 0:2]).reshape(-1)
    n_blocks = 2 * n_tok // MOE_BLOCK + N_EXPERTS
    blk_start = jnp.arange(n_blocks, dtype=jnp.int32) * MOE_BLOCK
    nused = pad_end[-1] // MOE_BLOCK
    blk = jnp.minimum(jnp.arange(n_blocks, dtype=jnp.int32), nused - 1)
    blk_expert = jnp.sum((blk[:, None] * MOE_BLOCK >= pad_end[None, :]).astype(jnp.int32), axis=-1)
    fill_start = pad_start + counts
    fill_n = padded - counts
    return dict(dest=dest.astype(jnp.int32), blk=blk.astype(jnp.int32),
                blk_expert=jnp.minimum(blk_expert, N_EXPERTS - 1).astype(jnp.int32),
                nused=nused.astype(jnp.int32).reshape(1), fill_start=fill_start.astype(jnp.int32),
                fill_n=fill_n.astype(jnp.int32), n_blocks=n_blocks)


DISPATCH_ROWS = 512
DMA_UNROLL = 4


def _dispatch_kernel(dest_ref, fstart_ref, fn_ref, nused_ref, *refs, tr, n_x, n_c):
    if n_c:
        mx_hbm, mc_hbm, xs_hbm, buf, zbuf, in_sem, out_sem, fsem = refs
    else:
        mx_hbm, xs_hbm, buf, zbuf, in_sem, out_sem, fsem = refs
        mc_hbm = None
    i = pl.program_id(0)
    nsteps = n_x + n_c
    nslot = 3

    def start_load(step):
        slot = step % nslot

        @pl.when(step < n_x)
        def _():
            rows = pl.ds(pl.multiple_of(step * tr, tr), tr)
            pltpu.make_async_copy(mx_hbm.at[rows], buf.at[slot], in_sem.at[slot]).start()

        if n_c:
            @pl.when(step >= n_x)
            def _():
                rows = pl.ds(pl.multiple_of((step - n_x) * tr, tr), tr)
                pltpu.make_async_copy(mc_hbm.at[rows], buf.at[slot], in_sem.at[slot]).start()

    def wait_load(step):
        slot = step % nslot
        pltpu.make_async_copy(mx_hbm.at[pl.ds(0, tr)], buf.at[slot], in_sem.at[slot]).wait()

    def row_copy(step, r, k):
        slot = step % nslot
        dst = dest_ref[2 * (step * tr + r) + k]
        return pltpu.make_async_copy(buf.at[slot, pl.ds(r, 1)], xs_hbm.at[pl.ds(dst, 1)],
                                     out_sem.at[slot])

    def wait_scatter(step):
        slot = step % nslot
        for _ in range(2):
            pltpu.make_async_copy(buf.at[slot], xs_hbm.at[pl.ds(0, tr)], out_sem.at[slot]).wait()

    @pl.when(i == 0)
    def _():
        zbuf[...] = jnp.zeros_like(zbuf)
        start_load(i)

    @pl.when(i >= 2)
    def _():
        wait_scatter(i - 2)

    @pl.when(i + 1 < nsteps)
    def _():
        start_load(i + 1)

    wait_load(i)

    def issue(j, carry):
        for u in range(DMA_UNROLL):
            r = j * DMA_UNROLL + u
            row_copy(i, r, 0).start()
            row_copy(i, r, 1).start()
        return carry

    lax.fori_loop(0, tr // DMA_UNROLL, issue, 0)

    @pl.when(i == nsteps - 1)
    def _():
        if nsteps >= 2:
            wait_scatter(i - 1)
        wait_scatter(i)

        def fill_copy(e, r):
            return pltpu.make_async_copy(zbuf.at[pl.ds(0, 1)],
                                         xs_hbm.at[pl.ds(fstart_ref[e] + r, 1)], fsem.at[0])

        def per_expert(e, carry):
            lax.fori_loop(0, fn_ref[e], lambda r, c: (fill_copy(e, r).start(), c)[1], 0)
            return carry

        def per_expert_wait(e, carry):
            lax.fori_loop(0, fn_ref[e], lambda r, c: (fill_copy(e, r).wait(), c)[1], 0)
            return carry

        lax.fori_loop(0, N_EXPERTS, per_expert, 0)
        lax.fori_loop(0, N_EXPERTS, per_expert_wait, 0)

        def tail_copy(b):
            start = pl.multiple_of(b * MOE_BLOCK, MOE_BLOCK)
            return pltpu.make_async_copy(zbuf, xs_hbm.at[pl.ds(start, MOE_BLOCK)], fsem.at[0])

        nblk = xs_hbm.shape[0] // MOE_BLOCK
        lax.fori_loop(nused_ref[0], nblk, lambda b, c: (tail_copy(b).start(), c)[1], 0)
        lax.fori_loop(nused_ref[0], nblk, lambda b, c: (tail_copy(b).wait(), c)[1], 0)


def _dispatch(m_x, m_c, plan):
    n_lat, d = m_x.shape
    n_ctx = 0 if m_c is None else m_c.shape[0]
    tr = DISPATCH_ROWS
    while n_lat % tr or n_ctx % tr:
        tr //= 2
    slots = plan["n_blocks"] * MOE_BLOCK
    n_x, n_c = n_lat // tr, n_ctx // tr
    srcs = (m_x,) if m_c is None else (m_x, m_c)
    grid_spec = pltpu.PrefetchScalarGridSpec(
        num_scalar_prefetch=4,
        grid=(n_x + n_c,),
        in_specs=[pl.BlockSpec(memory_space=pl.ANY)] * len(srcs),
        out_specs=pl.BlockSpec(memory_space=pl.ANY),
        scratch_shapes=[pltpu.VMEM((3, tr, d), F32), pltpu.VMEM((MOE_BLOCK, d), F32),
                        pltpu.SemaphoreType.DMA((3,)), pltpu.SemaphoreType.DMA((3,)),
                        pltpu.SemaphoreType.DMA((1,))],
    )
    return pl.pallas_call(
        functools.partial(_dispatch_kernel, tr=tr, n_x=n_x, n_c=n_c),
        grid_spec=grid_spec,
        out_shape=jax.ShapeDtypeStruct((slots, d), F32),
        compiler_params=_cp(("arbitrary",)),
        name="dispatch",
    )(plan["dest"], plan["fill_start"], plan["fill_n"], plan["nused"], *srcs)


def _expert_kernel(blk_ref, be_ref, nused_ref, xs_ref, w1_ref, w2_ref, ys_ref, w1b, w2b):
    i = pl.program_id(0)
    used = i < nused_ref[0]
    fresh = (i == 0) | (be_ref[i] != be_ref[jnp.maximum(i - 1, 0)])

    @pl.when(used & fresh)
    def _():
        w1b[...] = w1_ref[0, 0].astype(BF16)
        w2b[...] = w2_ref[0, 0].astype(BF16)

    @pl.when(used)
    def _():
        h = _dot(xs_ref[...], w1b[...])
        act = _silu(h[:, :D_EXPERT]) * h[:, D_EXPERT:]
        ys_ref[...] = _dot(act, w2b[...])

    @pl.when(jnp.logical_not(used))
    def _():
        ys_ref[...] = jnp.zeros_like(ys_ref)


def _experts(xs, plan, w1, w2, layer):
    slots, d = xs.shape
    grid_spec = pltpu.PrefetchScalarGridSpec(
        num_scalar_prefetch=3,
        grid=(plan["n_blocks"],),
        in_specs=[pl.BlockSpec((MOE_BLOCK, d), lambda i, blk, be, nu: (blk[i], 0)),
                  pl.BlockSpec((1, 1, d, 2 * D_EXPERT), lambda i, blk, be, nu: (layer, be[i], 0, 0)),
                  pl.BlockSpec((1, 1, D_EXPERT, d), lambda i, blk, be, nu: (layer, be[i], 0, 0))],
        out_specs=pl.BlockSpec((MOE_BLOCK, d), lambda i, blk, be, nu: (i, 0)),
        scratch_shapes=[pltpu.VMEM((d, 2 * D_EXPERT), BF16), pltpu.VMEM((D_EXPERT, d), BF16)],
    )
    return pl.pallas_call(
        _expert_kernel,
        grid_spec=grid_spec,
        out_shape=jax.ShapeDtypeStruct((slots, d), F32),
        compiler_params=_cp(("arbitrary",)),
        name="experts",
    )(plan["blk"], plan["blk_expert"], plan["nused"], xs, w1, w2)


COMBINE_ROWS = 256


def _combine_kernel(dest_ref, x_ref, route_ref, mod_ref, gf_ref, ys_hbm, o_ref, y0buf, y1buf, sem,
                    *, final, tok_off):
    i = pl.program_id(0)
    nsteps = pl.num_programs(0)
    tc = COMBINE_ROWS

    def row_copy(step, r, k, slot):
        a = 2 * (tok_off + step * tc + r) + k
        buf = y0buf if k == 0 else y1buf
        return pltpu.make_async_copy(ys_hbm.at[pl.ds(dest_ref[a], 1)], buf.at[slot, pl.ds(r, 1)],
                                     sem.at[slot])

    def issue_step(step):
        def body(j, carry):
            for u in range(DMA_UNROLL):
                r = j * DMA_UNROLL + u
                row_copy(step, r, 0, step % 2).start()
                row_copy(step, r, 1, step % 2).start()
            return carry
        lax.fori_loop(0, tc // DMA_UNROLL, body, 0)

    @pl.when(i == 0)
    def _():
        issue_step(0)

    @pl.when(i + 1 < nsteps)
    def _():
        issue_step(i + 1)

    slot = i % 2
    pltpu.make_async_copy(ys_hbm.at[pl.ds(0, tc)], y0buf.at[slot], sem.at[slot]).wait()
    pltpu.make_async_copy(ys_hbm.at[pl.ds(0, tc)], y1buf.at[slot], sem.at[slot]).wait()
    rec = route_ref[...]
    y = rec[:, 2:3] * y0buf[slot] + rec[:, 3:4] * y1buf[slot]
    x = x_ref[0] + mod_ref[0][5:6] * y
    o_ref[0] = _rms(x, gf_ref[...]) if final else x


def _combine(x, ys, route, plan, tok_off, mod, mod_row, g_final, final):
    bsz, seq, d = x.shape
    tc = COMBINE_ROWS
    nt = seq // tc
    assert seq % tc == 0 and tok_off % tc == 0
    row = (lambda i: i // nt) if mod_row is None else (lambda i: mod_row)
    grid_spec = pltpu.PrefetchScalarGridSpec(
        num_scalar_prefetch=1,
        grid=(bsz * nt,),
        in_specs=[pl.BlockSpec((1, tc, d), lambda i, de: (i // nt, i % nt, 0)),
                  pl.BlockSpec((tc, ROUTE_W), lambda i, de: (tok_off // tc + i, 0)),
                  pl.BlockSpec((1, 6, d), lambda i, de: (row(i), 0, 0)),
                  pl.BlockSpec((1, d), lambda i, de: (0, 0)),
                  pl.BlockSpec(memory_space=pl.ANY)],
        out_specs=pl.BlockSpec((1, tc, d), lambda i, de: (i // nt, i % nt, 0)),
        scratch_shapes=[pltpu.VMEM((2, tc, d), F32), pltpu.VMEM((2, tc, d), F32),
                        pltpu.SemaphoreType.DMA((2,))],
    )
    return pl.pallas_call(
        functools.partial(_combine_kernel, final=final, tok_off=tok_off),
        grid_spec=grid_spec,
        out_shape=jax.ShapeDtypeStruct((bsz, seq, d), F32),
        compiler_params=_cp(("arbitrary",)),
        name="combine",
    )(plan["dest"], x, route, mod, g_final, ys)


def _layer_params(l, w_in, gla_w_gate_f, gla_b_gate_f, gla_w_gate_b, gla_b_gate_b, gla_g_out,
                  conf_w_dw, conf_b_dw, conf_ln_g, conf_ln_b, sc_w_dw, pool_w, pool_scale, w_out,
                  router_w_group, router_b_group, router_w_expert, router_b_expert,
                  expert_w_in, expert_w_out, g_norm1, g_norm2):
    d = D_MODEL
    idx = np.cumsum((0,) + IN_SPLITS)
    q, k, v, g, zf, zb, conf, scb, scc, scx, pool = [w_in[l][:, idx[j]:idx[j + 1]] for j in range(11)]
    pad = jnp.zeros((d, N_IN_PAD - idx[-1]), F32)
    w_in_p = jnp.concatenate([q, k, v, g, conf, scb, scc, scx, pool, zf, zb, pad], axis=1).astype(BF16)
    r = GLA_GATE_RANK
    w_gate = jnp.zeros((LANE, 2 * GLA_QK_W), F32)
    w_gate = w_gate.at[0:r, 0:GLA_QK_W].set(gla_w_gate_f[l]).at[r:2 * r, GLA_QK_W:].set(gla_w_gate_b[l])
    b_gate = jnp.concatenate([gla_b_gate_f[l], gla_b_gate_b[l]])[None]
    pool_bd = jnp.zeros((GROUP_W, GROUP_W), F32)
    for gi in range(len(POOL_WINDOWS)):
        s = slice(gi * POOL_GW, (gi + 1) * POOL_GW)
        pool_bd = pool_bd.at[s, s].set(pool_w[l, gi])
    head = np.arange(GROUP_W) // GLA_DV
    head_sum = jnp.asarray(head[:, None] == head[None, :], BF16)
    w_router = jnp.zeros((d, ROUTE_W), F32)
    w_router = w_router.at[:, 0:N_GROUPS].set(router_w_group[l])
    w_router = w_router.at[:, N_GROUPS:N_GROUPS + N_EXPERTS].set(router_w_expert[l])
    b_router = jnp.zeros((1, ROUTE_W), F32)
    b_router = b_router.at[0, 0:N_GROUPS].set(router_b_group[l])
    b_router = b_router.at[0, N_GROUPS:N_GROUPS + N_EXPERTS].set(router_b_expert[l])
    return dict(
        w_in=w_in_p, w_gate=w_gate.astype(BF16), b_gate=b_gate,
        g_norm1=g_norm1[l][None], g_norm2=g_norm2[l][None],
        g_out=gla_g_out[l][None], head_sum=head_sum,
        conf_w=conf_w_dw[l], conf_b=conf_b_dw[l][None], conf_g=conf_ln_g[l][None],
        conf_beta=conf_ln_b[l][None], sc_w=sc_w_dw[l], pool_bd=pool_bd.astype(BF16),
        pool_scale=pool_scale[l][None], w_out=w_out[l].astype(BF16),
        w_router=w_router.astype(BF16), b_router=b_router)


def kernel(x, c, ctx, c_ctx, w_mod, b_mod, g_norm1, g_norm2, w_in, gla_w_gate_f, gla_b_gate_f,
           gla_w_gate_b, gla_b_gate_b, gla_g_out, conf_w_dw, conf_b_dw, conf_ln_g, conf_ln_b,
           sc_w_dw, pool_w, pool_scale, w_out, router_w_group, router_b_group, router_w_expert,
           router_b_expert, expert_w_in, expert_w_out, g_final):
    bsz, seq, d = x.shape
    ctx_len = ctx.shape[1]
    depth = w_mod.shape[0]
    n_lat = bsz * seq
    n_ctx = bsz * ctx_len
    assert d == D_MODEL and seq % 1024 == 0 and ctx_len % GLA_GROUP == 0 and bsz <= 7
    tm = 512
    tm_c = 256

    cv = jnp.zeros((8, d), F32).at[0:bsz].set(c).at[bsz].set(c_ctx)
    mods = _adaln(cv, w_mod, b_mod).reshape(depth, 8, 6, d)
    gf = g_final[None]
    zero_state = jnp.zeros((bsz, 2, GLA_QK_W, GROUP_W), F32)

    for l in range(depth):
        last = l == depth - 1
        p = _layer_params(l, w_in, gla_w_gate_f, gla_b_gate_f, gla_w_gate_b, gla_b_gate_b,
                          gla_g_out, conf_w_dw, conf_b_dw, conf_ln_g, conf_ln_b, sc_w_dw, pool_w,
                          pool_scale, w_out, router_w_group, router_b_group, router_w_expert,
                          router_b_expert, expert_w_in, expert_w_out, g_norm1, g_norm2)
        mod = mods[l]

        c_qkvg, c_la, c_u, c_sc, c_pool = _inproj(ctx, mod, bsz, p["g_norm1"], p["w_in"],
                                                  p["w_gate"], p["b_gate"], tm_c)
        c_of, c_ob, s_ctx = _gla(c_qkvg, c_la, zero_state, ctx_len)
        x_qkvg, x_la, x_u, x_sc, x_pool = _inproj(x, mod, None, p["g_norm1"], p["w_in"],
                                                  p["w_gate"], p["b_gate"], tm)
        x_of, x_ob, _ = _gla(x_qkvg, x_la, s_ctx, 512)
        x_loc = _local_axial(x_u, x_sc, x_pool, p, 1024)
        x, m_x, r_x = _outproj(x_of, x_ob, x_qkvg, x_loc, x, mod, None, p, tm)

        if last:
            m_c = None
            route = r_x.reshape(n_lat, ROUTE_W)
        else:
            c_loc = _local_seq(c_u, c_sc, c_pool, p)
            ctx, m_c, r_c = _outproj(c_of, c_ob, c_qkvg, c_loc, ctx, mod, bsz, p, tm_c)
            m_c = m_c.reshape(n_ctx, d)
            route = jnp.concatenate([r_x.reshape(n_lat, ROUTE_W), r_c.reshape(n_ctx, ROUTE_W)], axis=0)

        plan = _plan(route)
        xs = _dispatch(m_x.reshape(n_lat, d), m_c, plan)
        ys = _experts(xs, plan, expert_w_in, expert_w_out, l)
        x = _combine(x, ys, route, plan, 0, mod, None, gf, last)
        if not last:
            ctx = _combine(ctx, ys, route, plan, n_lat, mod, bsz, gf, False)
    return x
```

```python
import functools

import numpy as np
import jax
import jax.numpy as jnp
from jax import lax
from jax.experimental import pallas as pl
from jax.experimental.pallas import tpu as pltpu

F32 = jnp.float32
BF16 = jnp.bfloat16

D_MODEL = 1024
GRID_W = 64
GROUP_W = 256
GLA_HEADS = 4
GLA_DV = 64
GLA_DK = 32
GLA_QK_W = 128
GLA_GATE_RANK = 16
GLA_GATE_NORM = 16.0
GLA_CHUNK = 64
GLA_GROUP = 256
CONF_WIDTH = 31
SC_WIDTH = 3
POOL_WINDOWS = (2, 4, 8, 16)
POOL_GW = 64
N_GROUPS = 4
EXPERTS_PER_GROUP = 8
N_EXPERTS = 32
D_EXPERT = 512
MOE_BLOCK = 512
NORM_EPS = 1e-6
IN_SPLITS = (128, 128, 256, 256, 16, 16, 512, 256, 256, 256, 256)
N_IN_PAD = 2432
LANE = 128
ROUTE_W = LANE

VMEM_LIMIT = 48 << 20


def _cp(sem, vmem=VMEM_LIMIT):
    return pltpu.CompilerParams(dimension_semantics=sem, vmem_limit_bytes=vmem)


def _sigmoid(x):
    return 1.0 / (1.0 + jnp.exp(-x))


def _silu(x):
    return x * _sigmoid(x)


def _log_sigmoid(x):
    return jnp.minimum(x, 0.0) - jnp.log1p(jnp.exp(-jnp.abs(x)))


def _dot(a, b):
    return jnp.dot(a.astype(BF16), b.astype(BF16), preferred_element_type=F32)


def _rms(x, g):
    return x * lax.rsqrt(jnp.mean(x * x, axis=-1, keepdims=True) + NORM_EPS) * g


def _adaln_kernel(cv_ref, w_ref, b_ref, o_ref):
    o_ref[0] = _dot(_silu(cv_ref[...]), w_ref[0]) + b_ref[0]


def _adaln(cv, w_mod, b_mod):
    depth, d, n = w_mod.shape
    tn = 1024
    return pl.pallas_call(
        _adaln_kernel,
        grid=(depth, n // tn),
        in_specs=[pl.BlockSpec((8, d), lambda l, j: (0, 0)),
                  pl.BlockSpec((1, d, tn), lambda l, j: (l, 0, j)),
                  pl.BlockSpec((1, 1, tn), lambda l, j: (l, 0, j))],
        out_specs=pl.BlockSpec((1, 8, tn), lambda l, j: (l, 0, j)),
        out_shape=jax.ShapeDtypeStruct((depth, 8, n), F32),
        compiler_params=_cp(("parallel", "parallel")),
        name="adaln",
    )(cv, w_mod, b_mod.reshape(depth, 1, n))


def _inproj_kernel(x_ref, mod_ref, g1_ref, w_ref, wg_ref, bg_ref,
                   qkvg_ref, la_ref, u_ref, sc_ref, pool_ref):
    x = x_ref[0]
    mod = mod_ref[0]
    m = _rms(x, g1_ref[...]) * (1.0 + mod[1:2]) + mod[0:1]
    h = _dot(m, w_ref[...])
    qkvg_ref[0] = h[:, 0:768]
    u_ref[0] = h[:, 768:1024] * _sigmoid(h[:, 1024:1280])
    sc_ref[0, :, 0:256] = h[:, 1280:1536]
    sc_ref[0, :, 256:512] = h[:, 1536:1792] * h[:, 1792:2048]
    pool_ref[0] = h[:, 2048:2304]
    z = _dot(h[:, 2304:2432], wg_ref[...]) + bg_ref[...]
    la_ref[0] = _log_sigmoid(z) / GLA_GATE_NORM


def _inproj(x, mod, mod_row, g1, w_in_p, w_gate, b_gate, tm):
    bsz, seq, d = x.shape
    row = (lambda b: b) if mod_row is None else (lambda b: mod_row)
    tok = lambda w: pl.BlockSpec((1, tm, w), lambda b, i: (b, i, 0))
    full = lambda a: pl.BlockSpec(a.shape, lambda b, i: (0,) * a.ndim)
    outs = (768, 256, 256, 512, 256)
    return pl.pallas_call(
        _inproj_kernel,
        grid=(bsz, seq // tm),
        in_specs=[tok(d),
                  pl.BlockSpec((1, 6, d), lambda b, i: (row(b), 0, 0)),
                  full(g1), full(w_in_p), full(w_gate), full(b_gate)],
        out_specs=[tok(w) for w in outs],
        out_shape=[jax.ShapeDtypeStruct((bsz, seq, w), F32) for w in outs],
        compiler_params=_cp(("parallel", "parallel")),
        name="inproj",
    )(x, mod, g1, w_in_p, w_gate, b_gate)


def _gla_direction(q, k, v, la, state, reverse):
    blk = q.shape[0]
    g = GLA_GROUP
    nchunk = g // GLA_CHUNK
    ti = lax.broadcasted_iota(jnp.int32, (g, g), 0)
    tj = lax.broadcasted_iota(jnp.int32, (g, g), 1)
    same = (ti // GLA_CHUNK) == (tj // GLA_CHUNK)
    cmask = same & ((tj >= ti) if reverse else (tj <= ti))
    tri = jnp.where(cmask, 1.0, 0.0).astype(BF16)
    qk_head = lax.broadcasted_iota(jnp.int32, (g, GLA_QK_W), 1) // GLA_DK
    v_head = lax.broadcasted_iota(jnp.int32, (g, GROUP_W), 1) // GLA_DV
    bd = (lax.broadcasted_iota(jnp.int32, (GLA_QK_W, GROUP_W), 0) // GLA_DK
          == lax.broadcasted_iota(jnp.int32, (GLA_QK_W, GROUP_W), 1) // GLA_DV)
    tok_chunk = lax.broadcasted_iota(jnp.int32, (GLA_QK_W, g), 1) // GLA_CHUNK
    scale = GLA_DK ** -0.5
    outs = [None] * (blk // g)
    groups = range(blk // g)
    for gi in (reversed(groups) if reverse else groups):
        sl = slice(gi * g, (gi + 1) * g)
        qg, kg, vg, lg = q[sl], k[sl], v[sl], la[sl]
        l0 = lg.astype(BF16)
        r0 = lg - l0.astype(F32)
        l1 = r0.astype(BF16)
        l2 = (r0 - l1.astype(F32)).astype(BF16)
        b = (jnp.dot(tri, l0, preferred_element_type=F32)
             + jnp.dot(tri, l1, preferred_element_type=F32)
             + jnp.dot(tri, l2, preferred_element_type=F32))
        last = 0 if reverse else GLA_CHUNK - 1
        blast = jnp.concatenate(
            [jnp.broadcast_to(b[c * GLA_CHUNK + last:c * GLA_CHUNK + last + 1], (GLA_CHUNK, GLA_QK_W))
             for c in range(nchunk)], axis=0)
        qd = (qg * scale) * jnp.exp(b)
        ki = (kg * jnp.exp(-b)).astype(BF16)
        kw_t = (kg * jnp.exp(blast - b)).T
        b_t = b.T
        vb = vg.astype(BF16)
        o = jnp.zeros((g, GROUP_W), F32)
        for h in range(GLA_HEADS):
            qh = jnp.where(qk_head == h, qd, 0.0).astype(BF16)
            s = lax.dot_general(qh, ki, (((1,), (1,)), ((), ())), preferred_element_type=F32)
            s = jnp.where(cmask, s, 0.0).astype(BF16)
            oh = jnp.dot(s, vb, preferred_element_type=F32)
            o = o + jnp.where(v_head == h, oh, 0.0)
        qdb = qd.astype(BF16)
        inter = [None] * nchunk
        chunks = range(nchunk)
        for c in (reversed(chunks) if reverse else chunks):
            rows = slice(c * GLA_CHUNK, (c + 1) * GLA_CHUNK)
            inter[c] = jnp.dot(qdb[rows], state.astype(BF16), preferred_element_type=F32)
            col = c * GLA_CHUNK + last
            dec = jnp.exp(b_t[:, col:col + 1])
            kv = jnp.dot(jnp.where(tok_chunk == c, kw_t, 0.0).astype(BF16), vb,
                         preferred_element_type=F32)
            state = dec * state + jnp.where(bd, kv, 0.0)
        outs[gi] = o + jnp.concatenate(inter, axis=0)
    return jnp.concatenate(outs, axis=0) if len(outs) > 1 else outs[0], state


def _gla_kernel(qk_f, v_f, la_f, qk_b, v_b, la_b, s0_ref, of_ref, ob_ref, sfin_ref, st_ref):
    i = pl.program_id(1)

    @pl.when(i == 0)
    def _():
        st_ref[...] = s0_ref[0]

    qk = qk_f[0]
    o, s = _gla_direction(qk[:, 0:128], qk[:, 128:256], v_f[0], la_f[0], st_ref[0], False)
    of_ref[0] = o
    st_ref[0] = s
    qk = qk_b[0]
    o, s = _gla_direction(qk[:, 0:128], qk[:, 128:256], v_b[0], la_b[0], st_ref[1], True)
    ob_ref[0] = o
    st_ref[1] = s

    @pl.when(i == pl.num_programs(1) - 1)
    def _():
        sfin_ref[0] = st_ref[...]


def _gla(qkvg, la, s0, blk):
    bsz, seq, _ = qkvg.shape
    nb = seq // blk
    fwd = lambda j: pl.BlockSpec((1, blk, 256), lambda b, i: (b, i, j))
    bwd = lambda j: pl.BlockSpec((1, blk, 256), lambda b, i: (b, nb - 1 - i, j))
    la_f = pl.BlockSpec((1, blk, 128), lambda b, i: (b, i, 0))
    la_b = pl.BlockSpec((1, blk, 128), lambda b, i: (b, nb - 1 - i, 1))
    st = pl.BlockSpec((1, 2, GLA_QK_W, GROUP_W), lambda b, i: (b, 0, 0, 0))
    return pl.pallas_call(
        _gla_kernel,
        grid=(bsz, nb),
        in_specs=[fwd(0), fwd(1), la_f, bwd(0), bwd(1), la_b, st],
        out_specs=[pl.BlockSpec((1, blk, GROUP_W), lambda b, i: (b, i, 0)),
                   pl.BlockSpec((1, blk, GROUP_W), lambda b, i: (b, nb - 1 - i, 0)),
                   st],
        out_shape=[jax.ShapeDtypeStruct((bsz, seq, GROUP_W), F32),
                   jax.ShapeDtypeStruct((bsz, seq, GROUP_W), F32),
                   jax.ShapeDtypeStruct((bsz, 2, GLA_QK_W, GROUP_W), F32)],
        scratch_shapes=[pltpu.VMEM((2, GLA_QK_W, GROUP_W), F32)],
        compiler_params=_cp(("parallel", "arbitrary")),
        name="gla",
    )(qkvg, qkvg, la, qkvg, qkvg, la, s0)


def _row_shift(x, s, col, row_len):
    if s == 0:
        return x
    n = x.shape[0]
    y = pltpu.roll(x, (-s) % n, 0)
    valid = (col < row_len - s) if s > 0 else (col >= -s)
    return jnp.where(valid, y, 0.0)


def _row_conv(x, w, col, row_len):
    k = w.shape[0]
    acc = None
    for j in range(k):
        term = _row_shift(x, j - k // 2, col, row_len) * w[j:j + 1]
        acc = term if acc is None else acc + term
    return acc


def _col_conv(pad_ref, w, t0, n):
    k = w.shape[0]
    acc = None
    for j in range(k):
        start = pl.multiple_of(t0 + j * GRID_W, GRID_W)
        term = pad_ref[pl.ds(start, n), :] * w[j:j + 1]
        acc = term if acc is None else acc + term
    return acc


def _pool_diffs(x, col, row_len):
    colh = col[:, 0:LANE]
    colf = colh.astype(F32)
    first = lax.broadcasted_iota(jnp.int32, colh.shape, 1) < POOL_GW

    def count(win):
        return jnp.minimum(colf + win // 2, float(row_len)) - jnp.maximum(colf - win // 2, 0.0)

    means = []
    for hi in range(len(POOL_WINDOWS) // 2):
        wa, wb = POOL_WINDOWS[2 * hi], POOL_WINDOWS[2 * hi + 1]
        xh = x[:, hi * LANE:(hi + 1) * LANE]
        acc_a = acc_b = None
        for s in range(-(wb // 2), wb // 2):
            t = _row_shift(xh, s, colh, row_len)
            acc_b = t if acc_b is None else acc_b + t
            if -(wa // 2) <= s < wa // 2:
                acc_a = t if acc_a is None else acc_a + t
        means.append(jnp.where(first, acc_a / count(wa), acc_b / count(wb)))
    return jnp.concatenate(means, axis=-1) - x


def _local_tail(u, scb, conv_sc, pool, col, row_len, cb, lg, lb, pw, ps, out_ref):
    u = u + cb
    uc = u - jnp.mean(u, axis=-1, keepdims=True)
    ln = uc * lax.rsqrt(jnp.mean(uc * uc, axis=-1, keepdims=True) + NORM_EPS) * lg + lb
    out_ref[0, :, 0:256] = _silu(ln).astype(BF16)
    out_ref[0, :, 256:512] = (scb * conv_sc).astype(BF16)
    out_ref[0, :, 512:768] = (_dot(_pool_diffs(pool, col, row_len), pw) * ps).astype(BF16)


def _local_axial_kernel(uh_ref, uv_ref, scb_ref, sch_ref, scv_ref, pool_ref,
                        cw_ref, cb_ref, lg_ref, lb_ref, sw_ref, pw_ref, ps_ref,
                        out_ref, upad, spad):
    i = pl.program_id(1)
    blk = uh_ref.shape[1]
    seq = uv_ref.shape[1]
    pu = (CONF_WIDTH // 2) * GRID_W
    psc = (SC_WIDTH // 2) * GRID_W

    @pl.when(i == 0)
    def _():
        upad[0:pu, :] = jnp.zeros((pu, LANE), F32)
        upad[pu:pu + seq, :] = uv_ref[0]
        upad[pu + seq:pu + seq + pu, :] = jnp.zeros((pu, LANE), F32)
        spad[0:psc, :] = jnp.zeros((psc, LANE), F32)
        spad[psc:psc + seq, :] = scv_ref[0]
        spad[psc + seq:psc + seq + psc, :] = jnp.zeros((psc, LANE), F32)

    t0 = i * blk
    col = lax.broadcasted_iota(jnp.int32, (blk, LANE), 0) % GRID_W
    cw = cw_ref[...]
    sw = sw_ref[...]
    u = jnp.concatenate([_row_conv(uh_ref[0], cw[:, 0:128], col, GRID_W),
                         _col_conv(upad, cw[:, 128:256], t0, blk)], axis=-1)
    csc = jnp.concatenate([_row_conv(sch_ref[0], sw[:, 0:128], col, GRID_W),
                           _col_conv(spad, sw[:, 128:256], t0, blk)], axis=-1)
    col2 = jnp.concatenate([col, col], axis=-1)
    _local_tail(u, scb_ref[0], csc, pool_ref[0], col2, GRID_W, cb_ref[...], lg_ref[...],
                lb_ref[...], pw_ref[...], ps_ref[...], out_ref)


def _local_seq_kernel(u_ref, sc_ref, pool_ref, cw_ref, cb_ref, lg_ref, lb_ref, sw_ref,
                      pw_ref, ps_ref, out_ref):
    seq = u_ref.shape[1]
    col = lax.broadcasted_iota(jnp.int32, (seq, GROUP_W), 0)
    sc = sc_ref[0]
    u = _row_conv(u_ref[0], cw_ref[...], col, seq)
    csc = _row_conv(sc[:, 256:512], sw_ref[...], col, seq)
    _local_tail(u, sc[:, 0:256], csc, pool_ref[0], col, seq, cb_ref[...], lg_ref[...],
                lb_ref[...], pw_ref[...], ps_ref[...], out_ref)


def _local_params(p):
    return (p["conf_w"], p["conf_b"], p["conf_g"], p["conf_beta"], p["sc_w"], p["pool_bd"],
            p["pool_scale"])


def _local_axial(u, sc, pool, p, blk):
    bsz, seq, _ = u.shape
    cw, cb, lg, lb, sw, pw, ps = _local_params(p)
    blkspec = lambda w, j: pl.BlockSpec((1, blk, w), lambda b, i: (b, i, j))
    seqspec = lambda j: pl.BlockSpec((1, seq, LANE), lambda b, i: (b, 0, j))
    full = lambda a: pl.BlockSpec(a.shape, lambda b, i: (0,) * a.ndim)
    pu = (CONF_WIDTH // 2) * GRID_W
    psc = (SC_WIDTH // 2) * GRID_W
    return pl.pallas_call(
        _local_axial_kernel,
        grid=(bsz, seq // blk),
        in_specs=[blkspec(LANE, 0), seqspec(1), blkspec(256, 0), blkspec(LANE, 2), seqspec(3),
                  blkspec(256, 0), full(cw), full(cb), full(lg), full(lb), full(sw), full(pw),
                  full(ps)],
        out_specs=pl.BlockSpec((1, blk, 768), lambda b, i: (b, i, 0)),
        out_shape=jax.ShapeDtypeStruct((bsz, seq, 768), BF16),
        scratch_shapes=[pltpu.VMEM((seq + 2 * pu, LANE), F32),
                        pltpu.VMEM((seq + 2 * psc, LANE), F32)],
        compiler_params=_cp(("parallel", "arbitrary")),
        name="local_axial",
    )(u, u, sc, sc, sc, pool, cw, cb, lg, lb, sw, pw, ps)


def _local_seq(u, sc, pool, p):
    bsz, seq, _ = u.shape
    cw, cb, lg, lb, sw, pw, ps = _local_params(p)
    tok = lambda w: pl.BlockSpec((1, seq, w), lambda b: (b, 0, 0))
    full = lambda a: pl.BlockSpec(a.shape, lambda b: (0,) * a.ndim)
    return pl.pallas_call(
        _local_seq_kernel,
        grid=(bsz,),
        in_specs=[tok(256), tok(512), tok(256), full(cw), full(cb), full(lg), full(lb),
                  full(sw), full(pw), full(ps)],
        out_specs=tok(768),
        out_shape=jax.ShapeDtypeStruct((bsz, seq, 768), BF16),
        compiler_params=_cp(("parallel",)),
        name="local_seq",
    )(u, sc, pool, cw, cb, lg, lb, sw, pw, ps)


def _route(logits):
    lane = lax.broadcasted_iota(jnp.int32, logits.shape, 1)
    neg = -jnp.inf
    big = ROUTE_W

    def first_max(vals):
        mx = jnp.max(vals, axis=-1, keepdims=True)
        idx = jnp.min(jnp.where(vals == mx, lane, big), axis=-1, keepdims=True)
        return mx, idx

    lg = jnp.where(lane < N_GROUPS, logits, neg)
    gmx, grp = first_max(lg)
    p_grp = 1.0 / jnp.sum(jnp.exp(lg - gmx), axis=-1, keepdims=True)
    lo = N_GROUPS + grp * EXPERTS_PER_GROUP
    le = jnp.where((lane >= lo) & (lane < lo + EXPERTS_PER_GROUP), logits, neg)
    v1, i1 = first_max(le)
    v2, i2 = first_max(jnp.where(lane == i1, neg, le))
    e2 = jnp.exp(v2 - v1)
    w1 = p_grp / (1.0 + e2)
    w2 = p_grp * e2 / (1.0 + e2)
    rec = jnp.where(lane == 0, (i1 - N_GROUPS).astype(F32), 0.0)
    rec = jnp.where(lane == 1, (i2 - N_GROUPS).astype(F32), rec)
    rec = jnp.where(lane == 2, w1, rec)
    return jnp.where(lane == 3, w2, rec)


def _outproj_kernel(of_ref, ob_ref, g_ref, loc_ref, x_ref, mod_ref, gout_ref, hsum_ref,
                    wo_ref, g2_ref, wr_ref, br_ref, xo_ref, m_ref, route_ref):
    mod = mod_ref[0]
    o = of_ref[0] + ob_ref[0]
    o2 = o * o
    hi = o2.astype(BF16)
    lo = (o2 - hi.astype(F32)).astype(BF16)
    ms = (jnp.dot(hi, hsum_ref[...], preferred_element_type=F32)
          + jnp.dot(lo, hsum_ref[...], preferred_element_type=F32)) / GLA_DV
    y_gla = o * lax.rsqrt(ms + NORM_EPS) * gout_ref[...] * _silu(g_ref[0])
    wo = wo_ref[...]
    proj = _dot(y_gla, wo[0:256]) + _dot(loc_ref[0], wo[256:1024])
    x = x_ref[0] + mod[2:3] * proj
    xo_ref[0] = x
    m = _rms(x, g2_ref[...]) * (1.0 + mod[4:5]) + mod[3:4]
    m_ref[0] = m
    route_ref[0] = _route(_dot(m, wr_ref[...]) + br_ref[...])


def _outproj(o_f, o_b, qkvg, y_loc, x, mod, mod_row, p, tm):
    bsz, seq, d = x.shape
    row = (lambda b: b) if mod_row is None else (lambda b: mod_row)
    tok = lambda w, j=0: pl.BlockSpec((1, tm, w), lambda b, i: (b, i, j))
    full = lambda a: pl.BlockSpec(a.shape, lambda b, i: (0,) * a.ndim)
    consts = (p["g_out"], p["head_sum"], p["w_out"], p["g_norm2"], p["w_router"], p["b_router"])
    return pl.pallas_call(
        _outproj_kernel,
        grid=(bsz, seq // tm),
        in_specs=[tok(256), tok(256), tok(256, 2), tok(768), tok(d),
                  pl.BlockSpec((1, 6, d), lambda b, i: (row(b), 0, 0))] + [full(a) for a in consts],
        out_specs=[tok(d), tok(d), tok(ROUTE_W)],
        out_shape=[jax.ShapeDtypeStruct((bsz, seq, d), F32),
                   jax.ShapeDtypeStruct((bsz, seq, d), F32),
                   jax.ShapeDtypeStruct((bsz, seq, ROUTE_W), F32)],
        compiler_params=_cp(("parallel", "parallel")),
        name="outproj",
    )(o_f, o_b, qkvg, y_loc, x, mod, *consts)


RANK_ROWS = 512


def _rank_kernel(route_ref, rank_ref, cnt_ref, carry):
    i = pl.program_id(0)

    @pl.when(i == 0)
    def _():
        carry[...] = jnp.zeros_like(carry)

    rec = route_ref[...]
    tm = rec.shape[0]
    lane = lax.broadcasted_iota(jnp.int32, rec.shape, 1).astype(F32)
    oh0 = jnp.where(lane == rec[:, 0:1], 1.0, 0.0)
    oh1 = jnp.where(lane == rec[:, 1:2], 1.0, 0.0)
    ti = lax.broadcasted_iota(jnp.int32, (tm, tm), 0)
    tj = lax.broadcasted_iota(jnp.int32, (tm, tm), 1)
    before = jnp.where(tj < ti, 1.0, 0.0).astype(BF16)
    tot0 = jnp.sum(oh0, axis=0, keepdims=True)
    tot1 = jnp.sum(oh1, axis=0, keepdims=True)
    base = carry[...]
    pre0 = jnp.dot(before, oh0.astype(BF16), preferred_element_type=F32) + base
    pre1 = jnp.dot(before, oh1.astype(BF16), preferred_element_type=F32) + (base + tot0)
    r0 = jnp.sum(oh0 * pre0, axis=-1, keepdims=True)
    r1 = jnp.sum(oh1 * pre1, axis=-1, keepdims=True)
    lane_i = lax.broadcasted_iota(jnp.int32, rec.shape, 1)
    rank_ref[...] = jnp.where(lane_i == 0, r0, jnp.where(lane_i == 1, r1, 0.0)).astype(jnp.int32)
    carry[...] = base + tot0 + tot1
    cnt_ref[...] = carry[...]


def _rank(route):
    n_tok = route.shape[0]
    tm = RANK_ROWS if n_tok % RANK_ROWS == 0 else 256
    return pl.pallas_call(
        _rank_kernel,
        grid=(n_tok // tm,),
        in_specs=[pl.BlockSpec((tm, ROUTE_W), lambda i: (i, 0))],
        out_specs=[pl.BlockSpec((tm, ROUTE_W), lambda i: (i, 0)),
                   pl.BlockSpec((1, ROUTE_W), lambda i: (0, 0))],
        out_shape=[jax.ShapeDtypeStruct((n_tok, ROUTE_W), jnp.int32),
                   jax.ShapeDtypeStruct((1, ROUTE_W), F32)],
        scratch_shapes=[pltpu.VMEM((1, ROUTE_W), F32)],
        compiler_params=_cp(("arbitrary",)),
        name="rank",
    )(route)


def _plan(route):
    n_tok = route.shape[0]
    rank, cnt = _rank(route)
    counts = cnt[0, 0:N_EXPERTS].astype(jnp.int32)
    padded = (counts + MOE_BLOCK - 1) // MOE_BLOCK * MOE_BLOCK
    pad_end = jnp.cumsum(padded)
    pad_start = pad_end - padded
    eid = route[:, 0:2].astype(jnp.int32)
    onehot = eid[:, :, None] == jnp.arange(N_EXPERTS, dtype=jnp.int32)[None, None, :]
    dest = (jnp.sum(jnp.where(onehot, pad_start[None, None, :], 0), axis=-1) + rank[:, 0:2]).reshape(-1)
    n_blocks = 2 * n_tok // MOE_BLOCK + N_EXPERTS
    blk_start = jnp.arange(n_blocks, dtype=jnp.int32) * MOE_BLOCK
    nused = pad_end[-1] // MOE_BLOCK
    blk = jnp.minimum(jnp.arange(n_blocks, dtype=jnp.int32), nused - 1)
    blk_expert = jnp.sum((blk[:, None] * MOE_BLOCK >= pad_end[None, :]).astype(jnp.int32), axis=-1)
    fill_start = pad_start + counts
    fill_n = padded - counts
    return dict(dest=dest.astype(jnp.int32), blk=blk.astype(jnp.int32),
                blk_expert=jnp.minimum(blk_expert, N_EXPERTS - 1).astype(jnp.int32),
                nused=nused.astype(jnp.int32).reshape(1), fill_start=fill_start.astype(jnp.int32),
                fill_n=fill_n.astype(jnp.int32), n_blocks=n_blocks)


DISPATCH_ROWS = 256


def _dispatch_kernel(dest_ref, fstart_ref, fn_ref, nused_ref, *refs, tr, n_x, n_c):
    if n_c:
        mx_hbm, mc_hbm, xs_hbm, buf, zbuf, in_sem, out_sem, fsem = refs
    else:
        mx_hbm, xs_hbm, buf, zbuf, in_sem, out_sem, fsem = refs
        mc_hbm = None
    i = pl.program_id(0)
    nsteps = n_x + n_c
    nslot = 3

    def start_load(step):
        slot = step % nslot

        @pl.when(step < n_x)
        def _():
            rows = pl.ds(pl.multiple_of(step * tr, tr), tr)
            pltpu.make_async_copy(mx_hbm.at[rows], buf.at[slot], in_sem.at[slot]).start()

        if n_c:
            @pl.when(step >= n_x)
            def _():
                rows = pl.ds(pl.multiple_of((step - n_x) * tr, tr), tr)
                pltpu.make_async_copy(mc_hbm.at[rows], buf.at[slot], in_sem.at[slot]).start()

    def wait_load(step):
        slot = step % nslot
        pltpu.make_async_copy(mx_hbm.at[pl.ds(0, tr)], buf.at[slot], in_sem.at[slot]).wait()

    def row_copy(step, r, k):
        slot = step % nslot
        dst = dest_ref[2 * (step * tr + r) + k]
        return pltpu.make_async_copy(buf.at[slot, pl.ds(r, 1)], xs_hbm.at[pl.ds(dst, 1)],
                                     out_sem.at[slot])

    def wait_scatter(step):
        slot = step % nslot
        for _ in range(2):
            pltpu.make_async_copy(buf.at[slot], xs_hbm.at[pl.ds(0, tr)], out_sem.at[slot]).wait()

    @pl.when(i == 0)
    def _():
        zbuf[...] = jnp.zeros_like(zbuf)
        start_load(i)

    @pl.when(i >= 2)
    def _():
        wait_scatter(i - 2)

    @pl.when(i + 1 < nsteps)
    def _():
        start_load(i + 1)

    wait_load(i)

    for r in range(tr):
        row_copy(i, r, 0).start()
        row_copy(i, r, 1).start()

    @pl.when(i == nsteps - 1)
    def _():
        if nsteps >= 2:
            wait_scatter(i - 1)
        wait_scatter(i)

        def fill_copy(e, r):
            return pltpu.make_async_copy(zbuf.at[pl.ds(0, 1)],
                                         xs_hbm.at[pl.ds(fstart_ref[e] + r, 1)], fsem.at[0])

        def per_expert(e, carry):
            lax.fori_loop(0, fn_ref[e], lambda r, c: (fill_copy(e, r).start(), c)[1], 0)
            return carry

        def per_expert_wait(e, carry):
            lax.fori_loop(0, fn_ref[e], lambda r, c: (fill_copy(e, r).wait(), c)[1], 0)
            return carry

        lax.fori_loop(0, N_EXPERTS, per_expert, 0)
        lax.fori_loop(0, N_EXPERTS, per_expert_wait, 0)

        def tail_copy(b):
            start = pl.multiple_of(b * MOE_BLOCK, MOE_BLOCK)
            return pltpu.make_async_copy(zbuf, xs_hbm.at[pl.ds(start, MOE_BLOCK)], fsem.at[0])

        nblk = xs_hbm.shape[0] // MOE_BLOCK
        lax.fori_loop(nused_ref[0], nblk, lambda b, c: (tail_copy(b).start(), c)[1], 0)
        lax.fori_loop(nused_ref[0], nblk, lambda b, c: (tail_copy(b).wait(), c)[1], 0)


def _dispatch(m_x, m_c, plan):
    n_lat, d = m_x.shape
    n_ctx = 0 if m_c is None else m_c.shape[0]
    tr = DISPATCH_ROWS
    while n_lat % tr or n_ctx % tr:
        tr //= 2
    slots = plan["n_blocks"] * MOE_BLOCK
    n_x, n_c = n_lat // tr, n_ctx // tr
    srcs = (m_x,) if m_c is None else (m_x, m_c)
    grid_spec = pltpu.PrefetchScalarGridSpec(
        num_scalar_prefetch=4,
        grid=(n_x + n_c,),
        in_specs=[pl.BlockSpec(memory_space=pl.ANY)] * len(srcs),
        out_specs=pl.BlockSpec(memory_space=pl.ANY),
        scratch_shapes=[pltpu.VMEM((3, tr, d), F32), pltpu.VMEM((MOE_BLOCK, d), F32),
                        pltpu.SemaphoreType.DMA((3,)), pltpu.SemaphoreType.DMA((3,)),
                        pltpu.SemaphoreType.DMA((1,))],
    )
    return pl.pallas_call(
        functools.partial(_dispatch_kernel, tr=tr, n_x=n_x, n_c=n_c),
        grid_spec=grid_spec,
        out_shape=jax.ShapeDtypeStruct((slots, d), F32),
        compiler_params=_cp(("arbitrary",)),
        name="dispatch",
    )(plan["dest"], plan["fill_start"], plan["fill_n"], plan["nused"], *srcs)


def _expert_kernel(blk_ref, be_ref, nused_ref, xs_ref, w1_ref, w2_ref, ys_ref, w1b, w2b):
    i = pl.program_id(0)
    used = i < nused_ref[0]
    fresh = (i == 0) | (be_ref[i] != be_ref[jnp.maximum(i - 1, 0)])

    @pl.when(used & fresh)
    def _():
        w1b[...] = w1_ref[0, 0].astype(BF16)
        w2b[...] = w2_ref[0, 0].astype(BF16)

    @pl.when(used)
    def _():
        h = _dot(xs_ref[...], w1b[...])
        act = _silu(h[:, :D_EXPERT]) * h[:, D_EXPERT:]
        ys_ref[...] = _dot(act, w2b[...])

    @pl.when(jnp.logical_not(used))
    def _():
        ys_ref[...] = jnp.zeros_like(ys_ref)


def _experts(xs, plan, w1, w2, layer):
    slots, d = xs.shape
    grid_spec = pltpu.PrefetchScalarGridSpec(
        num_scalar_prefetch=3,
        grid=(plan["n_blocks"],),
        in_specs=[pl.BlockSpec((MOE_BLOCK, d), lambda i, blk, be, nu: (blk[i], 0)),
                  pl.BlockSpec((1, 1, d, 2 * D_EXPERT), lambda i, blk, be, nu: (layer, be[i], 0, 0)),
                  pl.BlockSpec((1, 1, D_EXPERT, d), lambda i, blk, be, nu: (layer, be[i], 0, 0))],
        out_specs=pl.BlockSpec((MOE_BLOCK, d), lambda i, blk, be, nu: (i, 0)),
        scratch_shapes=[pltpu.VMEM((d, 2 * D_EXPERT), BF16), pltpu.VMEM((D_EXPERT, d), BF16)],
    )
    return pl.pallas_call(
        _expert_kernel,
        grid_spec=grid_spec,
        out_shape=jax.ShapeDtypeStruct((slots, d), F32),
        compiler_params=_cp(("arbitrary",)),
        name="experts",
    )(plan["blk"], plan["blk_expert"], plan["nused"], xs, w1, w2)


COMBINE_ROWS = 256


def _combine_kernel(dest_ref, x_ref, route_ref, mod_ref, gf_ref, ys_hbm, o_ref, y0buf, y1buf, sem,
                    *, final, tok_off):
    i = pl.program_id(0)
    nsteps = pl.num_programs(0)
    tc = COMBINE_ROWS

    def row_copy(step, r, k, slot):
        a = 2 * (tok_off + step * tc + r) + k
        buf = y0buf if k == 0 else y1buf
        return pltpu.make_async_copy(ys_hbm.at[pl.ds(dest_ref[a], 1)], buf.at[slot, pl.ds(r, 1)],
                                     sem.at[slot])

    def issue_step(step):
        for r in range(tc):
            row_copy(step, r, 0, step % 2).start()
            row_copy(step, r, 1, step % 2).start()

    @pl.when(i == 0)
    def _():
        issue_step(0)

    @pl.when(i + 1 < nsteps)
    def _():
        issue_step(i + 1)

    slot = i % 2
    pltpu.make_async_copy(ys_hbm.at[pl.ds(0, tc)], y0buf.at[slot], sem.at[slot]).wait()
    pltpu.make_async_copy(ys_hbm.at[pl.ds(0, tc)], y1buf.at[slot], sem.at[slot]).wait()
    rec = route_ref[...]
    y = rec[:, 2:3] * y0buf[slot] + rec[:, 3:4] * y1buf[slot]
    x = x_ref[0] + mod_ref[0][5:6] * y
    o_ref[0] = _rms(x, gf_ref[...]) if final else x


def _combine(x, ys, route, plan, tok_off, mod, mod_row, g_final, final):
    bsz, seq, d = x.shape
    tc = COMBINE_ROWS
    nt = seq // tc
    assert seq % tc == 0 and tok_off % tc == 0
    row = (lambda i: i // nt) if mod_row is None else (lambda i: mod_row)
    grid_spec = pltpu.PrefetchScalarGridSpec(
        num_scalar_prefetch=1,
        grid=(bsz * nt,),
        in_specs=[pl.BlockSpec((1, tc, d), lambda i, de: (i // nt, i % nt, 0)),
                  pl.BlockSpec((tc, ROUTE_W), lambda i, de: (tok_off // tc + i, 0)),
                  pl.BlockSpec((1, 6, d), lambda i, de: (row(i), 0, 0)),
                  pl.BlockSpec((1, d), lambda i, de: (0, 0)),
                  pl.BlockSpec(memory_space=pl.ANY)],
        out_specs=pl.BlockSpec((1, tc, d), lambda i, de: (i // nt, i % nt, 0)),
        scratch_shapes=[pltpu.VMEM((2, tc, d), F32), pltpu.VMEM((2, tc, d), F32),
                        pltpu.SemaphoreType.DMA((2,))],
    )
    return pl.pallas_call(
        functools.partial(_combine_kernel, final=final, tok_off=tok_off),
        grid_spec=grid_spec,
        out_shape=jax.ShapeDtypeStruct((bsz, seq, d), F32),
        compiler_params=_cp(("arbitrary",)),
        name="combine",
    )(plan["dest"], x, route, mod, g_final, ys)


def _layer_params(l, w_in, gla_w_gate_f, gla_b_gate_f, gla_w_gate_b, gla_b_gate_b, gla_g_out,
                  conf_w_dw, conf_b_dw, conf_ln_g, conf_ln_b, sc_w_dw, pool_w, pool_scale, w_out,
                  router_w_group, router_b_group, router_w_expert, router_b_expert,
                  expert_w_in, expert_w_out, g_norm1, g_norm2):
    d = D_MODEL
    idx = np.cumsum((0,) + IN_SPLITS)
    q, k, v, g, zf, zb, conf, scb, scc, scx, pool = [w_in[l][:, idx[j]:idx[j + 1]] for j in range(11)]
    pad = jnp.zeros((d, N_IN_PAD - idx[-1]), F32)
    w_in_p = jnp.concatenate([q, k, v, g, conf, scb, scc, scx, pool, zf, zb, pad], axis=1).astype(BF16)
    r = GLA_GATE_RANK
    w_gate = jnp.zeros((LANE, 2 * GLA_QK_W), F32)
    w_gate = w_gate.at[0:r, 0:GLA_QK_W].set(gla_w_gate_f[l]).at[r:2 * r, GLA_QK_W:].set(gla_w_gate_b[l])
    b_gate = jnp.concatenate([gla_b_gate_f[l], gla_b_gate_b[l]])[None]
    pool_bd = jnp.zeros((GROUP_W, GROUP_W), F32)
    for gi in range(len(POOL_WINDOWS)):
        s = slice(gi * POOL_GW, (gi + 1) * POOL_GW)
        pool_bd = pool_bd.at[s, s].set(pool_w[l, gi])
    head = np.arange(GROUP_W) // GLA_DV
    head_sum = jnp.asarray(head[:, None] == head[None, :], BF16)
    w_router = jnp.zeros((d, ROUTE_W), F32)
    w_router = w_router.at[:, 0:N_GROUPS].set(router_w_group[l])
    w_router = w_router.at[:, N_GROUPS:N_GROUPS + N_EXPERTS].set(router_w_expert[l])
    b_router = jnp.zeros((1, ROUTE_W), F32)
    b_router = b_router.at[0, 0:N_GROUPS].set(router_b_group[l])
    b_router = b_router.at[0, N_GROUPS:N_GROUPS + N_EXPERTS].set(router_b_expert[l])
    return dict(
        w_in=w_in_p, w_gate=w_gate.astype(BF16), b_gate=b_gate,
        g_norm1=g_norm1[l][None], g_norm2=g_norm2[l][None],
        g_out=gla_g_out[l][None], head_sum=head_sum,
        conf_w=conf_w_dw[l], conf_b=conf_b_dw[l][None], conf_g=conf_ln_g[l][None],
        conf_beta=conf_ln_b[l][None], sc_w=sc_w_dw[l], pool_bd=pool_bd.astype(BF16),
        pool_scale=pool_scale[l][None], w_out=w_out[l].astype(BF16),
        w_router=w_router.astype(BF16), b_router=b_router)


def kernel(x, c, ctx, c_ctx, w_mod, b_mod, g_norm1, g_norm2, w_in, gla_w_gate_f, gla_b_gate_f,
           gla_w_gate_b, gla_b_gate_b, gla_g_out, conf_w_dw, conf_b_dw, conf_ln_g, conf_ln_b,
           sc_w_dw, pool_w, pool_scale, w_out, router_w_group, router_b_group, router_w_expert,
           router_b_expert, expert_w_in, expert_w_out, g_final):
    bsz, seq, d = x.shape
    ctx_len = ctx.shape[1]
    depth = w_mod.shape[0]
    n_lat = bsz * seq
    n_ctx = bsz * ctx_len
    assert d == D_MODEL and seq % 1024 == 0 and ctx_len % GLA_GROUP == 0 and bsz <= 7
    tm = 512
    tm_c = 256

    cv = jnp.zeros((8, d), F32).at[0:bsz].set(c).at[bsz].set(c_ctx)
    mods = _adaln(cv, w_mod, b_mod).reshape(depth, 8, 6, d)
    gf = g_final[None]
    zero_state = jnp.zeros((bsz, 2, GLA_QK_W, GROUP_W), F32)

    for l in range(depth):
        last = l == depth - 1
        p = _layer_params(l, w_in, gla_w_gate_f, gla_b_gate_f, gla_w_gate_b, gla_b_gate_b,
                          gla_g_out, conf_w_dw, conf_b_dw, conf_ln_g, conf_ln_b, sc_w_dw, pool_w,
                          pool_scale, w_out, router_w_group, router_b_group, router_w_expert,
                          router_b_expert, expert_w_in, expert_w_out, g_norm1, g_norm2)
        mod = mods[l]

        c_qkvg, c_la, c_u, c_sc, c_pool = _inproj(ctx, mod, bsz, p["g_norm1"], p["w_in"],
                                                  p["w_gate"], p["b_gate"], tm_c)
        c_of, c_ob, s_ctx = _gla(c_qkvg, c_la, zero_state, ctx_len)
        x_qkvg, x_la, x_u, x_sc, x_pool = _inproj(x, mod, None, p["g_norm1"], p["w_in"],
                                                  p["w_gate"], p["b_gate"], tm)
        x_of, x_ob, _ = _gla(x_qkvg, x_la, s_ctx, 512)
        x_loc = _local_axial(x_u, x_sc, x_pool, p, 1024)
        x, m_x, r_x = _outproj(x_of, x_ob, x_qkvg, x_loc, x, mod, None, p, tm)

        if last:
            m_c = None
            route = r_x.reshape(n_lat, ROUTE_W)
        else:
            c_loc = _local_seq(c_u, c_sc, c_pool, p)
            ctx, m_c, r_c = _outproj(c_of, c_ob, c_qkvg, c_loc, ctx, mod, bsz, p, tm_c)
            m_c = m_c.reshape(n_ctx, d)
            route = jnp.concatenate([r_x.reshape(n_lat, ROUTE_W), r_c.reshape(n_ctx, ROUTE_W)], axis=0)

        plan = _plan(route)
        xs = _dispatch(m_x.reshape(n_lat, d), m_c, plan)
        ys = _experts(xs, plan, expert_w_in, expert_w_out, l)
        x = _combine(x, ys, route, plan, 0, mod, None, gf, last)
        if not last:
            ctx = _combine(ctx, ys, route, plan, n_lat, mod, bsz, gf, False)
    return x
```

```python
import functools

import numpy as np
import jax
import jax.numpy as jnp
from jax import lax
from jax.experimental import pallas as pl
from jax.experimental.pallas import tpu as pltpu

F32 = jnp.float32
BF16 = jnp.bfloat16

D_MODEL = 1024
GRID_W = 64
GROUP_W = 256
GLA_HEADS = 4
GLA_DV = 64
GLA_DK = 32
GLA_QK_W = 128
GLA_GATE_RANK = 16
GLA_GATE_NORM = 16.0
GLA_CHUNK = 64
GLA_GROUP = 256
CONF_WIDTH = 31
SC_WIDTH = 3
POOL_WINDOWS = (2, 4, 8, 16)
POOL_GW = 64
N_GROUPS = 4
EXPERTS_PER_GROUP = 8
N_EXPERTS = 32
D_EXPERT = 512
MOE_BLOCK = 512
NORM_EPS = 1e-6
IN_SPLITS = (128, 128, 256, 256, 16, 16, 512, 256, 256, 256, 256)
N_IN_PAD = 2432
LANE = 128
ROUTE_W = LANE

VMEM_LIMIT = 48 << 20


def _cp(sem, vmem=VMEM_LIMIT):
    return pltpu.CompilerParams(dimension_semantics=sem, vmem_limit_bytes=vmem)


def _sigmoid(x):
    return 1.0 / (1.0 + jnp.exp(-x))


def _silu(x):
    return x * _sigmoid(x)


def _log_sigmoid(x):
    return jnp.minimum(x, 0.0) - jnp.log1p(jnp.exp(-jnp.abs(x)))


def _dot(a, b):
    return jnp.dot(a.astype(BF16), b.astype(BF16), preferred_element_type=F32)


def _rms(x, g):
    return x * lax.rsqrt(jnp.mean(x * x, axis=-1, keepdims=True) + NORM_EPS) * g


def _adaln_kernel(cv_ref, w_ref, b_ref, o_ref):
    o_ref[0] = _dot(_silu(cv_ref[...]), w_ref[0]) + b_ref[0]


def _adaln(cv, w_mod, b_mod):
    depth, d, n = w_mod.shape
    tn = 1024
    return pl.pallas_call(
        _adaln_kernel,
        grid=(depth, n // tn),
        in_specs=[pl.BlockSpec((8, d), lambda l, j: (0, 0)),
                  pl.BlockSpec((1, d, tn), lambda l, j: (l, 0, j)),
                  pl.BlockSpec((1, 1, tn), lambda l, j: (l, 0, j))],
        out_specs=pl.BlockSpec((1, 8, tn), lambda l, j: (l, 0, j)),
        out_shape=jax.ShapeDtypeStruct((depth, 8, n), F32),
        compiler_params=_cp(("parallel", "parallel")),
        name="adaln",
    )(cv, w_mod, b_mod.reshape(depth, 1, n))


def _inproj_kernel(x_ref, mod_ref, g1_ref, w_ref, wg_ref, bg_ref,
                   qkvg_ref, la_ref, u_ref, sc_ref, pool_ref):
    x = x_ref[0]
    mod = mod_ref[0]
    m = _rms(x, g1_ref[...]) * (1.0 + mod[1:2]) + mod[0:1]
    h = _dot(m, w_ref[...])
    qkvg_ref[0] = h[:, 0:768]
    u_ref[0] = h[:, 768:1024] * _sigmoid(h[:, 1024:1280])
    sc_ref[0, :, 0:256] = h[:, 1280:1536]
    sc_ref[0, :, 256:512] = h[:, 1536:1792] * h[:, 1792:2048]
    pool_ref[0] = h[:, 2048:2304]
    z = _dot(h[:, 2304:2432], wg_ref[...]) + bg_ref[...]
    la_ref[0] = _log_sigmoid(z) / GLA_GATE_NORM


def _inproj(x, mod, mod_row, g1, w_in_p, w_gate, b_gate, tm):
    bsz, seq, d = x.shape
    row = (lambda b: b) if mod_row is None else (lambda b: mod_row)
    tok = lambda w: pl.BlockSpec((1, tm, w), lambda b, i: (b, i, 0))
    full = lambda a: pl.BlockSpec(a.shape, lambda b, i: (0,) * a.ndim)
    outs = (768, 256, 256, 512, 256)
    return pl.pallas_call(
        _inproj_kernel,
        grid=(bsz, seq // tm),
        in_specs=[tok(d),
                  pl.BlockSpec((1, 6, d), lambda b, i: (row(b), 0, 0)),
                  full(g1), full(w_in_p), full(w_gate), full(b_gate)],
        out_specs=[tok(w) for w in outs],
        out_shape=[jax.ShapeDtypeStruct((bsz, seq, w), F32) for w in outs],
        compiler_params=_cp(("parallel", "parallel")),
        name="inproj",
    )(x, mod, g1, w_in_p, w_gate, b_gate)


def _gla_direction(q, k, v, la, state, reverse):
    blk = q.shape[0]
    g = GLA_GROUP
    nchunk = g // GLA_CHUNK
    ti = lax.broadcasted_iota(jnp.int32, (g, g), 0)
    tj = lax.broadcasted_iota(jnp.int32, (g, g), 1)
    same = (ti // GLA_CHUNK) == (tj // GLA_CHUNK)
    cmask = same & ((tj >= ti) if reverse else (tj <= ti))
    tri = jnp.where(cmask, 1.0, 0.0).astype(BF16)
    qk_head = lax.broadcasted_iota(jnp.int32, (g, GLA_QK_W), 1) // GLA_DK
    v_head = lax.broadcasted_iota(jnp.int32, (g, GROUP_W), 1) // GLA_DV
    bd = (lax.broadcasted_iota(jnp.int32, (GLA_QK_W, GROUP_W), 0) // GLA_DK
          == lax.broadcasted_iota(jnp.int32, (GLA_QK_W, GROUP_W), 1) // GLA_DV)
    tok_chunk = lax.broadcasted_iota(jnp.int32, (GLA_QK_W, g), 1) // GLA_CHUNK
    scale = GLA_DK ** -0.5
    outs = [None] * (blk // g)
    groups = range(blk // g)
    for gi in (reversed(groups) if reverse else groups):
        sl = slice(gi * g, (gi + 1) * g)
        qg, kg, vg, lg = q[sl], k[sl], v[sl], la[sl]
        l0 = lg.astype(BF16)
        r0 = lg - l0.astype(F32)
        l1 = r0.astype(BF16)
        l2 = (r0 - l1.astype(F32)).astype(BF16)
        b = (jnp.dot(tri, l0, preferred_element_type=F32)
             + jnp.dot(tri, l1, preferred_element_type=F32)
             + jnp.dot(tri, l2, preferred_element_type=F32))
        last = 0 if reverse else GLA_CHUNK - 1
        blast = jnp.concatenate(
            [jnp.broadcast_to(b[c * GLA_CHUNK + last:c * GLA_CHUNK + last + 1], (GLA_CHUNK, GLA_QK_W))
             for c in range(nchunk)], axis=0)
        qd = (qg * scale) * jnp.exp(b)
        ki = (kg * jnp.exp(-b)).astype(BF16)
        kw_t = (kg * jnp.exp(blast - b)).T
        b_t = b.T
        vb = vg.astype(BF16)
        o = jnp.zeros((g, GROUP_W), F32)
        for h in range(GLA_HEADS):
            qh = jnp.where(qk_head == h, qd, 0.0).astype(BF16)
            s = lax.dot_general(qh, ki, (((1,), (1,)), ((), ())), preferred_element_type=F32)
            s = jnp.where(cmask, s, 0.0).astype(BF16)
            oh = jnp.dot(s, vb, preferred_element_type=F32)
            o = o + jnp.where(v_head == h, oh, 0.0)
        qdb = qd.astype(BF16)
        inter = [None] * nchunk
        chunks = range(nchunk)
        for c in (reversed(chunks) if reverse else chunks):
            rows = slice(c * GLA_CHUNK, (c + 1) * GLA_CHUNK)
            inter[c] = jnp.dot(qdb[rows], state.astype(BF16), preferred_element_type=F32)
            col = c * GLA_CHUNK + last
            dec = jnp.exp(b_t[:, col:col + 1])
            kv = jnp.dot(jnp.where(tok_chunk == c, kw_t, 0.0).astype(BF16), vb,
                         preferred_element_type=F32)
            state = dec * state + jnp.where(bd, kv, 0.0)
        outs[gi] = o + jnp.concatenate(inter, axis=0)
    return jnp.concatenate(outs, axis=0) if len(outs) > 1 else outs[0], state


def _gla_kernel(qk_f, v_f, la_f, qk_b, v_b, la_b, s0_ref, of_ref, ob_ref, sfin_ref, st_ref):
    i = pl.program_id(1)

    @pl.when(i == 0)
    def _():
        st_ref[...] = s0_ref[0]

    qk = qk_f[0]
    o, s = _gla_direction(qk[:, 0:128], qk[:, 128:256], v_f[0], la_f[0], st_ref[0], False)
    of_ref[0] = o
    st_ref[0] = s
    qk = qk_b[0]
    o, s = _gla_direction(qk[:, 0:128], qk[:, 128:256], v_b[0], la_b[0], st_ref[1], True)
    ob_ref[0] = o
    st_ref[1] = s

    @pl.when(i == pl.num_programs(1) - 1)
    def _():
        sfin_ref[0] = st_ref[...]


def _gla(qkvg, la, s0, blk):
    bsz, seq, _ = qkvg.shape
    nb = seq // blk
    fwd = lambda j: pl.BlockSpec((1, blk, 256), lambda b, i: (b, i, j))
    bwd = lambda j: pl.BlockSpec((1, blk, 256), lambda b, i: (b, nb - 1 - i, j))
    la_f = pl.BlockSpec((1, blk, 128), lambda b, i: (b, i, 0))
    la_b = pl.BlockSpec((1, blk, 128), lambda b, i: (b, nb - 1 - i, 1))
    st = pl.BlockSpec((1, 2, GLA_QK_W, GROUP_W), lambda b, i: (b, 0, 0, 0))
    return pl.pallas_call(
        _gla_kernel,
        grid=(bsz, nb),
        in_specs=[fwd(0), fwd(1), la_f, bwd(0), bwd(1), la_b, st],
        out_specs=[pl.BlockSpec((1, blk, GROUP_W), lambda b, i: (b, i, 0)),
                   pl.BlockSpec((1, blk, GROUP_W), lambda b, i: (b, nb - 1 - i, 0)),
                   st],
        out_shape=[jax.ShapeDtypeStruct((bsz, seq, GROUP_W), F32),
                   jax.ShapeDtypeStruct((bsz, seq, GROUP_W), F32),
                   jax.ShapeDtypeStruct((bsz, 2, GLA_QK_W, GROUP_W), F32)],
        scratch_shapes=[pltpu.VMEM((2, GLA_QK_W, GROUP_W), F32)],
        compiler_params=_cp(("parallel", "arbitrary")),
        name="gla",
    )(qkvg, qkvg, la, qkvg, qkvg, la, s0)


def _row_shift(x, s, col, row_len):
    if s == 0:
        return x
    n = x.shape[0]
    y = pltpu.roll(x, (-s) % n, 0)
    valid = (col < row_len - s) if s > 0 else (col >= -s)
    return jnp.where(valid, y, 0.0)


def _row_conv(x, w, col, row_len):
    k = w.shape[0]
    acc = None
    for j in range(k):
        term = _row_shift(x, j - k // 2, col, row_len) * w[j:j + 1]
        acc = term if acc is None else acc + term
    return acc


def _col_conv(pad_ref, w, t0, n):
    k = w.shape[0]
    acc = None
    for j in range(k):
        start = pl.multiple_of(t0 + j * GRID_W, GRID_W)
        term = pad_ref[pl.ds(start, n), :] * w[j:j + 1]
        acc = term if acc is None else acc + term
    return acc


def _pool_diffs(x, col, row_len):
    colh = col[:, 0:LANE]
    colf = colh.astype(F32)
    first = lax.broadcasted_iota(jnp.int32, colh.shape, 1) < POOL_GW

    def count(win):
        return jnp.minimum(colf + win // 2, float(row_len)) - jnp.maximum(colf - win // 2, 0.0)

    means = []
    for hi in range(len(POOL_WINDOWS) // 2):
        wa, wb = POOL_WINDOWS[2 * hi], POOL_WINDOWS[2 * hi + 1]
        xh = x[:, hi * LANE:(hi + 1) * LANE]
        acc_a = acc_b = None
        for s in range(-(wb // 2), wb // 2):
            t = _row_shift(xh, s, colh, row_len)
            acc_b = t if acc_b is None else acc_b + t
            if -(wa // 2) <= s < wa // 2:
                acc_a = t if acc_a is None else acc_a + t
        means.append(jnp.where(first, acc_a / count(wa), acc_b / count(wb)))
    return jnp.concatenate(means, axis=-1) - x


def _local_tail(u, scb, conv_sc, pool, col, row_len, cb, lg, lb, pw, ps, out_ref):
    u = u + cb
    uc = u - jnp.mean(u, axis=-1, keepdims=True)
    ln = uc * lax.rsqrt(jnp.mean(uc * uc, axis=-1, keepdims=True) + NORM_EPS) * lg + lb
    out_ref[0, :, 0:256] = _silu(ln).astype(BF16)
    out_ref[0, :, 256:512] = (scb * conv_sc).astype(BF16)
    out_ref[0, :, 512:768] = (_dot(_pool_diffs(pool, col, row_len), pw) * ps).astype(BF16)


def _local_axial_kernel(uh_ref, uv_ref, scb_ref, sch_ref, scv_ref, pool_ref,
                        cw_ref, cb_ref, lg_ref, lb_ref, sw_ref, pw_ref, ps_ref,
                        out_ref, upad, spad):
    i = pl.program_id(1)
    blk = uh_ref.shape[1]
    seq = uv_ref.shape[1]
    pu = (CONF_WIDTH // 2) * GRID_W
    psc = (SC_WIDTH // 2) * GRID_W

    @pl.when(i == 0)
    def _():
        upad[0:pu, :] = jnp.zeros((pu, LANE), F32)
        upad[pu:pu + seq, :] = uv_ref[0]
        upad[pu + seq:pu + seq + pu, :] = jnp.zeros((pu, LANE), F32)
        spad[0:psc, :] = jnp.zeros((psc, LANE), F32)
        spad[psc:psc + seq, :] = scv_ref[0]
        spad[psc + seq:psc + seq + psc, :] = jnp.zeros((psc, LANE), F32)

    t0 = i * blk
    col = lax.broadcasted_iota(jnp.int32, (blk, LANE), 0) % GRID_W
    cw = cw_ref[...]
    sw = sw_ref[...]
    u = jnp.concatenate([_row_conv(uh_ref[0], cw[:, 0:128], col, GRID_W),
                         _col_conv(upad, cw[:, 128:256], t0, blk)], axis=-1)
    csc = jnp.concatenate([_row_conv(sch_ref[0], sw[:, 0:128], col, GRID_W),
                           _col_conv(spad, sw[:, 128:256], t0, blk)], axis=-1)
    col2 = jnp.concatenate([col, col], axis=-1)
    _local_tail(u, scb_ref[0], csc, pool_ref[0], col2, GRID_W, cb_ref[...], lg_ref[...],
                lb_ref[...], pw_ref[...], ps_ref[...], out_ref)


def _local_seq_kernel(u_ref, sc_ref, pool_ref, cw_ref, cb_ref, lg_ref, lb_ref, sw_ref,
                      pw_ref, ps_ref, out_ref):
    seq = u_ref.shape[1]
    col = lax.broadcasted_iota(jnp.int32, (seq, GROUP_W), 0)
    sc = sc_ref[0]
    u = _row_conv(u_ref[0], cw_ref[...], col, seq)
    csc = _row_conv(sc[:, 256:512], sw_ref[...], col, seq)
    _local_tail(u, sc[:, 0:256], csc, pool_ref[0], col, seq, cb_ref[...], lg_ref[...],
                lb_ref[...], pw_ref[...], ps_ref[...], out_ref)


def _local_params(p):
    return (p["conf_w"], p["conf_b"], p["conf_g"], p["conf_beta"], p["sc_w"], p["pool_bd"],
            p["pool_scale"])


def _local_axial(u, sc, pool, p, blk):
    bsz, seq, _ = u.shape
    cw, cb, lg, lb, sw, pw, ps = _local_params(p)
    blkspec = lambda w, j: pl.BlockSpec((1, blk, w), lambda b, i: (b, i, j))
    seqspec = lambda j: pl.BlockSpec((1, seq, LANE), lambda b, i: (b, 0, j))
    full = lambda a: pl.BlockSpec(a.shape, lambda b, i: (0,) * a.ndim)
    pu = (CONF_WIDTH // 2) * GRID_W
    psc = (SC_WIDTH // 2) * GRID_W
    return pl.pallas_call(
        _local_axial_kernel,
        grid=(bsz, seq // blk),
        in_specs=[blkspec(LANE, 0), seqspec(1), blkspec(256, 0), blkspec(LANE, 2), seqspec(3),
                  blkspec(256, 0), full(cw), full(cb), full(lg), full(lb), full(sw), full(pw),
                  full(ps)],
        out_specs=pl.BlockSpec((1, blk, 768), lambda b, i: (b, i, 0)),
        out_shape=jax.ShapeDtypeStruct((bsz, seq, 768), BF16),
        scratch_shapes=[pltpu.VMEM((seq + 2 * pu, LANE), F32),
                        pltpu.VMEM((seq + 2 * psc, LANE), F32)],
        compiler_params=_cp(("parallel", "arbitrary")),
        name="local_axial",
    )(u, u, sc, sc, sc, pool, cw, cb, lg, lb, sw, pw, ps)


def _local_seq(u, sc, pool, p):
    bsz, seq, _ = u.shape
    cw, cb, lg, lb, sw, pw, ps = _local_params(p)
    tok = lambda w: pl.BlockSpec((1, seq, w), lambda b: (b, 0, 0))
    full = lambda a: pl.BlockSpec(a.shape, lambda b: (0,) * a.ndim)
    return pl.pallas_call(
        _local_seq_kernel,
        grid=(bsz,),
        in_specs=[tok(256), tok(512), tok(256), full(cw), full(cb), full(lg), full(lb),
                  full(sw), full(pw), full(ps)],
        out_specs=tok(768),
        out_shape=jax.ShapeDtypeStruct((bsz, seq, 768), BF16),
        compiler_params=_cp(("parallel",)),
        name="local_seq",
    )(u, sc, pool, cw, cb, lg, lb, sw, pw, ps)


def _route(logits):
    lane = lax.broadcasted_iota(jnp.int32, logits.shape, 1)
    neg = -jnp.inf
    big = ROUTE_W

    def first_max(vals):
        mx = jnp.max(vals, axis=-1, keepdims=True)
        idx = jnp.min(jnp.where(vals == mx, lane, big), axis=-1, keepdims=True)
        return mx, idx

    lg = jnp.where(lane < N_GROUPS, logits, neg)
    gmx, grp = first_max(lg)
    p_grp = 1.0 / jnp.sum(jnp.exp(lg - gmx), axis=-1, keepdims=True)
    lo = N_GROUPS + grp * EXPERTS_PER_GROUP
    le = jnp.where((lane >= lo) & (lane < lo + EXPERTS_PER_GROUP), logits, neg)
    v1, i1 = first_max(le)
    v2, i2 = first_max(jnp.where(lane == i1, neg, le))
    e2 = jnp.exp(v2 - v1)
    w1 = p_grp / (1.0 + e2)
    w2 = p_grp * e2 / (1.0 + e2)
    rec = jnp.where(lane == 0, (i1 - N_GROUPS).astype(F32), 0.0)
    rec = jnp.where(lane == 1, (i2 - N_GROUPS).astype(F32), rec)
    rec = jnp.where(lane == 2, w1, rec)
    return jnp.where(lane == 3, w2, rec)


def _outproj_kernel(*refs, n_real, aliased):
    if aliased:
        refs = refs[1:]
    m_ref = refs[13]
    t = pl.program_id(0)

    @pl.when(t < n_real)
    def _():
        _outproj_body(*refs)

    @pl.when(t >= n_real)
    def _():
        m_ref[...] = jnp.zeros_like(m_ref)


def _outproj_body(of_ref, ob_ref, g_ref, loc_ref, x_ref, mod_ref, gout_ref, hsum_ref,
                  wo_ref, g2_ref, wr_ref, br_ref, xo_ref, m_ref, route_ref):
    mod = mod_ref[0]
    o = of_ref[0] + ob_ref[0]
    o2 = o * o
    hi = o2.astype(BF16)
    lo = (o2 - hi.astype(F32)).astype(BF16)
    ms = (jnp.dot(hi, hsum_ref[...], preferred_element_type=F32)
          + jnp.dot(lo, hsum_ref[...], preferred_element_type=F32)) / GLA_DV
    y_gla = o * lax.rsqrt(ms + NORM_EPS) * gout_ref[...] * _silu(g_ref[0])
    wo = wo_ref[...]
    proj = _dot(y_gla, wo[0:256]) + _dot(loc_ref[0], wo[256:1024])
    x = x_ref[0] + mod[2:3] * proj
    xo_ref[0] = x
    m = _rms(x, g2_ref[...]) * (1.0 + mod[4:5]) + mod[3:4]
    m_ref[...] = m
    route_ref[0] = _route(_dot(m, wr_ref[...]) + br_ref[...])


def _outproj(o_f, o_b, qkvg, y_loc, x, mod, mod_row, p, tm, m_rows, m_off, m_buf=None):
    bsz, seq, d = x.shape
    nt = seq // tm
    n_real = bsz * nt
    assert m_off % tm == 0 and m_rows % tm == 0
    n_steps = n_real if m_buf is not None else m_rows // tm
    assert m_buf is not None or m_off == 0
    tile = lambda t: jnp.minimum(t, n_real - 1)
    row = (lambda t: tile(t) // nt) if mod_row is None else (lambda t: mod_row)
    tok = lambda w, j=0: pl.BlockSpec((1, tm, w), lambda t: (tile(t) // nt, tile(t) % nt, j))
    full = lambda a: pl.BlockSpec(a.shape, lambda t: (0,) * a.ndim)
    consts = (p["g_out"], p["head_sum"], p["w_out"], p["g_norm2"], p["w_router"], p["b_router"])
    in_specs = [tok(256), tok(256), tok(256, 2), tok(768), tok(d),
                pl.BlockSpec((1, 6, d), lambda t: (row(t), 0, 0))] + [full(a) for a in consts]
    args = (o_f, o_b, qkvg, y_loc, x, mod, *consts)
    if m_buf is not None:
        in_specs = [pl.BlockSpec(memory_space=pl.ANY)] + in_specs
        args = (m_buf,) + args
    return pl.pallas_call(
        functools.partial(_outproj_kernel, n_real=n_real, aliased=m_buf is not None),
        grid=(n_steps,),
        in_specs=in_specs,
        out_specs=[tok(d), pl.BlockSpec((tm, d), lambda t: (m_off // tm + t, 0)), tok(ROUTE_W)],
        out_shape=[jax.ShapeDtypeStruct((bsz, seq, d), F32),
                   jax.ShapeDtypeStruct((m_rows, d), F32),
                   jax.ShapeDtypeStruct((bsz, seq, ROUTE_W), F32)],
        input_output_aliases={} if m_buf is None else {0: 1},
        compiler_params=_cp(("arbitrary",)),
        name="outproj",
    )(*args)


RANK_ROWS = 512


def _rank_kernel(route_ref, rank_ref, cnt_ref, carry):
    i = pl.program_id(0)

    @pl.when(i == 0)
    def _():
        carry[...] = jnp.zeros_like(carry)

    rec = route_ref[...]
    tm = rec.shape[0]
    lane = lax.broadcasted_iota(jnp.int32, rec.shape, 1).astype(F32)
    oh0 = jnp.where(lane == rec[:, 0:1], 1.0, 0.0)
    oh1 = jnp.where(lane == rec[:, 1:2], 1.0, 0.0)
    ti = lax.broadcasted_iota(jnp.int32, (tm, tm), 0)
    tj = lax.broadcasted_iota(jnp.int32, (tm, tm), 1)
    before = jnp.where(tj < ti, 1.0, 0.0).astype(BF16)
    tot0 = jnp.sum(oh0, axis=0, keepdims=True)
    tot1 = jnp.sum(oh1, axis=0, keepdims=True)
    base = carry[...]
    pre0 = jnp.dot(before, oh0.astype(BF16), preferred_element_type=F32) + base
    pre1 = jnp.dot(before, oh1.astype(BF16), preferred_element_type=F32) + (base + tot0)
    r0 = jnp.sum(oh0 * pre0, axis=-1, keepdims=True)
    r1 = jnp.sum(oh1 * pre1, axis=-1, keepdims=True)
    lane_i = lax.broadcasted_iota(jnp.int32, rec.shape, 1)
    rank_ref[...] = jnp.where(lane_i == 0, r0, jnp.where(lane_i == 1, r1, 0.0)).astype(jnp.int32)
    carry[...] = base + tot0 + tot1
    cnt_ref[...] = carry[...]


def _rank(route):
    n_tok = route.shape[0]
    tm = RANK_ROWS if n_tok % RANK_ROWS == 0 else 256
    return pl.pallas_call(
        _rank_kernel,
        grid=(n_tok // tm,),
        in_specs=[pl.BlockSpec((tm, ROUTE_W), lambda i: (i, 0))],
        out_specs=[pl.BlockSpec((tm, ROUTE_W), lambda i: (i, 0)),
                   pl.BlockSpec((1, ROUTE_W), lambda i: (0, 0))],
        out_shape=[jax.ShapeDtypeStruct((n_tok, ROUTE_W), jnp.int32),
                   jax.ShapeDtypeStruct((1, ROUTE_W), F32)],
        scratch_shapes=[pltpu.VMEM((1, ROUTE_W), F32)],
        compiler_params=_cp(("arbitrary",)),
        name="rank",
    )(route)


def _plan(route):
    n_tok = route.shape[0]
    rank, cnt = _rank(route)
    counts = cnt[0, 0:N_EXPERTS].astype(jnp.int32)
    padded = (counts + MOE_BLOCK - 1) // MOE_BLOCK * MOE_BLOCK
    pad_end = jnp.cumsum(padded)
    pad_start = pad_end - padded
    eid = route[:, 0:2].astype(jnp.int32)
    onehot = eid[:, :, None] == jnp.arange(N_EXPERTS, dtype=jnp.int32)[None, None, :]
    dest = (jnp.sum(jnp.where(onehot, pad_start[None, None, :], 0), axis=-1) + rank[:, 0:2]).reshape(-1)
    n_blocks = 2 * n_tok // MOE_BLOCK + N_EXPERTS
    blk_start = jnp.arange(n_blocks, dtype=jnp.int32) * MOE_BLOCK
    nused = pad_end[-1] // MOE_BLOCK
    blk = jnp.minimum(jnp.arange(n_blocks, dtype=jnp.int32), nused - 1)
    blk_expert = jnp.sum((blk[:, None] * MOE_BLOCK >= pad_end[None, :]).astype(jnp.int32), axis=-1)
    fill_start = pad_start + counts
    fill_n = padded - counts
    return dict(dest=dest.astype(jnp.int32), blk=blk.astype(jnp.int32),
                blk_expert=jnp.minimum(blk_expert, N_EXPERTS - 1).astype(jnp.int32),
                nused=nused.astype(jnp.int32).reshape(1), fill_start=fill_start.astype(jnp.int32),
                fill_n=fill_n.astype(jnp.int32), n_blocks=n_blocks)


DISPATCH_ROWS = 256


def _dispatch_kernel(dest_ref, fstart_ref, fn_ref, nused_ref, *refs, tr, n_x, n_c):
    if n_c:
        mx_hbm, mc_hbm, xs_hbm, buf, zbuf, in_sem, out_sem, fsem = refs
    else:
        mx_hbm, xs_hbm, buf, zbuf, in_sem, out_sem, fsem = refs
        mc_hbm = None
    i = pl.program_id(0)
    nsteps = n_x + n_c
    nslot = 3

    def start_load(step):
        slot = step % nslot

        @pl.when(step < n_x)
        def _():
            rows = pl.ds(pl.multiple_of(step * tr, tr), tr)
            pltpu.make_async_copy(mx_hbm.at[rows], buf.at[slot], in_sem.at[slot]).start()

        if n_c:
            @pl.when(step >= n_x)
            def _():
                rows = pl.ds(pl.multiple_of((step - n_x) * tr, tr), tr)
                pltpu.make_async_copy(mc_hbm.at[rows], buf.at[slot], in_sem.at[slot]).start()

    def wait_load(step):
        slot = step % nslot
        pltpu.make_async_copy(mx_hbm.at[pl.ds(0, tr)], buf.at[slot], in_sem.at[slot]).wait()

    def row_copy(step, r, k):
        slot = step % nslot
        dst = dest_ref[2 * (step * tr + r) + k]
        return pltpu.make_async_copy(buf.at[slot, pl.ds(r, 1)], xs_hbm.at[pl.ds(dst, 1)],
                                     out_sem.at[slot])

    def wait_scatter(step):
        slot = step % nslot
        for _ in range(2):
            pltpu.make_async_copy(buf.at[slot], xs_hbm.at[pl.ds(0, tr)], out_sem.at[slot]).wait()

    @pl.when(i == 0)
    def _():
        zbuf[...] = jnp.zeros_like(zbuf)
        start_load(i)

    @pl.when(i >= 2)
    def _():
        wait_scatter(i - 2)

    @pl.when(i + 1 < nsteps)
    def _():
        start_load(i + 1)

    wait_load(i)

    for r in range(tr):
        row_copy(i, r, 0).start()
        row_copy(i, r, 1).start()

    @pl.when(i == nsteps - 1)
    def _():
        if nsteps >= 2:
            wait_scatter(i - 1)
        wait_scatter(i)

        def fill_copy(e, r):
            return pltpu.make_async_copy(zbuf.at[pl.ds(0, 1)],
                                         xs_hbm.at[pl.ds(fstart_ref[e] + r, 1)], fsem.at[0])

        def per_expert(e, carry):
            lax.fori_loop(0, fn_ref[e], lambda r, c: (fill_copy(e, r).start(), c)[1], 0)
            return carry

        def per_expert_wait(e, carry):
            lax.fori_loop(0, fn_ref[e], lambda r, c: (fill_copy(e, r).wait(), c)[1], 0)
            return carry

        lax.fori_loop(0, N_EXPERTS, per_expert, 0)
        lax.fori_loop(0, N_EXPERTS, per_expert_wait, 0)

        def tail_copy(b):
            start = pl.multiple_of(b * MOE_BLOCK, MOE_BLOCK)
            return pltpu.make_async_copy(zbuf, xs_hbm.at[pl.ds(start, MOE_BLOCK)], fsem.at[0])

        nblk = xs_hbm.shape[0] // MOE_BLOCK
        lax.fori_loop(nused_ref[0], nblk, lambda b, c: (tail_copy(b).start(), c)[1], 0)
        lax.fori_loop(nused_ref[0], nblk, lambda b, c: (tail_copy(b).wait(), c)[1], 0)


def _dispatch(m_x, m_c, plan):
    n_lat, d = m_x.shape
    n_ctx = 0 if m_c is None else m_c.shape[0]
    tr = DISPATCH_ROWS
    while n_lat % tr or n_ctx % tr:
        tr //= 2
    slots = plan["n_blocks"] * MOE_BLOCK
    n_x, n_c = n_lat // tr, n_ctx // tr
    srcs = (m_x,) if m_c is None else (m_x, m_c)
    grid_spec = pltpu.PrefetchScalarGridSpec(
        num_scalar_prefetch=4,
        grid=(n_x + n_c,),
        in_specs=[pl.BlockSpec(memory_space=pl.ANY)] * len(srcs),
        out_specs=pl.BlockSpec(memory_space=pl.ANY),
        scratch_shapes=[pltpu.VMEM((3, tr, d), F32), pltpu.VMEM((MOE_BLOCK, d), F32),
                        pltpu.SemaphoreType.DMA((3,)), pltpu.SemaphoreType.DMA((3,)),
                        pltpu.SemaphoreType.DMA((1,))],
    )
    return pl.pallas_call(
        functools.partial(_dispatch_kernel, tr=tr, n_x=n_x, n_c=n_c),
        grid_spec=grid_spec,
        out_shape=jax.ShapeDtypeStruct((slots, d), F32),
        compiler_params=_cp(("arbitrary",)),
        name="dispatch",
    )(plan["dest"], plan["fill_start"], plan["fill_n"], plan["nused"], *srcs)


def _expert_kernel(blk_ref, be_ref, nused_ref, xs_ref, w1_ref, w2_ref, ys_ref, w1b, w2b):
    i = pl.program_id(0)
    used = i < nused_ref[0]
    fresh = (i == 0) | (be_ref[i] != be_ref[jnp.maximum(i - 1, 0)])

    @pl.when(used & fresh)
    def _():
        w1b[...] = w1_ref[0, 0].astype(BF16)
        w2b[...] = w2_ref[0, 0].astype(BF16)

    @pl.when(used)
    def _():
        h = _dot(xs_ref[...], w1b[...])
        act = _silu(h[:, :D_EXPERT]) * h[:, D_EXPERT:]
        ys_ref[...] = _dot(act, w2b[...])

    @pl.when(jnp.logical_not(used))
    def _():
        ys_ref[...] = jnp.zeros_like(ys_ref)


def _experts(xs, plan, w1, w2, layer):
    slots, d = xs.shape
    grid_spec = pltpu.PrefetchScalarGridSpec(
        num_scalar_prefetch=3,
        grid=(plan["n_blocks"],),
        in_specs=[pl.BlockSpec((MOE_BLOCK, d), lambda i, blk, be, nu: (blk[i], 0)),
                  pl.BlockSpec((1, 1, d, 2 * D_EXPERT), lambda i, blk, be, nu: (layer, be[i], 0, 0)),
                  pl.BlockSpec((1, 1, D_EXPERT, d), lambda i, blk, be, nu: (layer, be[i], 0, 0))],
        out_specs=pl.BlockSpec((MOE_BLOCK, d), lambda i, blk, be, nu: (i, 0)),
        scratch_shapes=[pltpu.VMEM((d, 2 * D_EXPERT), BF16), pltpu.VMEM((D_EXPERT, d), BF16)],
    )
    return pl.pallas_call(
        _expert_kernel,
        grid_spec=grid_spec,
        out_shape=jax.ShapeDtypeStruct((slots, d), F32),
        compiler_params=_cp(("arbitrary",)),
        name="experts",
    )(plan["blk"], plan["blk_expert"], plan["nused"], xs, w1, w2)


INV_UNROLL = 8
INV_CHUNK = 2048


def _invmap_kernel(dest_ref, fstart_ref, fn_ref, nused_ref, tok_ref, *, chunk):
    i = pl.program_id(0)

    @pl.when(i == 0)
    def _():
        def per_expert(e, carry):
            def one(r, c):
                tok_ref[fstart_ref[e] + r] = 0
                return c
            lax.fori_loop(0, fn_ref[e], one, 0)
            return carry

        lax.fori_loop(0, N_EXPERTS, per_expert, 0)

        def one_tail(s, c):
            tok_ref[s] = 0
            return c

        lax.fori_loop(nused_ref[0] * MOE_BLOCK, tok_ref.shape[0], one_tail, 0)

    def place(j, carry):
        for u in range(INV_UNROLL):
            a = i * chunk + j * INV_UNROLL + u
            tok_ref[dest_ref[a]] = a // 2
        return carry

    lax.fori_loop(0, chunk // INV_UNROLL, place, 0)


def _invmap(plan):
    dest = plan["dest"]
    n_asg = dest.shape[0]
    chunk = INV_CHUNK
    while n_asg % chunk:
        chunk //= 2
    assert chunk % INV_UNROLL == 0
    smem = pl.BlockSpec(memory_space=pltpu.SMEM)
    return pl.pallas_call(
        functools.partial(_invmap_kernel, chunk=chunk),
        grid=(n_asg // chunk,),
        in_specs=[smem, smem, smem, smem],
        out_specs=smem,
        out_shape=jax.ShapeDtypeStruct((plan["n_blocks"] * MOE_BLOCK,), jnp.int32),
        compiler_params=_cp(("arbitrary",)),
        name="invmap",
    )(dest, plan["fill_start"], plan["fill_n"], plan["nused"])


def _expert_gather_kernel(be_ref, nused_ref, slot_tok_ref, m_hbm, w1_ref, w2_ref, ys_ref,
                          xbuf0, xbuf1, w1b, w2b, sem):
    i = pl.program_id(0)
    nused = nused_ref[0]
    nrow = MOE_BLOCK
    bufs = (xbuf0, xbuf1)

    def row_copy(blk, r, par):
        tok = slot_tok_ref[blk * nrow + r]
        return pltpu.make_async_copy(m_hbm.at[pl.ds(tok, 1)], bufs[par].at[pl.ds(r, 1)],
                                     sem.at[par])

    def wait_block(par):
        pltpu.make_async_copy(m_hbm.at[pl.ds(0, nrow)], bufs[par], sem.at[par]).wait()

    @pl.when(i == 0)
    def _():
        for r in range(nrow):
            row_copy(i, r, 0).start()

    def used_step(par):
        @pl.when((i == 0) | (be_ref[i] != be_ref[jnp.maximum(i - 1, 0)]))
        def _():
            w1b[...] = w1_ref[0, 0].astype(BF16)
            w2b[...] = w2_ref[0, 0].astype(BF16)

        wait_block(par)
        half = nrow // 2
        for r in range(half):
            row_copy(i + 1, r, 1 - par).start()
        h = _dot(bufs[par][...], w1b[...])
        for r in range(half, nrow):
            row_copy(i + 1, r, 1 - par).start()
        act = _silu(h[:, :D_EXPERT]) * h[:, D_EXPERT:]
        ys_ref[...] = _dot(act, w2b[...])

    for par in range(2):
        pl.when((i < nused) & (i % 2 == par))(functools.partial(used_step, par))

    @pl.when(i >= nused)
    def _():
        for par in range(2):
            pl.when((i == nused) & (i % 2 == par))(functools.partial(wait_block, par))
        ys_ref[...] = jnp.zeros_like(ys_ref)


def _experts_gather(m_all, slot_tok, plan, w1, w2, layer):
    d = m_all.shape[1]
    n_blocks = plan["n_blocks"]
    grid_spec = pltpu.PrefetchScalarGridSpec(
        num_scalar_prefetch=3,
        grid=(n_blocks,),
        in_specs=[pl.BlockSpec(memory_space=pl.ANY),
                  pl.BlockSpec((1, 1, d, 2 * D_EXPERT), lambda i, be, nu, st: (layer, be[i], 0, 0)),
                  pl.BlockSpec((1, 1, D_EXPERT, d), lambda i, be, nu, st: (layer, be[i], 0, 0))],
        out_specs=pl.BlockSpec((MOE_BLOCK, d), lambda i, be, nu, st: (i, 0)),
        scratch_shapes=[pltpu.VMEM((MOE_BLOCK, d), F32), pltpu.VMEM((MOE_BLOCK, d), F32),
                        pltpu.VMEM((d, 2 * D_EXPERT), BF16), pltpu.VMEM((D_EXPERT, d), BF16),
                        pltpu.SemaphoreType.DMA((2,))],
    )
    return pl.pallas_call(
        _expert_gather_kernel,
        grid_spec=grid_spec,
        out_shape=jax.ShapeDtypeStruct((n_blocks * MOE_BLOCK, d), F32),
        compiler_params=_cp(("arbitrary",)),
        name="experts",
    )(plan["blk_expert"], plan["nused"], slot_tok, m_all, w1, w2)


COMBINE_ROWS = 256


def _combine_kernel(dest_ref, x_ref, route_ref, mod_ref, gf_ref, ys_hbm, o_ref, y0buf, y1buf, sem,
                    *, final, tok_off):
    i = pl.program_id(0)
    nsteps = pl.num_programs(0)
    tc = COMBINE_ROWS

    def row_copy(step, r, k, slot):
        a = 2 * (tok_off + step * tc + r) + k
        buf = y0buf if k == 0 else y1buf
        return pltpu.make_async_copy(ys_hbm.at[pl.ds(dest_ref[a], 1)], buf.at[slot, pl.ds(r, 1)],
                                     sem.at[slot])

    def issue_step(step):
        for r in range(tc):
            row_copy(step, r, 0, step % 2).start()
            row_copy(step, r, 1, step % 2).start()

    @pl.when(i == 0)
    def _():
        issue_step(0)

    @pl.when(i + 1 < nsteps)
    def _():
        issue_step(i + 1)

    slot = i % 2
    pltpu.make_async_copy(ys_hbm.at[pl.ds(0, tc)], y0buf.at[slot], sem.at[slot]).wait()
    pltpu.make_async_copy(ys_hbm.at[pl.ds(0, tc)], y1buf.at[slot], sem.at[slot]).wait()
    rec = route_ref[...]
    y = rec[:, 2:3] * y0buf[slot] + rec[:, 3:4] * y1buf[slot]
    x = x_ref[0] + mod_ref[0][5:6] * y
    o_ref[0] = _rms(x, gf_ref[...]) if final else x


def _combine(x, ys, route, plan, tok_off, mod, mod_row, g_final, final):
    bsz, seq, d = x.shape
    tc = COMBINE_ROWS
    nt = seq // tc
    assert seq % tc == 0 and tok_off % tc == 0
    row = (lambda i: i // nt) if mod_row is None else (lambda i: mod_row)
    grid_spec = pltpu.PrefetchScalarGridSpec(
        num_scalar_prefetch=1,
        grid=(bsz * nt,),
        in_specs=[pl.BlockSpec((1, tc, d), lambda i, de: (i // nt, i % nt, 0)),
                  pl.BlockSpec((tc, ROUTE_W), lambda i, de: (tok_off // tc + i, 0)),
                  pl.BlockSpec((1, 6, d), lambda i, de: (row(i), 0, 0)),
                  pl.BlockSpec((1, d), lambda i, de: (0, 0)),
                  pl.BlockSpec(memory_space=pl.ANY)],
        out_specs=pl.BlockSpec((1, tc, d), lambda i, de: (i // nt, i % nt, 0)),
        scratch_shapes=[pltpu.VMEM((2, tc, d), F32), pltpu.VMEM((2, tc, d), F32),
                        pltpu.SemaphoreType.DMA((2,))],
    )
    return pl.pallas_call(
        functools.partial(_combine_kernel, final=final, tok_off=tok_off),
        grid_spec=grid_spec,
        out_shape=jax.ShapeDtypeStruct((bsz, seq, d), F32),
        compiler_params=_cp(("arbitrary",)),
        name="combine",
    )(plan["dest"], x, route, mod, g_final, ys)


def _layer_params(l, w_in, gla_w_gate_f, gla_b_gate_f, gla_w_gate_b, gla_b_gate_b, gla_g_out,
                  conf_w_dw, conf_b_dw, conf_ln_g, conf_ln_b, sc_w_dw, pool_w, pool_scale, w_out,
                  router_w_group, router_b_group, router_w_expert, router_b_expert,
                  expert_w_in, expert_w_out, g_norm1, g_norm2):
    d = D_MODEL
    idx = np.cumsum((0,) + IN_SPLITS)
    q, k, v, g, zf, zb, conf, scb, scc, scx, pool = [w_in[l][:, idx[j]:idx[j + 1]] for j in range(11)]
    pad = jnp.zeros((d, N_IN_PAD - idx[-1]), F32)
    w_in_p = jnp.concatenate([q, k, v, g, conf, scb, scc, scx, pool, zf, zb, pad], axis=1).astype(BF16)
    r = GLA_GATE_RANK
    w_gate = jnp.zeros((LANE, 2 * GLA_QK_W), F32)
    w_gate = w_gate.at[0:r, 0:GLA_QK_W].set(gla_w_gate_f[l]).at[r:2 * r, GLA_QK_W:].set(gla_w_gate_b[l])
    b_gate = jnp.concatenate([gla_b_gate_f[l], gla_b_gate_b[l]])[None]
    pool_bd = jnp.zeros((GROUP_W, GROUP_W), F32)
    for gi in range(len(POOL_WINDOWS)):
        s = slice(gi * POOL_GW, (gi + 1) * POOL_GW)
        pool_bd = pool_bd.at[s, s].set(pool_w[l, gi])
    head = np.arange(GROUP_W) // GLA_DV
    head_sum = jnp.asarray(head[:, None] == head[None, :], BF16)
    w_router = jnp.zeros((d, ROUTE_W), F32)
    w_router = w_router.at[:, 0:N_GROUPS].set(router_w_group[l])
    w_router = w_router.at[:, N_GROUPS:N_GROUPS + N_EXPERTS].set(router_w_expert[l])
    b_router = jnp.zeros((1, ROUTE_W), F32)
    b_router = b_router.at[0, 0:N_GROUPS].set(router_b_group[l])
    b_router = b_router.at[0, N_GROUPS:N_GROUPS + N_EXPERTS].set(router_b_expert[l])
    return dict(
        w_in=w_in_p, w_gate=w_gate.astype(BF16), b_gate=b_gate,
        g_norm1=g_norm1[l][None], g_norm2=g_norm2[l][None],
        g_out=gla_g_out[l][None], head_sum=head_sum,
        conf_w=conf_w_dw[l], conf_b=conf_b_dw[l][None], conf_g=conf_ln_g[l][None],
        conf_beta=conf_ln_b[l][None], sc_w=sc_w_dw[l], pool_bd=pool_bd.astype(BF16),
        pool_scale=pool_scale[l][None], w_out=w_out[l].astype(BF16),
        w_router=w_router.astype(BF16), b_router=b_router)


def kernel(x, c, ctx, c_ctx, w_mod, b_mod, g_norm1, g_norm2, w_in, gla_w_gate_f, gla_b_gate_f,
           gla_w_gate_b, gla_b_gate_b, gla_g_out, conf_w_dw, conf_b_dw, conf_ln_g, conf_ln_b,
           sc_w_dw, pool_w, pool_scale, w_out, router_w_group, router_b_group, router_w_expert,
           router_b_expert, expert_w_in, expert_w_out, g_final):
    bsz, seq, d = x.shape
    ctx_len = ctx.shape[1]
    depth = w_mod.shape[0]
    n_lat = bsz * seq
    n_ctx = bsz * ctx_len
    assert d == D_MODEL and seq % 1024 == 0 and ctx_len % GLA_GROUP == 0 and bsz <= 7
    tm = 512
    tm_c = 256

    cv = jnp.zeros((8, d), F32).at[0:bsz].set(c).at[bsz].set(c_ctx)
    mods = _adaln(cv, w_mod, b_mod).reshape(depth, 8, 6, d)
    gf = g_final[None]
    zero_state = jnp.zeros((bsz, 2, GLA_QK_W, GROUP_W), F32)

    for l in range(depth):
        last = l == depth - 1
        p = _layer_params(l, w_in, gla_w_gate_f, gla_b_gate_f, gla_w_gate_b, gla_b_gate_b,
                          gla_g_out, conf_w_dw, conf_b_dw, conf_ln_g, conf_ln_b, sc_w_dw, pool_w,
                          pool_scale, w_out, router_w_group, router_b_group, router_w_expert,
                          router_b_expert, expert_w_in, expert_w_out, g_norm1, g_norm2)
        mod = mods[l]

        c_qkvg, c_la, c_u, c_sc, c_pool = _inproj(ctx, mod, bsz, p["g_norm1"], p["w_in"],
                                                  p["w_gate"], p["b_gate"], tm_c)
        c_of, c_ob, s_ctx = _gla(c_qkvg, c_la, zero_state, ctx_len)
        x_qkvg, x_la, x_u, x_sc, x_pool = _inproj(x, mod, None, p["g_norm1"], p["w_in"],
                                                  p["w_gate"], p["b_gate"], tm)
        x_of, x_ob, _ = _gla(x_qkvg, x_la, s_ctx, 512)
        x_loc = _local_axial(x_u, x_sc, x_pool, p, 1024)
        n_tok = n_lat if last else n_lat + n_ctx
        x, m_all, r_x = _outproj(x_of, x_ob, x_qkvg, x_loc, x, mod, None, p,
                                 tm if n_tok % tm == 0 else tm_c, n_tok, 0)

        if last:
            route = r_x.reshape(n_lat, ROUTE_W)
        else:
            c_loc = _local_seq(c_u, c_sc, c_pool, p)
            ctx, m_all, r_c = _outproj(c_of, c_ob, c_qkvg, c_loc, ctx, mod, bsz, p, tm_c, n_tok,
                                       n_lat, m_all)
            route = jnp.concatenate([r_x.reshape(n_lat, ROUTE_W), r_c.reshape(n_ctx, ROUTE_W)], axis=0)

        plan = _plan(route)
        slot_tok = _invmap(plan)
        ys = _experts_gather(m_all, slot_tok, plan, expert_w_in, expert_w_out, l)
        x = _combine(x, ys, route, plan, 0, mod, None, gf, last)
        if not last:
            ctx = _combine(ctx, ys, route, plan, n_lat, mod, bsz, gf, False)
    return x
```

```python
import functools

import numpy as np
import jax
import jax.numpy as jnp
from jax import lax
from jax.experimental import pallas as pl
from jax.experimental.pallas import tpu as pltpu

F32 = jnp.float32
BF16 = jnp.bfloat16

D_MODEL = 1024
GRID_W = 64
GROUP_W = 256
GLA_HEADS = 4
GLA_DV = 64
GLA_DK = 32
GLA_QK_W = 128
GLA_GATE_RANK = 16
GLA_GATE_NORM = 16.0
GLA_CHUNK = 64
GLA_GROUP = 256
CONF_WIDTH = 31
SC_WIDTH = 3
POOL_WINDOWS = (2, 4, 8, 16)
POOL_GW = 64
N_GROUPS = 4
EXPERTS_PER_GROUP = 8
N_EXPERTS = 32
D_EXPERT = 512
MOE_BLOCK = 512
NORM_EPS = 1e-6
IN_SPLITS = (128, 128, 256, 256, 16, 16, 512, 256, 256, 256, 256)
N_IN_PAD = 2432
LANE = 128
ROUTE_W = LANE

VMEM_LIMIT = 48 << 20


def _cp(sem, vmem=VMEM_LIMIT):
    return pltpu.CompilerParams(dimension_semantics=sem, vmem_limit_bytes=vmem)


def _sigmoid(x):
    return 1.0 / (1.0 + jnp.exp(-x))


def _silu(x):
    return x * _sigmoid(x)


def _log_sigmoid(x):
    return jnp.minimum(x, 0.0) - jnp.log1p(jnp.exp(-jnp.abs(x)))


def _dot(a, b):
    return jnp.dot(a.astype(BF16), b.astype(BF16), preferred_element_type=F32)


def _rms(x, g):
    return x * lax.rsqrt(jnp.mean(x * x, axis=-1, keepdims=True) + NORM_EPS) * g


def _adaln_kernel(cv_ref, w_ref, b_ref, o_ref):
    o_ref[0] = _dot(_silu(cv_ref[...]), w_ref[0]) + b_ref[0]


def _adaln(cv, w_mod, b_mod):
    depth, d, n = w_mod.shape
    tn = 1024
    return pl.pallas_call(
        _adaln_kernel,
        grid=(depth, n // tn),
        in_specs=[pl.BlockSpec((8, d), lambda l, j: (0, 0)),
                  pl.BlockSpec((1, d, tn), lambda l, j: (l, 0, j)),
                  pl.BlockSpec((1, 1, tn), lambda l, j: (l, 0, j))],
        out_specs=pl.BlockSpec((1, 8, tn), lambda l, j: (l, 0, j)),
        out_shape=jax.ShapeDtypeStruct((depth, 8, n), F32),
        compiler_params=_cp(("parallel", "parallel")),
        name="adaln",
    )(cv, w_mod, b_mod.reshape(depth, 1, n))


def _inproj_kernel(x_ref, mod_ref, g1_ref, w_ref, wg_ref, bg_ref,
                   qkvg_ref, la_ref, u_ref, sc_ref, pool_ref):
    x = x_ref[0]
    mod = mod_ref[0]
    m = _rms(x, g1_ref[...]) * (1.0 + mod[1:2]) + mod[0:1]
    h = _dot(m, w_ref[...])
    qkvg_ref[0] = h[:, 0:768]
    u_ref[0] = h[:, 768:1024] * _sigmoid(h[:, 1024:1280])
    sc_ref[0, :, 0:256] = h[:, 1280:1536]
    sc_ref[0, :, 256:512] = h[:, 1536:1792] * h[:, 1792:2048]
    pool_ref[0] = h[:, 2048:2304]
    z = _dot(h[:, 2304:2432], wg_ref[...]) + bg_ref[...]
    la_ref[0] = _log_sigmoid(z) / GLA_GATE_NORM


def _inproj(x, mod, mod_row, g1, w_in_p, w_gate, b_gate, tm):
    bsz, seq, d = x.shape
    row = (lambda b: b) if mod_row is None else (lambda b: mod_row)
    tok = lambda w: pl.BlockSpec((1, tm, w), lambda b, i: (b, i, 0))
    full = lambda a: pl.BlockSpec(a.shape, lambda b, i: (0,) * a.ndim)
    outs = (768, 256, 256, 512, 256)
    return pl.pallas_call(
        _inproj_kernel,
        grid=(bsz, seq // tm),
        in_specs=[tok(d),
                  pl.BlockSpec((1, 6, d), lambda b, i: (row(b), 0, 0)),
                  full(g1), full(w_in_p), full(w_gate), full(b_gate)],
        out_specs=[tok(w) for w in outs],
        out_shape=[jax.ShapeDtypeStruct((bsz, seq, w), F32) for w in outs],
        compiler_params=_cp(("parallel", "parallel")),
        name="inproj",
    )(x, mod, g1, w_in_p, w_gate, b_gate)


def _gla_direction(q, k, v, la, state, reverse):
    blk = q.shape[0]
    g = GLA_GROUP
    nchunk = g // GLA_CHUNK
    ti = lax.broadcasted_iota(jnp.int32, (g, g), 0)
    tj = lax.broadcasted_iota(jnp.int32, (g, g), 1)
    same = (ti // GLA_CHUNK) == (tj // GLA_CHUNK)
    cmask = same & ((tj >= ti) if reverse else (tj <= ti))
    tri = jnp.where(cmask, 1.0, 0.0).astype(BF16)
    qk_head = lax.broadcasted_iota(jnp.int32, (g, GLA_QK_W), 1) // GLA_DK
    v_head = lax.broadcasted_iota(jnp.int32, (g, GROUP_W), 1) // GLA_DV
    bd = (lax.broadcasted_iota(jnp.int32, (GLA_QK_W, GROUP_W), 0) // GLA_DK
          == lax.broadcasted_iota(jnp.int32, (GLA_QK_W, GROUP_W), 1) // GLA_DV)
    tok_chunk = lax.broadcasted_iota(jnp.int32, (GLA_QK_W, g), 1) // GLA_CHUNK
    scale = GLA_DK ** -0.5
    outs = [None] * (blk // g)
    groups = range(blk // g)
    for gi in (reversed(groups) if reverse else groups):
        sl = slice(gi * g, (gi + 1) * g)
        qg, kg, vg, lg = q[sl], k[sl], v[sl], la[sl]
        l0 = lg.astype(BF16)
        r0 = lg - l0.astype(F32)
        l1 = r0.astype(BF16)
        l2 = (r0 - l1.astype(F32)).astype(BF16)
        b3 = jnp.dot(tri, jnp.concatenate([l0, l1, l2], axis=1), preferred_element_type=F32)
        b = b3[:, 0:GLA_QK_W] + b3[:, GLA_QK_W:2 * GLA_QK_W] + b3[:, 2 * GLA_QK_W:3 * GLA_QK_W]
        last = 0 if reverse else GLA_CHUNK - 1
        blast = jnp.concatenate(
            [jnp.broadcast_to(b[c * GLA_CHUNK + last:c * GLA_CHUNK + last + 1], (GLA_CHUNK, GLA_QK_W))
             for c in range(nchunk)], axis=0)
        qd = (qg * scale) * jnp.exp(b)
        ki = (kg * jnp.exp(-b)).astype(BF16)
        kw_t = (kg * jnp.exp(blast - b)).T
        b_t = b.T
        vb = vg.astype(BF16)
        qs = jnp.concatenate([jnp.where(qk_head == h, qd, 0.0) for h in range(GLA_HEADS)],
                             axis=0).astype(BF16)
        s = lax.dot_general(qs, ki, (((1,), (1,)), ((), ())), preferred_element_type=F32)
        s = jnp.where(jnp.concatenate([cmask] * GLA_HEADS, axis=0), s, 0.0).astype(BF16)
        oh = jnp.dot(s, vb, preferred_element_type=F32)
        o = jnp.zeros((g, GROUP_W), F32)
        for h in range(GLA_HEADS):
            o = o + jnp.where(v_head == h, oh[h * g:(h + 1) * g], 0.0)
        kws = jnp.concatenate([jnp.where(tok_chunk == c, kw_t, 0.0) for c in range(nchunk)],
                              axis=0).astype(BF16)
        kv_all = jnp.dot(kws, vb, preferred_element_type=F32)
        qdb = qd.astype(BF16)
        inter = [None] * nchunk
        chunks = range(nchunk)
        for c in (reversed(chunks) if reverse else chunks):
            rows = slice(c * GLA_CHUNK, (c + 1) * GLA_CHUNK)
            inter[c] = jnp.dot(qdb[rows], state.astype(BF16), preferred_element_type=F32)
            col = c * GLA_CHUNK + last
            dec = jnp.exp(b_t[:, col:col + 1])
            kv = kv_all[c * GLA_QK_W:(c + 1) * GLA_QK_W]
            state = dec * state + jnp.where(bd, kv, 0.0)
        outs[gi] = o + jnp.concatenate(inter, axis=0)
    return jnp.concatenate(outs, axis=0) if len(outs) > 1 else outs[0], state


def _gla_kernel(qk_f, v_f, la_f, qk_b, v_b, la_b, s0_ref, of_ref, ob_ref, sfin_ref, st_ref):
    i = pl.program_id(1)

    @pl.when(i == 0)
    def _():
        st_ref[...] = s0_ref[0]

    qk = qk_f[0]
    o, s = _gla_direction(qk[:, 0:128], qk[:, 128:256], v_f[0], la_f[0], st_ref[0], False)
    of_ref[0] = o
    st_ref[0] = s
    qk = qk_b[0]
    o, s = _gla_direction(qk[:, 0:128], qk[:, 128:256], v_b[0], la_b[0], st_ref[1], True)
    ob_ref[0] = o
    st_ref[1] = s

    @pl.when(i == pl.num_programs(1) - 1)
    def _():
        sfin_ref[0] = st_ref[...]


def _gla(qkvg, la, s0, blk):
    bsz, seq, _ = qkvg.shape
    nb = seq // blk
    fwd = lambda j: pl.BlockSpec((1, blk, 256), lambda b, i: (b, i, j))
    bwd = lambda j: pl.BlockSpec((1, blk, 256), lambda b, i: (b, nb - 1 - i, j))
    la_f = pl.BlockSpec((1, blk, 128), lambda b, i: (b, i, 0))
    la_b = pl.BlockSpec((1, blk, 128), lambda b, i: (b, nb - 1 - i, 1))
    st = pl.BlockSpec((1, 2, GLA_QK_W, GROUP_W), lambda b, i: (b, 0, 0, 0))
    return pl.pallas_call(
        _gla_kernel,
        grid=(bsz, nb),
        in_specs=[fwd(0), fwd(1), la_f, bwd(0), bwd(1), la_b, st],
        out_specs=[pl.BlockSpec((1, blk, GROUP_W), lambda b, i: (b, i, 0)),
                   pl.BlockSpec((1, blk, GROUP_W), lambda b, i: (b, nb - 1 - i, 0)),
                   st],
        out_shape=[jax.ShapeDtypeStruct((bsz, seq, GROUP_W), F32),
                   jax.ShapeDtypeStruct((bsz, seq, GROUP_W), F32),
                   jax.ShapeDtypeStruct((bsz, 2, GLA_QK_W, GROUP_W), F32)],
        scratch_shapes=[pltpu.VMEM((2, GLA_QK_W, GROUP_W), F32)],
        compiler_params=_cp(("parallel", "arbitrary")),
        name="gla",
    )(qkvg, qkvg, la, qkvg, qkvg, la, s0)


def _row_shift(x, s, col, row_len):
    if s == 0:
        return x
    n = x.shape[0]
    y = pltpu.roll(x, (-s) % n, 0)
    valid = (col < row_len - s) if s > 0 else (col >= -s)
    return jnp.where(valid, y, 0.0)


def _row_conv(x, w, col, row_len):
    k = w.shape[0]
    acc = None
    for j in range(k):
        term = _row_shift(x, j - k // 2, col, row_len) * w[j:j + 1]
        acc = term if acc is None else acc + term
    return acc


def _col_conv(pad_ref, w, t0, n):
    k = w.shape[0]
    acc = None
    for j in range(k):
        start = pl.multiple_of(t0 + j * GRID_W, GRID_W)
        term = pad_ref[pl.ds(start, n), :] * w[j:j + 1]
        acc = term if acc is None else acc + term
    return acc


def _pool_diffs(x, col, row_len):
    colh = col[:, 0:LANE]
    colf = colh.astype(F32)
    first = lax.broadcasted_iota(jnp.int32, colh.shape, 1) < POOL_GW

    def count(win):
        return jnp.minimum(colf + win // 2, float(row_len)) - jnp.maximum(colf - win // 2, 0.0)

    means = []
    for hi in range(len(POOL_WINDOWS) // 2):
        wa, wb = POOL_WINDOWS[2 * hi], POOL_WINDOWS[2 * hi + 1]
        xh = x[:, hi * LANE:(hi + 1) * LANE]
        acc_a = acc_b = None
        for s in range(-(wb // 2), wb // 2):
            t = _row_shift(xh, s, colh, row_len)
            acc_b = t if acc_b is None else acc_b + t
            if -(wa // 2) <= s < wa // 2:
                acc_a = t if acc_a is None else acc_a + t
        means.append(jnp.where(first, acc_a / count(wa), acc_b / count(wb)))
    return jnp.concatenate(means, axis=-1) - x


def _local_tail(u, scb, conv_sc, pool, col, row_len, cb, lg, lb, pw, ps, out_ref):
    u = u + cb
    uc = u - jnp.mean(u, axis=-1, keepdims=True)
    ln = uc * lax.rsqrt(jnp.mean(uc * uc, axis=-1, keepdims=True) + NORM_EPS) * lg + lb
    out_ref[0, :, 0:256] = _silu(ln).astype(BF16)
    out_ref[0, :, 256:512] = (scb * conv_sc).astype(BF16)
    out_ref[0, :, 512:768] = (_dot(_pool_diffs(pool, col, row_len), pw) * ps).astype(BF16)


def _local_axial_kernel(uh_ref, uv_ref, scb_ref, sch_ref, scv_ref, pool_ref,
                        cw_ref, cb_ref, lg_ref, lb_ref, sw_ref, pw_ref, ps_ref,
                        out_ref, upad, spad):
    i = pl.program_id(1)
    blk = uh_ref.shape[1]
    seq = uv_ref.shape[1]
    pu = (CONF_WIDTH // 2) * GRID_W
    psc = (SC_WIDTH // 2) * GRID_W

    @pl.when(i == 0)
    def _():
        upad[0:pu, :] = jnp.zeros((pu, LANE), F32)
        upad[pu:pu + seq, :] = uv_ref[0]
        upad[pu + seq:pu + seq + pu, :] = jnp.zeros((pu, LANE), F32)
        spad[0:psc, :] = jnp.zeros((psc, LANE), F32)
        spad[psc:psc + seq, :] = scv_ref[0]
        spad[psc + seq:psc + seq + psc, :] = jnp.zeros((psc, LANE), F32)

    t0 = i * blk
    col = lax.broadcasted_iota(jnp.int32, (blk, LANE), 0) % GRID_W
    cw = cw_ref[...]
    sw = sw_ref[...]
    u = jnp.concatenate([_row_conv(uh_ref[0], cw[:, 0:128], col, GRID_W),
                         _col_conv(upad, cw[:, 128:256], t0, blk)], axis=-1)
    csc = jnp.concatenate([_row_conv(sch_ref[0], sw[:, 0:128], col, GRID_W),
                           _col_conv(spad, sw[:, 128:256], t0, blk)], axis=-1)
    col2 = jnp.concatenate([col, col], axis=-1)
    _local_tail(u, scb_ref[0], csc, pool_ref[0], col2, GRID_W, cb_ref[...], lg_ref[...],
                lb_ref[...], pw_ref[...], ps_ref[...], out_ref)


def _local_seq_kernel(u_ref, sc_ref, pool_ref, cw_ref, cb_ref, lg_ref, lb_ref, sw_ref,
                      pw_ref, ps_ref, out_ref):
    seq = u_ref.shape[1]
    col = lax.broadcasted_iota(jnp.int32, (seq, GROUP_W), 0)
    sc = sc_ref[0]
    u = _row_conv(u_ref[0], cw_ref[...], col, seq)
    csc = _row_conv(sc[:, 256:512], sw_ref[...], col, seq)
    _local_tail(u, sc[:, 0:256], csc, pool_ref[0], col, seq, cb_ref[...], lg_ref[...],
                lb_ref[...], pw_ref[...], ps_ref[...], out_ref)


def _local_params(p):
    return (p["conf_w"], p["conf_b"], p["conf_g"], p["conf_beta"], p["sc_w"], p["pool_bd"],
            p["pool_scale"])


def _local_axial(u, sc, pool, p, blk):
    bsz, seq, _ = u.shape
    cw, cb, lg, lb, sw, pw, ps = _local_params(p)
    blkspec = lambda w, j: pl.BlockSpec((1, blk, w), lambda b, i: (b, i, j))
    seqspec = lambda j: pl.BlockSpec((1, seq, LANE), lambda b, i: (b, 0, j))
    full = lambda a: pl.BlockSpec(a.shape, lambda b, i: (0,) * a.ndim)
    pu = (CONF_WIDTH // 2) * GRID_W
    psc = (SC_WIDTH // 2) * GRID_W
    return pl.pallas_call(
        _local_axial_kernel,
        grid=(bsz, seq // blk),
        in_specs=[blkspec(LANE, 0), seqspec(1), blkspec(256, 0), blkspec(LANE, 2), seqspec(3),
                  blkspec(256, 0), full(cw), full(cb), full(lg), full(lb), full(sw), full(pw),
                  full(ps)],
        out_specs=pl.BlockSpec((1, blk, 768), lambda b, i: (b, i, 0)),
        out_shape=jax.ShapeDtypeStruct((bsz, seq, 768), BF16),
        scratch_shapes=[pltpu.VMEM((seq + 2 * pu, LANE), F32),
                        pltpu.VMEM((seq + 2 * psc, LANE), F32)],
        compiler_params=_cp(("parallel", "arbitrary")),
        name="local_axial",
    )(u, u, sc, sc, sc, pool, cw, cb, lg, lb, sw, pw, ps)


def _local_seq(u, sc, pool, p):
    bsz, seq, _ = u.shape
    cw, cb, lg, lb, sw, pw, ps = _local_params(p)
    tok = lambda w: pl.BlockSpec((1, seq, w), lambda b: (b, 0, 0))
    full = lambda a: pl.BlockSpec(a.shape, lambda b: (0,) * a.ndim)
    return pl.pallas_call(
        _local_seq_kernel,
        grid=(bsz,),
        in_specs=[tok(256), tok(512), tok(256), full(cw), full(cb), full(lg), full(lb),
                  full(sw), full(pw), full(ps)],
        out_specs=tok(768),
        out_shape=jax.ShapeDtypeStruct((bsz, seq, 768), BF16),
        compiler_params=_cp(("parallel",)),
        name="local_seq",
    )(u, sc, pool, cw, cb, lg, lb, sw, pw, ps)


def _route(logits):
    lane = lax.broadcasted_iota(jnp.int32, logits.shape, 1)
    neg = -jnp.inf
    big = ROUTE_W

    def first_max(vals):
        mx = jnp.max(vals, axis=-1, keepdims=True)
        idx = jnp.min(jnp.where(vals == mx, lane, big), axis=-1, keepdims=True)
        return mx, idx

    lg = jnp.where(lane < N_GROUPS, logits, neg)
    gmx, grp = first_max(lg)
    p_grp = 1.0 / jnp.sum(jnp.exp(lg - gmx), axis=-1, keepdims=True)
    lo = N_GROUPS + grp * EXPERTS_PER_GROUP
    le = jnp.where((lane >= lo) & (lane < lo + EXPERTS_PER_GROUP), logits, neg)
    v1, i1 = first_max(le)
    v2, i2 = first_max(jnp.where(lane == i1, neg, le))
    e2 = jnp.exp(v2 - v1)
    w1 = p_grp / (1.0 + e2)
    w2 = p_grp * e2 / (1.0 + e2)
    rec = jnp.where(lane == 0, (i1 - N_GROUPS).astype(F32), 0.0)
    rec = jnp.where(lane == 1, (i2 - N_GROUPS).astype(F32), rec)
    rec = jnp.where(lane == 2, w1, rec)
    return jnp.where(lane == 3, w2, rec)


def _outproj_kernel(of_ref, ob_ref, g_ref, loc_ref, x_ref, mod_ref, gout_ref, hsum_ref,
                    wo_ref, g2_ref, wr_ref, br_ref, xo_ref, m_ref, route_ref):
    mod = mod_ref[0]
    o = of_ref[0] + ob_ref[0]
    o2 = o * o
    hi = o2.astype(BF16)
    lo = (o2 - hi.astype(F32)).astype(BF16)
    ms = (jnp.dot(hi, hsum_ref[...], preferred_element_type=F32)
          + jnp.dot(lo, hsum_ref[...], preferred_element_type=F32)) / GLA_DV
    y_gla = o * lax.rsqrt(ms + NORM_EPS) * gout_ref[...] * _silu(g_ref[0])
    wo = wo_ref[...]
    proj = _dot(y_gla, wo[0:256]) + _dot(loc_ref[0], wo[256:1024])
    x = x_ref[0] + mod[2:3] * proj
    xo_ref[0] = x
    m = _rms(x, g2_ref[...]) * (1.0 + mod[4:5]) + mod[3:4]
    m_ref[0] = m
    route_ref[0] = _route(_dot(m, wr_ref[...]) + br_ref[...])


def _outproj(o_f, o_b, qkvg, y_loc, x, mod, mod_row, p, tm):
    bsz, seq, d = x.shape
    row = (lambda b: b) if mod_row is None else (lambda b: mod_row)
    tok = lambda w, j=0: pl.BlockSpec((1, tm, w), lambda b, i: (b, i, j))
    full = lambda a: pl.BlockSpec(a.shape, lambda b, i: (0,) * a.ndim)
    consts = (p["g_out"], p["head_sum"], p["w_out"], p["g_norm2"], p["w_router"], p["b_router"])
    return pl.pallas_call(
        _outproj_kernel,
        grid=(bsz, seq // tm),
        in_specs=[tok(256), tok(256), tok(256, 2), tok(768), tok(d),
                  pl.BlockSpec((1, 6, d), lambda b, i: (row(b), 0, 0))] + [full(a) for a in consts],
        out_specs=[tok(d), tok(d), tok(ROUTE_W)],
        out_shape=[jax.ShapeDtypeStruct((bsz, seq, d), F32),
                   jax.ShapeDtypeStruct((bsz, seq, d), F32),
                   jax.ShapeDtypeStruct((bsz, seq, ROUTE_W), F32)],
        compiler_params=_cp(("parallel", "parallel")),
        name="outproj",
    )(o_f, o_b, qkvg, y_loc, x, mod, *consts)


RANK_ROWS = 512


def _rank_kernel(route_ref, rank_ref, cnt_ref, carry):
    i = pl.program_id(0)

    @pl.when(i == 0)
    def _():
        carry[...] = jnp.zeros_like(carry)

    rec = route_ref[...]
    tm = rec.shape[0]
    lane = lax.broadcasted_iota(jnp.int32, rec.shape, 1).astype(F32)
    oh0 = jnp.where(lane == rec[:, 0:1], 1.0, 0.0)
    oh1 = jnp.where(lane == rec[:, 1:2], 1.0, 0.0)
    ti = lax.broadcasted_iota(jnp.int32, (tm, tm), 0)
    tj = lax.broadcasted_iota(jnp.int32, (tm, tm), 1)
    before = jnp.where(tj < ti, 1.0, 0.0).astype(BF16)
    tot0 = jnp.sum(oh0, axis=0, keepdims=True)
    tot1 = jnp.sum(oh1, axis=0, keepdims=True)
    base = carry[...]
    pre0 = jnp.dot(before, oh0.astype(BF16), preferred_element_type=F32) + base
    pre1 = jnp.dot(before, oh1.astype(BF16), preferred_element_type=F32) + (base + tot0)
    r0 = jnp.sum(oh0 * pre0, axis=-1, keepdims=True)
    r1 = jnp.sum(oh1 * pre1, axis=-1, keepdims=True)
    lane_i = lax.broadcasted_iota(jnp.int32, rec.shape, 1)
    cols = jnp.where(lane_i == 0, r0, jnp.where(lane_i == 1, r1, 0.0))
    cols = jnp.where(lane_i == 2, rec[:, 0:1], jnp.where(lane_i == 3, rec[:, 1:2], cols))
    rank_ref[...] = cols.T[0:8, :].astype(jnp.int32)
    carry[...] = base + tot0 + tot1
    cnt_ref[...] = carry[...]


def _rank(route):
    n_tok = route.shape[0]
    tm = RANK_ROWS if n_tok % RANK_ROWS == 0 else 256
    return pl.pallas_call(
        _rank_kernel,
        grid=(n_tok // tm,),
        in_specs=[pl.BlockSpec((tm, ROUTE_W), lambda i: (i, 0))],
        out_specs=[pl.BlockSpec((8, tm), lambda i: (0, i)),
                   pl.BlockSpec((1, ROUTE_W), lambda i: (0, 0))],
        out_shape=[jax.ShapeDtypeStruct((8, n_tok), jnp.int32),
                   jax.ShapeDtypeStruct((1, ROUTE_W), F32)],
        scratch_shapes=[pltpu.VMEM((1, ROUTE_W), F32)],
        compiler_params=_cp(("arbitrary",)),
        name="rank",
    )(route)


def _plan(route):
    n_tok = route.shape[0]
    rank, cnt = _rank(route)
    counts = cnt[0, 0:N_EXPERTS].astype(jnp.int32)
    padded = (counts + MOE_BLOCK - 1) // MOE_BLOCK * MOE_BLOCK
    pad_end = jnp.cumsum(padded)
    pad_start = pad_end - padded
    onehot = rank[2:4][None] == jnp.arange(N_EXPERTS, dtype=jnp.int32)[:, None, None]
    dest = (jnp.sum(jnp.where(onehot, pad_start[:, None, None], 0), axis=0) + rank[0:2]).reshape(-1)
    n_blocks = 2 * n_tok // MOE_BLOCK + N_EXPERTS
    blk_start = jnp.arange(n_blocks, dtype=jnp.int32) * MOE_BLOCK
    nused = pad_end[-1] // MOE_BLOCK
    blk = jnp.minimum(jnp.arange(n_blocks, dtype=jnp.int32), nused - 1)
    blk_expert = jnp.sum((blk[:, None] * MOE_BLOCK >= pad_end[None, :]).astype(jnp.int32), axis=-1)
    fill_start = pad_start + counts
    fill_n = padded - counts
    return dict(dest=dest.astype(jnp.int32), blk=blk.astype(jnp.int32),
                blk_expert=jnp.minimum(blk_expert, N_EXPERTS - 1).astype(jnp.int32),
                nused=nused.astype(jnp.int32).reshape(1), fill_start=fill_start.astype(jnp.int32),
                fill_n=fill_n.astype(jnp.int32), n_blocks=n_blocks)


DISPATCH_ROWS = 256


def _dispatch_kernel(dest_ref, fstart_ref, fn_ref, nused_ref, *refs, tr, n_x, n_c):
    if n_c:
        mx_hbm, mc_hbm, xs_hbm, buf, zbuf, in_sem, out_sem, fsem = refs
    else:
        mx_hbm, xs_hbm, buf, zbuf, in_sem, out_sem, fsem = refs
        mc_hbm = None
    i = pl.program_id(0)
    nsteps = n_x + n_c
    nslot = 3

    def start_load(step):
        slot = step % nslot

        @pl.when(step < n_x)
        def _():
            rows = pl.ds(pl.multiple_of(step * tr, tr), tr)
            pltpu.make_async_copy(mx_hbm.at[rows], buf.at[slot], in_sem.at[slot]).start()

        if n_c:
            @pl.when(step >= n_x)
            def _():
                rows = pl.ds(pl.multiple_of((step - n_x) * tr, tr), tr)
                pltpu.make_async_copy(mc_hbm.at[rows], buf.at[slot], in_sem.at[slot]).start()

    def wait_load(step):
        slot = step % nslot
        pltpu.make_async_copy(mx_hbm.at[pl.ds(0, tr)], buf.at[slot], in_sem.at[slot]).wait()

    def row_copy(step, r, k):
        slot = step % nslot
        dst = dest_ref[k * (nsteps * tr) + step * tr + r]
        return pltpu.make_async_copy(buf.at[slot, pl.ds(r, 1)], xs_hbm.at[pl.ds(dst, 1)],
                                     out_sem.at[slot])

    def wait_scatter(step):
        slot = step % nslot
        for _ in range(2):
            pltpu.make_async_copy(buf.at[slot], xs_hbm.at[pl.ds(0, tr)], out_sem.at[slot]).wait()

    @pl.when(i == 0)
    def _():
        zbuf[...] = jnp.zeros_like(zbuf)
        start_load(i)

    @pl.when(i >= 2)
    def _():
        wait_scatter(i - 2)

    @pl.when(i + 1 < nsteps)
    def _():
        start_load(i + 1)

    wait_load(i)

    for r in range(tr):
        row_copy(i, r, 0).start()
        row_copy(i, r, 1).start()

    @pl.when(i == nsteps - 1)
    def _():
        if nsteps >= 2:
            wait_scatter(i - 1)
        wait_scatter(i)

        def fill_copy(e, r):
            return pltpu.make_async_copy(zbuf.at[pl.ds(0, 1)],
                                         xs_hbm.at[pl.ds(fstart_ref[e] + r, 1)], fsem.at[0])

        def per_expert(e, carry):
            lax.fori_loop(0, fn_ref[e], lambda r, c: (fill_copy(e, r).start(), c)[1], 0)
            return carry

        def per_expert_wait(e, carry):
            lax.fori_loop(0, fn_ref[e], lambda r, c: (fill_copy(e, r).wait(), c)[1], 0)
            return carry

        lax.fori_loop(0, N_EXPERTS, per_expert, 0)
        lax.fori_loop(0, N_EXPERTS, per_expert_wait, 0)

        def tail_copy(b):
            start = pl.multiple_of(b * MOE_BLOCK, MOE_BLOCK)
            return pltpu.make_async_copy(zbuf, xs_hbm.at[pl.ds(start, MOE_BLOCK)], fsem.at[0])

        nblk = xs_hbm.shape[0] // MOE_BLOCK
        lax.fori_loop(nused_ref[0], nblk, lambda b, c: (tail_copy(b).start(), c)[1], 0)
        lax.fori_loop(nused_ref[0], nblk, lambda b, c: (tail_copy(b).wait(), c)[1], 0)


def _dispatch(m_x, m_c, plan):
    n_lat, d = m_x.shape
    n_ctx = 0 if m_c is None else m_c.shape[0]
    tr = DISPATCH_ROWS
    while n_lat % tr or n_ctx % tr:
        tr //= 2
    slots = plan["n_blocks"] * MOE_BLOCK
    n_x, n_c = n_lat // tr, n_ctx // tr
    srcs = (m_x,) if m_c is None else (m_x, m_c)
    grid_spec = pltpu.PrefetchScalarGridSpec(
        num_scalar_prefetch=4,
        grid=(n_x + n_c,),
        in_specs=[pl.BlockSpec(memory_space=pl.ANY)] * len(srcs),
        out_specs=pl.BlockSpec(memory_space=pl.ANY),
        scratch_shapes=[pltpu.VMEM((3, tr, d), F32), pltpu.VMEM((MOE_BLOCK, d), F32),
                        pltpu.SemaphoreType.DMA((3,)), pltpu.SemaphoreType.DMA((3,)),
                        pltpu.SemaphoreType.DMA((1,))],
    )
    return pl.pallas_call(
        functools.partial(_dispatch_kernel, tr=tr, n_x=n_x, n_c=n_c),
        grid_spec=grid_spec,
        out_shape=jax.ShapeDtypeStruct((slots, d), F32),
        compiler_params=_cp(("arbitrary",)),
        name="dispatch",
    )(plan["dest"], plan["fill_start"], plan["fill_n"], plan["nused"], *srcs)


def _expert_kernel(blk_ref, be_ref, nused_ref, xs_ref, w1_ref, w2_ref, ys_ref, w1b, w2b):
    i = pl.program_id(0)
    used = i < nused_ref[0]
    fresh = (i == 0) | (be_ref[i] != be_ref[jnp.maximum(i - 1, 0)])

    @pl.when(used & fresh)
    def _():
        w1b[...] = w1_ref[0, 0].astype(BF16)
        w2b[...] = w2_ref[0, 0].astype(BF16)

    @pl.when(used)
    def _():
        h = _dot(xs_ref[...], w1b[...])
        act = _silu(h[:, :D_EXPERT]) * h[:, D_EXPERT:]
        ys_ref[...] = _dot(act, w2b[...])

    @pl.when(jnp.logical_not(used))
    def _():
        ys_ref[...] = jnp.zeros_like(ys_ref)


def _experts(xs, plan, w1, w2, layer):
    slots, d = xs.shape
    grid_spec = pltpu.PrefetchScalarGridSpec(
        num_scalar_prefetch=3,
        grid=(plan["n_blocks"],),
        in_specs=[pl.BlockSpec((MOE_BLOCK, d), lambda i, blk, be, nu: (blk[i], 0)),
                  pl.BlockSpec((1, 1, d, 2 * D_EXPERT), lambda i, blk, be, nu: (layer, be[i], 0, 0)),
                  pl.BlockSpec((1, 1, D_EXPERT, d), lambda i, blk, be, nu: (layer, be[i], 0, 0))],
        out_specs=pl.BlockSpec((MOE_BLOCK, d), lambda i, blk, be, nu: (i, 0)),
        scratch_shapes=[pltpu.VMEM((d, 2 * D_EXPERT), BF16), pltpu.VMEM((D_EXPERT, d), BF16)],
    )
    return pl.pallas_call(
        _expert_kernel,
        grid_spec=grid_spec,
        out_shape=jax.ShapeDtypeStruct((slots, d), F32),
        compiler_params=_cp(("arbitrary",)),
        name="experts",
    )(plan["blk"], plan["blk_expert"], plan["nused"], xs, w1, w2)


COMBINE_ROWS = 256


def _combine_kernel(dest_ref, x_ref, route_ref, mod_ref, gf_ref, ys_hbm, o_ref, y0buf, y1buf, sem,
                    *, final, tok_off):
    i = pl.program_id(0)
    nsteps = pl.num_programs(0)
    tc = COMBINE_ROWS

    def row_copy(step, r, k, slot):
        a = k * (dest_ref.shape[0] // 2) + tok_off + step * tc + r
        buf = y0buf if k == 0 else y1buf
        return pltpu.make_async_copy(ys_hbm.at[pl.ds(dest_ref[a], 1)], buf.at[slot, pl.ds(r, 1)],
                                     sem.at[slot])

    def issue_step(step):
        for r in range(tc):
            row_copy(step, r, 0, step % 2).start()
            row_copy(step, r, 1, step % 2).start()

    @pl.when(i == 0)
    def _():
        issue_step(0)

    @pl.when(i + 1 < nsteps)
    def _():
        issue_step(i + 1)

    slot = i % 2
    pltpu.make_async_copy(ys_hbm.at[pl.ds(0, tc)], y0buf.at[slot], sem.at[slot]).wait()
    pltpu.make_async_copy(ys_hbm.at[pl.ds(0, tc)], y1buf.at[slot], sem.at[slot]).wait()
    rec = route_ref[...]
    y = rec[:, 2:3] * y0buf[slot] + rec[:, 3:4] * y1buf[slot]
    x = x_ref[0] + mod_ref[0][5:6] * y
    o_ref[0] = _rms(x, gf_ref[...]) if final else x


def _combine(x, ys, route, plan, tok_off, mod, mod_row, g_final, final):
    bsz, seq, d = x.shape
    tc = COMBINE_ROWS
    nt = seq // tc
    assert seq % tc == 0 and tok_off % tc == 0
    row = (lambda i: i // nt) if mod_row is None else (lambda i: mod_row)
    grid_spec = pltpu.PrefetchScalarGridSpec(
        num_scalar_prefetch=1,
        grid=(bsz * nt,),
        in_specs=[pl.BlockSpec((1, tc, d), lambda i, de: (i // nt, i % nt, 0)),
                  pl.BlockSpec((tc, ROUTE_W), lambda i, de: (tok_off // tc + i, 0)),
                  pl.BlockSpec((1, 6, d), lambda i, de: (row(i), 0, 0)),
                  pl.BlockSpec((1, d), lambda i, de: (0, 0)),
                  pl.BlockSpec(memory_space=pl.ANY)],
        out_specs=pl.BlockSpec((1, tc, d), lambda i, de: (i // nt, i % nt, 0)),
        scratch_shapes=[pltpu.VMEM((2, tc, d), F32), pltpu.VMEM((2, tc, d), F32),
                        pltpu.SemaphoreType.DMA((2,))],
    )
    return pl.pallas_call(
        functools.partial(_combine_kernel, final=final, tok_off=tok_off),
        grid_spec=grid_spec,
        out_shape=jax.ShapeDtypeStruct((bsz, seq, d), F32),
        compiler_params=_cp(("arbitrary",)),
        name="combine",
    )(plan["dest"], x, route, mod, g_final, ys)


def _layer_params(l, w_in, gla_w_gate_f, gla_b_gate_f, gla_w_gate_b, gla_b_gate_b, gla_g_out,
                  conf_w_dw, conf_b_dw, conf_ln_g, conf_ln_b, sc_w_dw, pool_w, pool_scale, w_out,
                  router_w_group, router_b_group, router_w_expert, router_b_expert,
                  expert_w_in, expert_w_out, g_norm1, g_norm2):
    d = D_MODEL
    idx = np.cumsum((0,) + IN_SPLITS)
    q, k, v, g, zf, zb, conf, scb, scc, scx, pool = [w_in[l][:, idx[j]:idx[j + 1]] for j in range(11)]
    pad = jnp.zeros((d, N_IN_PAD - idx[-1]), F32)
    w_in_p = jnp.concatenate([q, k, v, g, conf, scb, scc, scx, pool, zf, zb, pad], axis=1).astype(BF16)
    r = GLA_GATE_RANK
    w_gate = jnp.zeros((LANE, 2 * GLA_QK_W), F32)
    w_gate = w_gate.at[0:r, 0:GLA_QK_W].set(gla_w_gate_f[l]).at[r:2 * r, GLA_QK_W:].set(gla_w_gate_b[l])
    b_gate = jnp.concatenate([gla_b_gate_f[l], gla_b_gate_b[l]])[None]
    pool_bd = jnp.zeros((GROUP_W, GROUP_W), F32)
    for gi in range(len(POOL_WINDOWS)):
        s = slice(gi * POOL_GW, (gi + 1) * POOL_GW)
        pool_bd = pool_bd.at[s, s].set(pool_w[l, gi])
    head = np.arange(GROUP_W) // GLA_DV
    head_sum = jnp.asarray(head[:, None] == head[None, :], BF16)
    w_router = jnp.zeros((d, ROUTE_W), F32)
    w_router = w_router.at[:, 0:N_GROUPS].set(router_w_group[l])
    w_router = w_router.at[:, N_GROUPS:N_GROUPS + N_EXPERTS].set(router_w_expert[l])
    b_router = jnp.zeros((1, ROUTE_W), F32)
    b_router = b_router.at[0, 0:N_GROUPS].set(router_b_group[l])
    b_router = b_router.at[0, N_GROUPS:N_GROUPS + N_EXPERTS].set(router_b_expert[l])
    return dict(
        w_in=w_in_p, w_gate=w_gate.astype(BF16), b_gate=b_gate,
        g_norm1=g_norm1[l][None], g_norm2=g_norm2[l][None],
        g_out=gla_g_out[l][None], head_sum=head_sum,
        conf_w=conf_w_dw[l], conf_b=conf_b_dw[l][None], conf_g=conf_ln_g[l][None],
        conf_beta=conf_ln_b[l][None], sc_w=sc_w_dw[l], pool_bd=pool_bd.astype(BF16),
        pool_scale=pool_scale[l][None], w_out=w_out[l].astype(BF16),
        w_router=w_router.astype(BF16), b_router=b_router)


def kernel(x, c, ctx, c_ctx, w_mod, b_mod, g_norm1, g_norm2, w_in, gla_w_gate_f, gla_b_gate_f,
           gla_w_gate_b, gla_b_gate_b, gla_g_out, conf_w_dw, conf_b_dw, conf_ln_g, conf_ln_b,
           sc_w_dw, pool_w, pool_scale, w_out, router_w_group, router_b_group, router_w_expert,
           router_b_expert, expert_w_in, expert_w_out, g_final):
    bsz, seq, d = x.shape
    ctx_len = ctx.shape[1]
    depth = w_mod.shape[0]
    n_lat = bsz * seq
    n_ctx = bsz * ctx_len
    assert d == D_MODEL and seq % 1024 == 0 and ctx_len % GLA_GROUP == 0 and bsz <= 7
    tm = 1024
    tm_c = 256

    cv = jnp.zeros((8, d), F32).at[0:bsz].set(c).at[bsz].set(c_ctx)
    mods = _adaln(cv, w_mod, b_mod).reshape(depth, 8, 6, d)
    gf = g_final[None]
    zero_state = jnp.zeros((bsz, 2, GLA_QK_W, GROUP_W), F32)

    for l in range(depth):
        last = l == depth - 1
        p = _layer_params(l, w_in, gla_w_gate_f, gla_b_gate_f, gla_w_gate_b, gla_b_gate_b,
                          gla_g_out, conf_w_dw, conf_b_dw, conf_ln_g, conf_ln_b, sc_w_dw, pool_w,
                          pool_scale, w_out, router_w_group, router_b_group, router_w_expert,
                          router_b_expert, expert_w_in, expert_w_out, g_norm1, g_norm2)
        mod = mods[l]

        c_qkvg, c_la, c_u, c_sc, c_pool = _inproj(ctx, mod, bsz, p["g_norm1"], p["w_in"],
                                                  p["w_gate"], p["b_gate"], tm_c)
        c_of, c_ob, s_ctx = _gla(c_qkvg, c_la, zero_state, ctx_len)
        x_qkvg, x_la, x_u, x_sc, x_pool = _inproj(x, mod, None, p["g_norm1"], p["w_in"],
                                                  p["w_gate"], p["b_gate"], tm)
        x_of, x_ob, _ = _gla(x_qkvg, x_la, s_ctx, min(1024, seq))
        x_loc = _local_axial(x_u, x_sc, x_pool, p, min(2048, seq))
        x, m_x, r_x = _outproj(x_of, x_ob, x_qkvg, x_loc, x, mod, None, p, tm)

        if last:
            m_c = None
            route = r_x.reshape(n_lat, ROUTE_W)
        else:
            c_loc = _local_seq(c_u, c_sc, c_pool, p)
            ctx, m_c, r_c = _outproj(c_of, c_ob, c_qkvg, c_loc, ctx, mod, bsz, p, tm_c)
            m_c = m_c.reshape(n_ctx, d)
            route = jnp.concatenate([r_x.reshape(n_lat, ROUTE_W), r_c.reshape(n_ctx, ROUTE_W)], axis=0)

        plan = _plan(route)
        xs = _dispatch(m_x.reshape(n_lat, d), m_c, plan)
        ys = _experts(xs, plan, expert_w_in, expert_w_out, l)
        x = _combine(x, ys, route, plan, 0, mod, None, gf, last)
        if not last:
            ctx = _combine(ctx, ys, route, plan, n_lat, mod, bsz, gf, False)
    return x
```

```python
import functools

import numpy as np
import jax
import jax.numpy as jnp
from jax import lax
from jax.experimental import pallas as pl
from jax.experimental.pallas import tpu as pltpu

F32 = jnp.float32
BF16 = jnp.bfloat16

D_MODEL = 1024
GRID_W = 64
GROUP_W = 256
GLA_HEADS = 4
GLA_DV = 64
GLA_DK = 32
GLA_QK_W = 128
GLA_GATE_RANK = 16
GLA_GATE_NORM = 16.0
GLA_CHUNK = 64
GLA_GROUP = 256
CONF_WIDTH = 31
SC_WIDTH = 3
POOL_WINDOWS = (2, 4, 8, 16)
POOL_GW = 64
N_GROUPS = 4
EXPERTS_PER_GROUP = 8
N_EXPERTS = 32
D_EXPERT = 512
MOE_BLOCK = 512
NORM_EPS = 1e-6
IN_SPLITS = (128, 128, 256, 256, 16, 16, 512, 256, 256, 256, 256)
N_IN_PAD = 2432
LANE = 128
ROUTE_W = LANE

VMEM_LIMIT = 48 << 20


def _cp(sem, vmem=VMEM_LIMIT):
    return pltpu.CompilerParams(dimension_semantics=sem, vmem_limit_bytes=vmem)


def _sigmoid(x):
    return 1.0 / (1.0 + jnp.exp(-x))


def _silu(x):
    return x * _sigmoid(x)


def _log_sigmoid(x):
    return jnp.minimum(x, 0.0) - jnp.log1p(jnp.exp(-jnp.abs(x)))


def _dot(a, b):
    return jnp.dot(a.astype(BF16), b.astype(BF16), preferred_element_type=F32)


def _rms(x, g):
    return x * lax.rsqrt(jnp.mean(x * x, axis=-1, keepdims=True) + NORM_EPS) * g


def _adaln_kernel(cv_ref, w_ref, b_ref, o_ref):
    o_ref[0] = _dot(_silu(cv_ref[...]), w_ref[0]) + b_ref[0]


def _adaln(cv, w_mod, b_mod):
    depth, d, n = w_mod.shape
    tn = 1024
    return pl.pallas_call(
        _adaln_kernel,
        grid=(depth, n // tn),
        in_specs=[pl.BlockSpec((8, d), lambda l, j: (0, 0)),
                  pl.BlockSpec((1, d, tn), lambda l, j: (l, 0, j)),
                  pl.BlockSpec((1, 1, tn), lambda l, j: (l, 0, j))],
        out_specs=pl.BlockSpec((1, 8, tn), lambda l, j: (l, 0, j)),
        out_shape=jax.ShapeDtypeStruct((depth, 8, n), F32),
        compiler_params=_cp(("parallel", "parallel")),
        name="adaln",
    )(cv, w_mod, b_mod.reshape(depth, 1, n))


def _inproj_kernel(x_ref, mod_ref, g1_ref, w_ref, wg_ref, bg_ref,
                   qkvg_ref, la_ref, u_ref, sc_ref, pool_ref):
    x = x_ref[0]
    mod = mod_ref[0]
    m = _rms(x, g1_ref[...]) * (1.0 + mod[1:2]) + mod[0:1]
    h = _dot(m, w_ref[...])
    qkvg_ref[0] = h[:, 0:768]
    u_ref[0] = h[:, 768:1024] * _sigmoid(h[:, 1024:1280])
    sc_ref[0, :, 0:256] = h[:, 1280:1536]
    sc_ref[0, :, 256:512] = h[:, 1536:1792] * h[:, 1792:2048]
    pool_ref[0] = h[:, 2048:2304]
    z = _dot(h[:, 2304:2432], wg_ref[...]) + bg_ref[...]
    la_ref[0] = _log_sigmoid(z) / GLA_GATE_NORM


def _inproj(x, mod, mod_row, g1, w_in_p, w_gate, b_gate, tm):
    bsz, seq, d = x.shape
    row = (lambda b: b) if mod_row is None else (lambda b: mod_row)
    tok = lambda w: pl.BlockSpec((1, tm, w), lambda b, i: (b, i, 0))
    full = lambda a: pl.BlockSpec(a.shape, lambda b, i: (0,) * a.ndim)
    outs = (768, 256, 256, 512, 256)
    return pl.pallas_call(
        _inproj_kernel,
        grid=(bsz, seq // tm),
        in_specs=[tok(d),
                  pl.BlockSpec((1, 6, d), lambda b, i: (row(b), 0, 0)),
                  full(g1), full(w_in_p), full(w_gate), full(b_gate)],
        out_specs=[tok(w) for w in outs],
        out_shape=[jax.ShapeDtypeStruct((bsz, seq, w), F32) for w in outs],
        compiler_params=_cp(("parallel", "parallel")),
        name="inproj",
    )(x, mod, g1, w_in_p, w_gate, b_gate)


def _gla_direction(q, k, v, la, state, reverse):
    blk = q.shape[0]
    g = GLA_GROUP
    nchunk = g // GLA_CHUNK
    ti = lax.broadcasted_iota(jnp.int32, (g, g), 0)
    tj = lax.broadcasted_iota(jnp.int32, (g, g), 1)
    same = (ti // GLA_CHUNK) == (tj // GLA_CHUNK)
    cmask = same & ((tj >= ti) if reverse else (tj <= ti))
    tri = jnp.where(cmask, 1.0, 0.0).astype(BF16)
    qk_head = lax.broadcasted_iota(jnp.int32, (g, GLA_QK_W), 1) // GLA_DK
    v_head = lax.broadcasted_iota(jnp.int32, (g, GROUP_W), 1) // GLA_DV
    bd = (lax.broadcasted_iota(jnp.int32, (GLA_QK_W, GROUP_W), 0) // GLA_DK
          == lax.broadcasted_iota(jnp.int32, (GLA_QK_W, GROUP_W), 1) // GLA_DV)
    tok_chunk = lax.broadcasted_iota(jnp.int32, (GLA_QK_W, g), 1) // GLA_CHUNK
    scale = GLA_DK ** -0.5
    outs = [None] * (blk // g)
    groups = range(blk // g)
    for gi in (reversed(groups) if reverse else groups):
        sl = slice(gi * g, (gi + 1) * g)
        qg, kg, vg, lg = q[sl], k[sl], v[sl], la[sl]
        l0 = lg.astype(BF16)
        r0 = lg - l0.astype(F32)
        l1 = r0.astype(BF16)
        l2 = (r0 - l1.astype(F32)).astype(BF16)
        b3 = jnp.dot(tri, jnp.concatenate([l0, l1, l2], axis=1), preferred_element_type=F32)
        b = b3[:, 0:GLA_QK_W] + b3[:, GLA_QK_W:2 * GLA_QK_W] + b3[:, 2 * GLA_QK_W:3 * GLA_QK_W]
        last = 0 if reverse else GLA_CHUNK - 1
        blast = jnp.concatenate(
            [jnp.broadcast_to(b[c * GLA_CHUNK + last:c * GLA_CHUNK + last + 1], (GLA_CHUNK, GLA_QK_W))
             for c in range(nchunk)], axis=0)
        qd = (qg * scale) * jnp.exp(b)
        ki = (kg * jnp.exp(-b)).astype(BF16)
        kw_t = (kg * jnp.exp(blast - b)).T
        b_t = b.T
        vb = vg.astype(BF16)
        qs = jnp.concatenate([jnp.where(qk_head == h, qd, 0.0) for h in range(GLA_HEADS)],
                             axis=0).astype(BF16)
        s = lax.dot_general(qs, ki, (((1,), (1,)), ((), ())), preferred_element_type=F32)
        s = jnp.where(jnp.concatenate([cmask] * GLA_HEADS, axis=0), s, 0.0).astype(BF16)
        oh = jnp.dot(s, vb, preferred_element_type=F32)
        o = jnp.zeros((g, GROUP_W), F32)
        for h in range(GLA_HEADS):
            o = o + jnp.where(v_head == h, oh[h * g:(h + 1) * g], 0.0)
        kws = jnp.concatenate([jnp.where(tok_chunk == c, kw_t, 0.0) for c in range(nchunk)],
                              axis=0).astype(BF16)
        kv_all = jnp.dot(kws, vb, preferred_element_type=F32)
        qdb = qd.astype(BF16)
        inter = [None] * nchunk
        chunks = range(nchunk)
        for c in (reversed(chunks) if reverse else chunks):
            rows = slice(c * GLA_CHUNK, (c + 1) * GLA_CHUNK)
            inter[c] = jnp.dot(qdb[rows], state.astype(BF16), preferred_element_type=F32)
            col = c * GLA_CHUNK + last
            dec = jnp.exp(b_t[:, col:col + 1])
            kv = kv_all[c * GLA_QK_W:(c + 1) * GLA_QK_W]
            state = dec * state + jnp.where(bd, kv, 0.0)
        outs[gi] = o + jnp.concatenate(inter, axis=0)
    return jnp.concatenate(outs, axis=0) if len(outs) > 1 else outs[0], state


def _gla_kernel(qk_f, v_f, la_f, qk_b, v_b, la_b, s0_ref, of_ref, ob_ref, sfin_ref, st_ref):
    i = pl.program_id(1)

    @pl.when(i == 0)
    def _():
        st_ref[...] = s0_ref[0]

    qk = qk_f[0]
    o, s = _gla_direction(qk[:, 0:128], qk[:, 128:256], v_f[0], la_f[0], st_ref[0], False)
    of_ref[0] = o
    st_ref[0] = s
    qk = qk_b[0]
    o, s = _gla_direction(qk[:, 0:128], qk[:, 128:256], v_b[0], la_b[0], st_ref[1], True)
    ob_ref[0] = o
    st_ref[1] = s

    @pl.when(i == pl.num_programs(1) - 1)
    def _():
        sfin_ref[0] = st_ref[...]


def _gla(qkvg, la, s0, blk):
    bsz, seq, _ = qkvg.shape
    nb = seq // blk
    fwd = lambda j: pl.BlockSpec((1, blk, 256), lambda b, i: (b, i, j))
    bwd = lambda j: pl.BlockSpec((1, blk, 256), lambda b, i: (b, nb - 1 - i, j))
    la_f = pl.BlockSpec((1, blk, 128), lambda b, i: (b, i, 0))
    la_b = pl.BlockSpec((1, blk, 128), lambda b, i: (b, nb - 1 - i, 1))
    st = pl.BlockSpec((1, 2, GLA_QK_W, GROUP_W), lambda b, i: (b, 0, 0, 0))
    return pl.pallas_call(
        _gla_kernel,
        grid=(bsz, nb),
        in_specs=[fwd(0), fwd(1), la_f, bwd(0), bwd(1), la_b, st],
        out_specs=[pl.BlockSpec((1, blk, GROUP_W), lambda b, i: (b, i, 0)),
                   pl.BlockSpec((1, blk, GROUP_W), lambda b, i: (b, nb - 1 - i, 0)),
                   st],
        out_shape=[jax.ShapeDtypeStruct((bsz, seq, GROUP_W), F32),
                   jax.ShapeDtypeStruct((bsz, seq, GROUP_W), F32),
                   jax.ShapeDtypeStruct((bsz, 2, GLA_QK_W, GROUP_W), F32)],
        scratch_shapes=[pltpu.VMEM((2, GLA_QK_W, GROUP_W), F32)],
        compiler_params=_cp(("parallel", "arbitrary")),
        name="gla",
    )(qkvg, qkvg, la, qkvg, qkvg, la, s0)


def _row_shift(x, s, col, row_len):
    if s == 0:
        return x
    n = x.shape[0]
    y = pltpu.roll(x, (-s) % n, 0)
    valid = (col < row_len - s) if s > 0 else (col >= -s)
    return jnp.where(valid, y, 0.0)


def _row_conv(x, w, col, row_len):
    k = w.shape[0]
    acc = None
    for j in range(k):
        term = _row_shift(x, j - k // 2, col, row_len) * w[j:j + 1]
        acc = term if acc is None else acc + term
    return acc


def _col_conv(pad_ref, w, t0, n):
    k = w.shape[0]
    acc = None
    for j in range(k):
        start = pl.multiple_of(t0 + j * GRID_W, GRID_W)
        term = pad_ref[pl.ds(start, n), :] * w[j:j + 1]
        acc = term if acc is None else acc + term
    return acc


def _pool_diffs(x, col, row_len):
    colh = col[:, 0:LANE]
    colf = colh.astype(F32)
    first = lax.broadcasted_iota(jnp.int32, colh.shape, 1) < POOL_GW

    def count(win):
        return jnp.minimum(colf + win // 2, float(row_len)) - jnp.maximum(colf - win // 2, 0.0)

    means = []
    for hi in range(len(POOL_WINDOWS) // 2):
        wa, wb = POOL_WINDOWS[2 * hi], POOL_WINDOWS[2 * hi + 1]
        xh = x[:, hi * LANE:(hi + 1) * LANE]
        acc_a = acc_b = None
        for s in range(-(wb // 2), wb // 2):
            t = _row_shift(xh, s, colh, row_len)
            acc_b = t if acc_b is None else acc_b + t
            if -(wa // 2) <= s < wa // 2:
                acc_a = t if acc_a is None else acc_a + t
        means.append(jnp.where(first, acc_a / count(wa), acc_b / count(wb)))
    return jnp.concatenate(means, axis=-1) - x


def _local_tail(u, scb, conv_sc, pool, col, row_len, cb, lg, lb, pw, ps, out_ref):
    u = u + cb
    uc = u - jnp.mean(u, axis=-1, keepdims=True)
    ln = uc * lax.rsqrt(jnp.mean(uc * uc, axis=-1, keepdims=True) + NORM_EPS) * lg + lb
    out_ref[0, :, 0:256] = _silu(ln).astype(BF16)
    out_ref[0, :, 256:512] = (scb * conv_sc).astype(BF16)
    out_ref[0, :, 512:768] = (_dot(_pool_diffs(pool, col, row_len), pw) * ps).astype(BF16)


def _local_axial_kernel(uh_ref, uv_ref, scb_ref, sch_ref, scv_ref, pool_ref,
                        cw_ref, cb_ref, lg_ref, lb_ref, sw_ref, pw_ref, ps_ref,
                        out_ref, upad, spad):
    i = pl.program_id(1)
    blk = uh_ref.shape[1]
    seq = uv_ref.shape[1]
    pu = (CONF_WIDTH // 2) * GRID_W
    psc = (SC_WIDTH // 2) * GRID_W

    @pl.when(i == 0)
    def _():
        upad[0:pu, :] = jnp.zeros((pu, LANE), F32)
        upad[pu:pu + seq, :] = uv_ref[0]
        upad[pu + seq:pu + seq + pu, :] = jnp.zeros((pu, LANE), F32)
        spad[0:psc, :] = jnp.zeros((psc, LANE), F32)
        spad[psc:psc + seq, :] = scv_ref[0]
        spad[psc + seq:psc + seq + psc, :] = jnp.zeros((psc, LANE), F32)

    t0 = i * blk
    col = lax.broadcasted_iota(jnp.int32, (blk, LANE), 0) % GRID_W
    cw = cw_ref[...]
    sw = sw_ref[...]
    u = jnp.concatenate([_row_conv(uh_ref[0], cw[:, 0:128], col, GRID_W),
                         _col_conv(upad, cw[:, 128:256], t0, blk)], axis=-1)
    csc = jnp.concatenate([_row_conv(sch_ref[0], sw[:, 0:128], col, GRID_W),
                           _col_conv(spad, sw[:, 128:256], t0, blk)], axis=-1)
    col2 = jnp.concatenate([col, col], axis=-1)
    _local_tail(u, scb_ref[0], csc, pool_ref[0], col2, GRID_W, cb_ref[...], lg_ref[...],
                lb_ref[...], pw_ref[...], ps_ref[...], out_ref)


def _local_seq_kernel(u_ref, sc_ref, pool_ref, cw_ref, cb_ref, lg_ref, lb_ref, sw_ref,
                      pw_ref, ps_ref, out_ref):
    seq = u_ref.shape[1]
    col = lax.broadcasted_iota(jnp.int32, (seq, GROUP_W), 0)
    sc = sc_ref[0]
    u = _row_conv(u_ref[0], cw_ref[...], col, seq)
    csc = _row_conv(sc[:, 256:512], sw_ref[...], col, seq)
    _local_tail(u, sc[:, 0:256], csc, pool_ref[0], col, seq, cb_ref[...], lg_ref[...],
                lb_ref[...], pw_ref[...], ps_ref[...], out_ref)


def _local_params(p):
    return (p["conf_w"], p["conf_b"], p["conf_g"], p["conf_beta"], p["sc_w"], p["pool_bd"],
            p["pool_scale"])


def _local_axial(u, sc, pool, p, blk):
    bsz, seq, _ = u.shape
    cw, cb, lg, lb, sw, pw, ps = _local_params(p)
    blkspec = lambda w, j: pl.BlockSpec((1, blk, w), lambda b, i: (b, i, j))
    seqspec = lambda j: pl.BlockSpec((1, seq, LANE), lambda b, i: (b, 0, j))
    full = lambda a: pl.BlockSpec(a.shape, lambda b, i: (0,) * a.ndim)
    pu = (CONF_WIDTH // 2) * GRID_W
    psc = (SC_WIDTH // 2) * GRID_W
    return pl.pallas_call(
        _local_axial_kernel,
        grid=(bsz, seq // blk),
        in_specs=[blkspec(LANE, 0), seqspec(1), blkspec(256, 0), blkspec(LANE, 2), seqspec(3),
                  blkspec(256, 0), full(cw), full(cb), full(lg), full(lb), full(sw), full(pw),
                  full(ps)],
        out_specs=pl.BlockSpec((1, blk, 768), lambda b, i: (b, i, 0)),
        out_shape=jax.ShapeDtypeStruct((bsz, seq, 768), BF16),
        scratch_shapes=[pltpu.VMEM((seq + 2 * pu, LANE), F32),
                        pltpu.VMEM((seq + 2 * psc, LANE), F32)],
        compiler_params=_cp(("parallel", "arbitrary")),
        name="local_axial",
    )(u, u, sc, sc, sc, pool, cw, cb, lg, lb, sw, pw, ps)


def _local_seq(u, sc, pool, p):
    bsz, seq, _ = u.shape
    cw, cb, lg, lb, sw, pw, ps = _local_params(p)
    tok = lambda w: pl.BlockSpec((1, seq, w), lambda b: (b, 0, 0))
    full = lambda a: pl.BlockSpec(a.shape, lambda b: (0,) * a.ndim)
    return pl.pallas_call(
        _local_seq_kernel,
        grid=(bsz,),
        in_specs=[tok(256), tok(512), tok(256), full(cw), full(cb), full(lg), full(lb),
                  full(sw), full(pw), full(ps)],
        out_specs=tok(768),
        out_shape=jax.ShapeDtypeStruct((bsz, seq, 768), BF16),
        compiler_params=_cp(("parallel",)),
        name="local_seq",
    )(u, sc, pool, cw, cb, lg, lb, sw, pw, ps)


def _route(logits):
    lane = lax.broadcasted_iota(jnp.int32, logits.shape, 1)
    neg = -jnp.inf
    big = ROUTE_W

    def first_max(vals):
        mx = jnp.max(vals, axis=-1, keepdims=True)
        idx = jnp.min(jnp.where(vals == mx, lane, big), axis=-1, keepdims=True)
        return mx, idx

    lg = jnp.where(lane < N_GROUPS, logits, neg)
    gmx, grp = first_max(lg)
    p_grp = 1.0 / jnp.sum(jnp.exp(lg - gmx), axis=-1, keepdims=True)
    lo = N_GROUPS + grp * EXPERTS_PER_GROUP
    le = jnp.where((lane >= lo) & (lane < lo + EXPERTS_PER_GROUP), logits, neg)
    v1, i1 = first_max(le)
    v2, i2 = first_max(jnp.where(lane == i1, neg, le))
    e2 = jnp.exp(v2 - v1)
    w1 = p_grp / (1.0 + e2)
    w2 = p_grp * e2 / (1.0 + e2)
    rec = jnp.where(lane == 0, (i1 - N_GROUPS).astype(F32), 0.0)
    rec = jnp.where(lane == 1, (i2 - N_GROUPS).astype(F32), rec)
    rec = jnp.where(lane == 2, w1, rec)
    return jnp.where(lane == 3, w2, rec)


def _outproj_kernel(of_ref, ob_ref, g_ref, loc_ref, x_ref, mod_ref, gout_ref, hsum_ref,
                    wo_ref, g2_ref, wr_ref, br_ref, xo_ref, m_ref, route_ref):
    mod = mod_ref[0]
    o = of_ref[0] + ob_ref[0]
    o2 = o * o
    hi = o2.astype(BF16)
    lo = (o2 - hi.astype(F32)).astype(BF16)
    ms = (jnp.dot(hi, hsum_ref[...], preferred_element_type=F32)
          + jnp.dot(lo, hsum_ref[...], preferred_element_type=F32)) / GLA_DV
    y_gla = o * lax.rsqrt(ms + NORM_EPS) * gout_ref[...] * _silu(g_ref[0])
    wo = wo_ref[...]
    proj = _dot(y_gla, wo[0:256]) + _dot(loc_ref[0], wo[256:1024])
    x = x_ref[0] + mod[2:3] * proj
    xo_ref[0] = x
    m = _rms(x, g2_ref[...]) * (1.0 + mod[4:5]) + mod[3:4]
    m_ref[0] = m
    route_ref[0] = _route(_dot(m, wr_ref[...]) + br_ref[...])


def _outproj(o_f, o_b, qkvg, y_loc, x, mod, mod_row, p, tm):
    bsz, seq, d = x.shape
    row = (lambda b: b) if mod_row is None else (lambda b: mod_row)
    tok = lambda w, j=0: pl.BlockSpec((1, tm, w), lambda b, i: (b, i, j))
    full = lambda a: pl.BlockSpec(a.shape, lambda b, i: (0,) * a.ndim)
    consts = (p["g_out"], p["head_sum"], p["w_out"], p["g_norm2"], p["w_router"], p["b_router"])
    return pl.pallas_call(
        _outproj_kernel,
        grid=(bsz, seq // tm),
        in_specs=[tok(256), tok(256), tok(256, 2), tok(768), tok(d),
                  pl.BlockSpec((1, 6, d), lambda b, i: (row(b), 0, 0))] + [full(a) for a in consts],
        out_specs=[tok(d), tok(d), tok(ROUTE_W)],
        out_shape=[jax.ShapeDtypeStruct((bsz, seq, d), F32),
                   jax.ShapeDtypeStruct((bsz, seq, d), F32),
                   jax.ShapeDtypeStruct((bsz, seq, ROUTE_W), F32)],
        compiler_params=_cp(("parallel", "parallel")),
        name="outproj",
    )(o_f, o_b, qkvg, y_loc, x, mod, *consts)


RANK_ROWS = 512


def _rank_kernel(route_ref, rank_ref, cnt_ref, carry):
    i = pl.program_id(0)

    @pl.when(i == 0)
    def _():
        carry[...] = jnp.zeros_like(carry)

    rec = route_ref[...]
    tm = rec.shape[0]
    lane = lax.broadcasted_iota(jnp.int32, rec.shape, 1).astype(F32)
    oh0 = jnp.where(lane == rec[:, 0:1], 1.0, 0.0)
    oh1 = jnp.where(lane == rec[:, 1:2], 1.0, 0.0)
    ti = lax.broadcasted_iota(jnp.int32, (tm, tm), 0)
    tj = lax.broadcasted_iota(jnp.int32, (tm, tm), 1)
    before = jnp.where(tj < ti, 1.0, 0.0).astype(BF16)
    tot0 = jnp.sum(oh0, axis=0, keepdims=True)
    tot1 = jnp.sum(oh1, axis=0, keepdims=True)
    base = carry[...]
    pre0 = jnp.dot(before, oh0.astype(BF16), preferred_element_type=F32) + base
    pre1 = jnp.dot(before, oh1.astype(BF16), preferred_element_type=F32) + (base + tot0)
    r0 = jnp.sum(oh0 * pre0, axis=-1, keepdims=True)
    r1 = jnp.sum(oh1 * pre1, axis=-1, keepdims=True)
    lane_i = lax.broadcasted_iota(jnp.int32, rec.shape, 1)
    cols = jnp.where(lane_i == 0, r0, jnp.where(lane_i == 1, r1, 0.0))
    cols = jnp.where(lane_i == 2, rec[:, 0:1], jnp.where(lane_i == 3, rec[:, 1:2], cols))
    rank_ref[...] = cols.T[0:8, :].astype(jnp.int32)
    carry[...] = base + tot0 + tot1
    cnt_ref[...] = carry[...]


def _rank(route):
    n_tok = route.shape[0]
    tm = RANK_ROWS if n_tok % RANK_ROWS == 0 else 256
    return pl.pallas_call(
        _rank_kernel,
        grid=(n_tok // tm,),
        in_specs=[pl.BlockSpec((tm, ROUTE_W), lambda i: (i, 0))],
        out_specs=[pl.BlockSpec((8, tm), lambda i: (0, i)),
                   pl.BlockSpec((1, ROUTE_W), lambda i: (0, 0))],
        out_shape=[jax.ShapeDtypeStruct((8, n_tok), jnp.int32),
                   jax.ShapeDtypeStruct((1, ROUTE_W), F32)],
        scratch_shapes=[pltpu.VMEM((1, ROUTE_W), F32)],
        compiler_params=_cp(("arbitrary",)),
        name="rank",
    )(route)


def _plan(route):
    n_tok = route.shape[0]
    rank, cnt = _rank(route)
    counts = cnt[0, 0:N_EXPERTS].astype(jnp.int32)
    padded = (counts + MOE_BLOCK - 1) // MOE_BLOCK * MOE_BLOCK
    pad_end = jnp.cumsum(padded)
    pad_start = pad_end - padded
    onehot = rank[2:4][None] == jnp.arange(N_EXPERTS, dtype=jnp.int32)[:, None, None]
    dest = (jnp.sum(jnp.where(onehot, pad_start[:, None, None], 0), axis=0) + rank[0:2]).reshape(-1)
    n_blocks = 2 * n_tok // MOE_BLOCK + N_EXPERTS
    blk_start = jnp.arange(n_blocks, dtype=jnp.int32) * MOE_BLOCK
    nused = pad_end[-1] // MOE_BLOCK
    blk = jnp.minimum(jnp.arange(n_blocks, dtype=jnp.int32), nused - 1)
    blk_expert = jnp.sum((blk[:, None] * MOE_BLOCK >= pad_end[None, :]).astype(jnp.int32), axis=-1)
    fill_start = pad_start + counts
    fill_n = padded - counts
    return dict(dest=dest.astype(jnp.int32), blk=blk.astype(jnp.int32),
                blk_expert=jnp.minimum(blk_expert, N_EXPERTS - 1).astype(jnp.int32),
                nused=nused.astype(jnp.int32).reshape(1), fill_start=fill_start.astype(jnp.int32),
                fill_n=fill_n.astype(jnp.int32), n_blocks=n_blocks)


DISPATCH_ROWS = 256


def _dispatch_kernel(dest_ref, fstart_ref, fn_ref, nused_ref, *refs, tr, n_x, n_c):
    if n_c:
        mx_hbm, mc_hbm, xs_hbm, buf, zbuf, in_sem, out_sem, fsem = refs
    else:
        mx_hbm, xs_hbm, buf, zbuf, in_sem, out_sem, fsem = refs
        mc_hbm = None
    i = pl.program_id(0)
    nsteps = n_x + n_c
    nslot = 3

    def start_load(step):
        slot = step % nslot

        @pl.when(step < n_x)
        def _():
            rows = pl.ds(pl.multiple_of(step * tr, tr), tr)
            pltpu.make_async_copy(mx_hbm.at[rows], buf.at[slot], in_sem.at[slot]).start()

        if n_c:
            @pl.when(step >= n_x)
            def _():
                rows = pl.ds(pl.multiple_of((step - n_x) * tr, tr), tr)
                pltpu.make_async_copy(mc_hbm.at[rows], buf.at[slot], in_sem.at[slot]).start()

    def wait_load(step):
        slot = step % nslot
        pltpu.make_async_copy(mx_hbm.at[pl.ds(0, tr)], buf.at[slot], in_sem.at[slot]).wait()

    def row_copy(step, r, k):
        slot = step % nslot
        dst = dest_ref[k * (nsteps * tr) + step * tr + r]
        return pltpu.make_async_copy(buf.at[slot, pl.ds(r, 1)], xs_hbm.at[pl.ds(dst, 1)],
                                     out_sem.at[slot])

    def wait_scatter(step):
        slot = step % nslot
        for _ in range(2):
            pltpu.make_async_copy(buf.at[slot], xs_hbm.at[pl.ds(0, tr)], out_sem.at[slot]).wait()

    @pl.when(i == 0)
    def _():
        zbuf[...] = jnp.zeros_like(zbuf)
        start_load(i)

    @pl.when(i >= 2)
    def _():
        wait_scatter(i - 2)

    @pl.when(i + 1 < nsteps)
    def _():
        start_load(i + 1)

    wait_load(i)

    for r in range(tr):
        row_copy(i, r, 0).start()
        row_copy(i, r, 1).start(priority=1)

    @pl.when(i == nsteps - 1)
    def _():
        if nsteps >= 2:
            wait_scatter(i - 1)
        wait_scatter(i)

        def fill_row(e, r):
            return pltpu.make_async_copy(zbuf.at[pl.ds(0, 1)],
                                         xs_hbm.at[pl.ds(fstart_ref[e] + r, 1)], fsem.at[0])

        def fill_piece(e, head, j):
            start = pl.multiple_of(fstart_ref[e] + head + 8 * j, 8)
            return pltpu.make_async_copy(zbuf.at[pl.ds(0, 8)], xs_hbm.at[pl.ds(start, 8)],
                                         fsem.at[0])

        def per_expert(start_not_wait):
            def body(e, carry):
                head = jnp.minimum((8 - fstart_ref[e] % 8) % 8, fn_ref[e])
                pieces = (fn_ref[e] - head) // 8
                if start_not_wait:
                    lax.fori_loop(0, head, lambda r, c: (fill_row(e, r).start(), c)[1], 0)
                    lax.fori_loop(0, pieces, lambda j, c: (fill_piece(e, head, j).start(), c)[1], 0)
                else:
                    lax.fori_loop(0, head, lambda r, c: (fill_row(e, r).wait(), c)[1], 0)
                    lax.fori_loop(0, pieces, lambda j, c: (fill_piece(e, head, j).wait(), c)[1], 0)
                return carry
            return body

        lax.fori_loop(0, N_EXPERTS, per_expert(True), 0)
        lax.fori_loop(0, N_EXPERTS, per_expert(False), 0)

        def tail_copy(b):
            start = pl.multiple_of(b * MOE_BLOCK, MOE_BLOCK)
            return pltpu.make_async_copy(zbuf, xs_hbm.at[pl.ds(start, MOE_BLOCK)], fsem.at[0])

        nblk = xs_hbm.shape[0] // MOE_BLOCK
        lax.fori_loop(nused_ref[0], nblk, lambda b, c: (tail_copy(b).start(), c)[1], 0)
        lax.fori_loop(nused_ref[0], nblk, lambda b, c: (tail_copy(b).wait(), c)[1], 0)


def _dispatch(m_x, m_c, plan):
    n_lat, d = m_x.shape
    n_ctx = 0 if m_c is None else m_c.shape[0]
    tr = DISPATCH_ROWS
    while n_lat % tr or n_ctx % tr:
        tr //= 2
    slots = plan["n_blocks"] * MOE_BLOCK
    n_x, n_c = n_lat // tr, n_ctx // tr
    srcs = (m_x,) if m_c is None else (m_x, m_c)
    grid_spec = pltpu.PrefetchScalarGridSpec(
        num_scalar_prefetch=4,
        grid=(n_x + n_c,),
        in_specs=[pl.BlockSpec(memory_space=pl.ANY)] * len(srcs),
        out_specs=pl.BlockSpec(memory_space=pl.ANY),
        scratch_shapes=[pltpu.VMEM((3, tr, d), F32), pltpu.VMEM((MOE_BLOCK, d), F32),
                        pltpu.SemaphoreType.DMA((3,)), pltpu.SemaphoreType.DMA((3,)),
                        pltpu.SemaphoreType.DMA((1,))],
    )
    return pl.pallas_call(
        functools.partial(_dispatch_kernel, tr=tr, n_x=n_x, n_c=n_c),
        grid_spec=grid_spec,
        out_shape=jax.ShapeDtypeStruct((slots, d), F32),
        compiler_params=_cp(("arbitrary",)),
        name="dispatch",
    )(plan["dest"], plan["fill_start"], plan["fill_n"], plan["nused"], *srcs)


def _expert_kernel(blk_ref, be_ref, nused_ref, xs_ref, w1_ref, w2_ref, ys_ref, w1b, w2b):
    i = pl.program_id(0)
    used = i < nused_ref[0]
    fresh = (i == 0) | (be_ref[i] != be_ref[jnp.maximum(i - 1, 0)])

    @pl.when(used & fresh)
    def _():
        w1b[...] = w1_ref[0, 0].astype(BF16)
        w2b[...] = w2_ref[0, 0].astype(BF16)

    @pl.when(used)
    def _():
        h = _dot(xs_ref[...], w1b[...])
        act = _silu(h[:, :D_EXPERT]) * h[:, D_EXPERT:]
        ys_ref[...] = _dot(act, w2b[...])

    @pl.when(jnp.logical_not(used))
    def _():
        ys_ref[...] = jnp.zeros_like(ys_ref)


def _experts(xs, plan, w1, w2, layer):
    slots, d = xs.shape
    grid_spec = pltpu.PrefetchScalarGridSpec(
        num_scalar_prefetch=3,
        grid=(plan["n_blocks"],),
        in_specs=[pl.BlockSpec((MOE_BLOCK, d), lambda i, blk, be, nu: (blk[i], 0)),
                  pl.BlockSpec((1, 1, d, 2 * D_EXPERT), lambda i, blk, be, nu: (layer, be[i], 0, 0)),
                  pl.BlockSpec((1, 1, D_EXPERT, d), lambda i, blk, be, nu: (layer, be[i], 0, 0))],
        out_specs=pl.BlockSpec((MOE_BLOCK, d), lambda i, blk, be, nu: (i, 0)),
        scratch_shapes=[pltpu.VMEM((d, 2 * D_EXPERT), BF16), pltpu.VMEM((D_EXPERT, d), BF16)],
    )
    return pl.pallas_call(
        _expert_kernel,
        grid_spec=grid_spec,
        out_shape=jax.ShapeDtypeStruct((slots, d), F32),
        compiler_params=_cp(("arbitrary",)),
        name="experts",
    )(plan["blk"], plan["blk_expert"], plan["nused"], xs, w1, w2)


COMBINE_ROWS = 256


def _combine_kernel(dest_ref, x_ref, route_ref, mod_ref, gf_ref, ys_hbm, o_ref, y0buf, y1buf, sem,
                    *, final, tok_off):
    i = pl.program_id(0)
    nsteps = pl.num_programs(0)
    tc = x_ref.shape[1]

    def row_copy(step, r, k, slot):
        a = k * (dest_ref.shape[0] // 2) + tok_off + step * tc + r
        buf = y0buf if k == 0 else y1buf
        return pltpu.make_async_copy(ys_hbm.at[pl.ds(dest_ref[a], 1)], buf.at[slot, pl.ds(r, 1)],
                                     sem.at[slot])

    def issue_step(step):
        for r in range(tc):
            row_copy(step, r, 0, step % 2).start()
            row_copy(step, r, 1, step % 2).start()

    @pl.when(i == 0)
    def _():
        issue_step(0)

    @pl.when(i + 1 < nsteps)
    def _():
        issue_step(i + 1)

    slot = i % 2
    pltpu.make_async_copy(ys_hbm.at[pl.ds(0, tc)], y0buf.at[slot], sem.at[slot]).wait()
    pltpu.make_async_copy(ys_hbm.at[pl.ds(0, tc)], y1buf.at[slot], sem.at[slot]).wait()
    rec = route_ref[...]
    y = rec[:, 2:3] * y0buf[slot] + rec[:, 3:4] * y1buf[slot]
    x = x_ref[0] + mod_ref[0][5:6] * y
    o_ref[0] = _rms(x, gf_ref[...]) if final else x


def _combine(x, ys, route, plan, tok_off, mod, mod_row, g_final, final):
    bsz, seq, d = x.shape
    tc = min(COMBINE_ROWS, seq)
    nt = seq // tc
    assert seq % tc == 0 and tok_off % tc == 0
    row = (lambda i: i // nt) if mod_row is None else (lambda i: mod_row)
    grid_spec = pltpu.PrefetchScalarGridSpec(
        num_scalar_prefetch=1,
        grid=(bsz * nt,),
        in_specs=[pl.BlockSpec((1, tc, d), lambda i, de: (i // nt, i % nt, 0)),
                  pl.BlockSpec((tc, ROUTE_W), lambda i, de: (tok_off // tc + i, 0)),
                  pl.BlockSpec((1, 6, d), lambda i, de: (row(i), 0, 0)),
                  pl.BlockSpec((1, d), lambda i, de: (0, 0)),
                  pl.BlockSpec(memory_space=pl.ANY)],
        out_specs=pl.BlockSpec((1, tc, d), lambda i, de: (i // nt, i % nt, 0)),
        scratch_shapes=[pltpu.VMEM((2, tc, d), F32), pltpu.VMEM((2, tc, d), F32),
                        pltpu.SemaphoreType.DMA((2,))],
    )
    return pl.pallas_call(
        functools.partial(_combine_kernel, final=final, tok_off=tok_off),
        grid_spec=grid_spec,
        out_shape=jax.ShapeDtypeStruct((bsz, seq, d), F32),
        compiler_params=_cp(("arbitrary",)),
        name="combine",
    )(plan["dest"], x, route, mod, g_final, ys)


def _layer_params(l, w_in, gla_w_gate_f, gla_b_gate_f, gla_w_gate_b, gla_b_gate_b, gla_g_out,
                  conf_w_dw, conf_b_dw, conf_ln_g, conf_ln_b, sc_w_dw, pool_w, pool_scale, w_out,
                  router_w_group, router_b_group, router_w_expert, router_b_expert,
                  expert_w_in, expert_w_out, g_norm1, g_norm2):
    d = D_MODEL
    idx = np.cumsum((0,) + IN_SPLITS)
    q, k, v, g, zf, zb, conf, scb, scc, scx, pool = [w_in[l][:, idx[j]:idx[j + 1]] for j in range(11)]
    pad = jnp.zeros((d, N_IN_PAD - idx[-1]), F32)
    w_in_p = jnp.concatenate([q, k, v, g, conf, scb, scc, scx, pool, zf, zb, pad], axis=1).astype(BF16)
    r = GLA_GATE_RANK
    w_gate = jnp.zeros((LANE, 2 * GLA_QK_W), F32)
    w_gate = w_gate.at[0:r, 0:GLA_QK_W].set(gla_w_gate_f[l]).at[r:2 * r, GLA_QK_W:].set(gla_w_gate_b[l])
    b_gate = jnp.concatenate([gla_b_gate_f[l], gla_b_gate_b[l]])[None]
    pool_bd = jnp.zeros((GROUP_W, GROUP_W), F32)
    for gi in range(len(POOL_WINDOWS)):
        s = slice(gi * POOL_GW, (gi + 1) * POOL_GW)
        pool_bd = pool_bd.at[s, s].set(pool_w[l, gi])
    head = np.arange(GROUP_W) // GLA_DV
    head_sum = jnp.asarray(head[:, None] == head[None, :], BF16)
    w_router = jnp.zeros((d, ROUTE_W), F32)
    w_router = w_router.at[:, 0:N_GROUPS].set(router_w_group[l])
    w_router = w_router.at[:, N_GROUPS:N_GROUPS + N_EXPERTS].set(router_w_expert[l])
    b_router = jnp.zeros((1, ROUTE_W), F32)
    b_router = b_router.at[0, 0:N_GROUPS].set(router_b_group[l])
    b_router = b_router.at[0, N_GROUPS:N_GROUPS + N_EXPERTS].set(router_b_expert[l])
    return dict(
        w_in=w_in_p, w_gate=w_gate.astype(BF16), b_gate=b_gate,
        g_norm1=g_norm1[l][None], g_norm2=g_norm2[l][None],
        g_out=gla_g_out[l][None], head_sum=head_sum,
        conf_w=conf_w_dw[l], conf_b=conf_b_dw[l][None], conf_g=conf_ln_g[l][None],
        conf_beta=conf_ln_b[l][None], sc_w=sc_w_dw[l], pool_bd=pool_bd.astype(BF16),
        pool_scale=pool_scale[l][None], w_out=w_out[l].astype(BF16),
        w_router=w_router.astype(BF16), b_router=b_router)


def kernel(x, c, ctx, c_ctx, w_mod, b_mod, g_norm1, g_norm2, w_in, gla_w_gate_f, gla_b_gate_f,
           gla_w_gate_b, gla_b_gate_b, gla_g_out, conf_w_dw, conf_b_dw, conf_ln_g, conf_ln_b,
           sc_w_dw, pool_w, pool_scale, w_out, router_w_group, router_b_group, router_w_expert,
           router_b_expert, expert_w_in, expert_w_out, g_final):
    bsz, seq, d = x.shape
    ctx_len = ctx.shape[1]
    depth = w_mod.shape[0]
    n_lat = bsz * seq
    n_ctx = bsz * ctx_len
    assert d == D_MODEL and seq % 1024 == 0 and ctx_len % GLA_GROUP == 0 and bsz <= 7
    tm = 1024
    tm_c = 256

    cv = jnp.zeros((8, d), F32).at[0:bsz].set(c).at[bsz].set(c_ctx)
    mods = _adaln(cv, w_mod, b_mod).reshape(depth, 8, 6, d)
    gf = g_final[None]
    zero_state = jnp.zeros((bsz, 2, GLA_QK_W, GROUP_W), F32)

    for l in range(depth):
        last = l == depth - 1
        p = _layer_params(l, w_in, gla_w_gate_f, gla_b_gate_f, gla_w_gate_b, gla_b_gate_b,
                          gla_g_out, conf_w_dw, conf_b_dw, conf_ln_g, conf_ln_b, sc_w_dw, pool_w,
                          pool_scale, w_out, router_w_group, router_b_group, router_w_expert,
                          router_b_expert, expert_w_in, expert_w_out, g_norm1, g_norm2)
        mod = mods[l]

        c_qkvg, c_la, c_u, c_sc, c_pool = _inproj(ctx, mod, bsz, p["g_norm1"], p["w_in"],
                                                  p["w_gate"], p["b_gate"], tm_c)
        c_of, c_ob, s_ctx = _gla(c_qkvg, c_la, zero_state, ctx_len)
        x_qkvg, x_la, x_u, x_sc, x_pool = _inproj(x, mod, None, p["g_norm1"], p["w_in"],
                                                  p["w_gate"], p["b_gate"], tm)
        x_of, x_ob, _ = _gla(x_qkvg, x_la, s_ctx, min(1024, seq))
        x_loc = _local_axial(x_u, x_sc, x_pool, p, min(2048, seq))
        x, m_x, r_x = _outproj(x_of, x_ob, x_qkvg, x_loc, x, mod, None, p, tm)

        if last:
            m_c = None
            route = r_x.reshape(n_lat, ROUTE_W)
        else:
            c_loc = _local_seq(c_u, c_sc, c_pool, p)
            ctx, m_c, r_c = _outproj(c_of, c_ob, c_qkvg, c_loc, ctx, mod, bsz, p, tm_c)
            m_c = m_c.reshape(n_ctx, d)
            route = jnp.concatenate([r_x.reshape(n_lat, ROUTE_W), r_c.reshape(n_ctx, ROUTE_W)], axis=0)

        plan = _plan(route)
        xs = _dispatch(m_x.reshape(n_lat, d), m_c, plan)
        ys = _experts(xs, plan, expert_w_in, expert_w_out, l)
        x = _combine(x, ys, route, plan, 0, mod, None, gf, last)
        if not last:
            ctx = _combine(ctx, ys, route, plan, n_lat, mod, bsz, gf, False)
    return x
```

```python
import functools

import numpy as np
import jax
import jax.numpy as jnp
from jax import lax
from jax.experimental import pallas as pl
from jax.experimental.pallas import tpu as pltpu

F32 = jnp.float32
BF16 = jnp.bfloat16

D_MODEL = 1024
GRID_W = 64
GROUP_W = 256
GLA_HEADS = 4
GLA_DV = 64
GLA_DK = 32
GLA_QK_W = 128
GLA_GATE_RANK = 16
GLA_GATE_NORM = 16.0
GLA_CHUNK = 64
GLA_GROUP = 256
CONF_WIDTH = 31
SC_WIDTH = 3
POOL_WINDOWS = (2, 4, 8, 16)
POOL_GW = 64
N_GROUPS = 4
EXPERTS_PER_GROUP = 8
N_EXPERTS = 32
D_EXPERT = 512
MOE_BLOCK = 512
NORM_EPS = 1e-6
IN_SPLITS = (128, 128, 256, 256, 16, 16, 512, 256, 256, 256, 256)
N_IN_PAD = 2432
LANE = 128
ROUTE_W = LANE

VMEM_LIMIT = 48 << 20


def _cp(sem, vmem=VMEM_LIMIT):
    return pltpu.CompilerParams(dimension_semantics=sem, vmem_limit_bytes=vmem)


def _sigmoid(x):
    return 1.0 / (1.0 + jnp.exp(-x))


def _silu(x):
    return x * _sigmoid(x)


def _log_sigmoid(x):
    return jnp.minimum(x, 0.0) - jnp.log1p(jnp.exp(-jnp.abs(x)))


def _dot(a, b):
    return jnp.dot(a.astype(BF16), b.astype(BF16), preferred_element_type=F32)


def _rms(x, g):
    return x * lax.rsqrt(jnp.mean(x * x, axis=-1, keepdims=True) + NORM_EPS) * g


def _adaln_kernel(cv_ref, w_ref, b_ref, o_ref):
    o_ref[0] = _dot(_silu(cv_ref[...]), w_ref[0]) + b_ref[0]


def _adaln(cv, w_mod, b_mod):
    depth, d, n = w_mod.shape
    tn = 1024
    return pl.pallas_call(
        _adaln_kernel,
        grid=(depth, n // tn),
        in_specs=[pl.BlockSpec((8, d), lambda l, j: (0, 0)),
                  pl.BlockSpec((1, d, tn), lambda l, j: (l, 0, j)),
                  pl.BlockSpec((1, 1, tn), lambda l, j: (l, 0, j))],
        out_specs=pl.BlockSpec((1, 8, tn), lambda l, j: (l, 0, j)),
        out_shape=jax.ShapeDtypeStruct((depth, 8, n), F32),
        compiler_params=_cp(("parallel", "parallel")),
        name="adaln",
    )(cv, w_mod, b_mod.reshape(depth, 1, n))


def _inproj_kernel(x_ref, mod_ref, g1_ref, w_ref, wg_ref, bg_ref,
                   qkvg_ref, la_ref, u_ref, sc_ref, pool_ref):
    x = x_ref[0]
    mod = mod_ref[0]
    m = _rms(x, g1_ref[...]) * (1.0 + mod[1:2]) + mod[0:1]
    h = _dot(m, w_ref[...])
    qkvg_ref[0] = h[:, 0:768]
    u_ref[0] = h[:, 768:1024] * _sigmoid(h[:, 1024:1280])
    sc_ref[0, :, 0:256] = h[:, 1280:1536]
    sc_ref[0, :, 256:512] = h[:, 1536:1792] * h[:, 1792:2048]
    pool_ref[0] = h[:, 2048:2304]
    z = _dot(h[:, 2304:2432], wg_ref[...]) + bg_ref[...]
    la_ref[0] = _log_sigmoid(z) / GLA_GATE_NORM


def _inproj(x, mod, mod_row, g1, w_in_p, w_gate, b_gate, tm):
    bsz, seq, d = x.shape
    row = (lambda b: b) if mod_row is None else (lambda b: mod_row)
    tok = lambda w: pl.BlockSpec((1, tm, w), lambda b, i: (b, i, 0))
    full = lambda a: pl.BlockSpec(a.shape, lambda b, i: (0,) * a.ndim)
    outs = (768, 256, 256, 512, 256)
    return pl.pallas_call(
        _inproj_kernel,
        grid=(bsz, seq // tm),
        in_specs=[tok(d),
                  pl.BlockSpec((1, 6, d), lambda b, i: (row(b), 0, 0)),
                  full(g1), full(w_in_p), full(w_gate), full(b_gate)],
        out_specs=[tok(w) for w in outs],
        out_shape=[jax.ShapeDtypeStruct((bsz, seq, w), F32) for w in outs],
        compiler_params=_cp(("parallel", "parallel")),
        name="inproj",
    )(x, mod, g1, w_in_p, w_gate, b_gate)


def _gla_direction(q, k, v, la, state, reverse):
    blk = q.shape[0]
    g = GLA_GROUP
    nchunk = g // GLA_CHUNK
    ti = lax.broadcasted_iota(jnp.int32, (g, g), 0)
    tj = lax.broadcasted_iota(jnp.int32, (g, g), 1)
    same = (ti // GLA_CHUNK) == (tj // GLA_CHUNK)
    cmask = same & ((tj >= ti) if reverse else (tj <= ti))
    tri = jnp.where(cmask, 1.0, 0.0).astype(BF16)
    qk_head = lax.broadcasted_iota(jnp.int32, (g, GLA_QK_W), 1) // GLA_DK
    v_head = lax.broadcasted_iota(jnp.int32, (g, GROUP_W), 1) // GLA_DV
    bd = (lax.broadcasted_iota(jnp.int32, (GLA_QK_W, GROUP_W), 0) // GLA_DK
          == lax.broadcasted_iota(jnp.int32, (GLA_QK_W, GROUP_W), 1) // GLA_DV)
    tok_chunk = lax.broadcasted_iota(jnp.int32, (GLA_QK_W, g), 1) // GLA_CHUNK
    scale = GLA_DK ** -0.5
    outs = [None] * (blk // g)
    groups = range(blk // g)
    for gi in (reversed(groups) if reverse else groups):
        sl = slice(gi * g, (gi + 1) * g)
        qg, kg, vg, lg = q[sl], k[sl], v[sl], la[sl]
        l0 = lg.astype(BF16)
        r0 = lg - l0.astype(F32)
        l1 = r0.astype(BF16)
        l2 = (r0 - l1.astype(F32)).astype(BF16)
        b3 = jnp.dot(tri, jnp.concatenate([l0, l1, l2], axis=1), preferred_element_type=F32)
        b = b3[:, 0:GLA_QK_W] + b3[:, GLA_QK_W:2 * GLA_QK_W] + b3[:, 2 * GLA_QK_W:3 * GLA_QK_W]
        last = 0 if reverse else GLA_CHUNK - 1
        blast = jnp.concatenate(
            [jnp.broadcast_to(b[c * GLA_CHUNK + last:c * GLA_CHUNK + last + 1], (GLA_CHUNK, GLA_QK_W))
             for c in range(nchunk)], axis=0)
        qd = (qg * scale) * jnp.exp(b)
        ki = (kg * jnp.exp(-b)).astype(BF16)
        kw_t = (kg * jnp.exp(blast - b)).T
        b_t = b.T
        vb = vg.astype(BF16)
        qs = jnp.concatenate([jnp.where(qk_head == h, qd, 0.0) for h in range(GLA_HEADS)],
                             axis=0).astype(BF16)
        s = lax.dot_general(qs, ki, (((1,), (1,)), ((), ())), preferred_element_type=F32)
        s = jnp.where(jnp.concatenate([cmask] * GLA_HEADS, axis=0), s, 0.0).astype(BF16)
        oh = jnp.dot(s, vb, preferred_element_type=F32)
        o = jnp.zeros((g, GROUP_W), F32)
        for h in range(GLA_HEADS):
            o = o + jnp.where(v_head == h, oh[h * g:(h + 1) * g], 0.0)
        kws = jnp.concatenate([jnp.where(tok_chunk == c, kw_t, 0.0) for c in range(nchunk)],
                              axis=0).astype(BF16)
        kv_all = jnp.dot(kws, vb, preferred_element_type=F32)
        qdb = qd.astype(BF16)
        inter = [None] * nchunk
        chunks = range(nchunk)
        for c in (reversed(chunks) if reverse else chunks):
            rows = slice(c * GLA_CHUNK, (c + 1) * GLA_CHUNK)
            inter[c] = jnp.dot(qdb[rows], state.astype(BF16), preferred_element_type=F32)
            col = c * GLA_CHUNK + last
            dec = jnp.exp(b_t[:, col:col + 1])
            kv = kv_all[c * GLA_QK_W:(c + 1) * GLA_QK_W]
            state = dec * state + jnp.where(bd, kv, 0.0)
        outs[gi] = o + jnp.concatenate(inter, axis=0)
    return jnp.concatenate(outs, axis=0) if len(outs) > 1 else outs[0], state


def _gla_kernel(qk_f, v_f, la_f, qk_b, v_b, la_b, s0_ref, of_ref, ob_ref, sfin_ref, st_ref):
    i = pl.program_id(1)

    @pl.when(i == 0)
    def _():
        st_ref[...] = s0_ref[0]

    qk = qk_f[0]
    o, s = _gla_direction(qk[:, 0:128], qk[:, 128:256], v_f[0], la_f[0], st_ref[0], False)
    of_ref[0] = o
    st_ref[0] = s
    qk = qk_b[0]
    o, s = _gla_direction(qk[:, 0:128], qk[:, 128:256], v_b[0], la_b[0], st_ref[1], True)
    ob_ref[0] = o
    st_ref[1] = s

    @pl.when(i == pl.num_programs(1) - 1)
    def _():
        sfin_ref[0] = st_ref[...]


def _gla(qkvg, la, s0, blk):
    bsz, seq, _ = qkvg.shape
    nb = seq // blk
    fwd = lambda j: pl.BlockSpec((1, blk, 256), lambda b, i: (b, i, j))
    bwd = lambda j: pl.BlockSpec((1, blk, 256), lambda b, i: (b, nb - 1 - i, j))
    la_f = pl.BlockSpec((1, blk, 128), lambda b, i: (b, i, 0))
    la_b = pl.BlockSpec((1, blk, 128), lambda b, i: (b, nb - 1 - i, 1))
    st = pl.BlockSpec((1, 2, GLA_QK_W, GROUP_W), lambda b, i: (b, 0, 0, 0))
    return pl.pallas_call(
        _gla_kernel,
        grid=(bsz, nb),
        in_specs=[fwd(0), fwd(1), la_f, bwd(0), bwd(1), la_b, st],
        out_specs=[pl.BlockSpec((1, blk, GROUP_W), lambda b, i: (b, i, 0)),
                   pl.BlockSpec((1, blk, GROUP_W), lambda b, i: (b, nb - 1 - i, 0)),
                   st],
        out_shape=[jax.ShapeDtypeStruct((bsz, seq, GROUP_W), F32),
                   jax.ShapeDtypeStruct((bsz, seq, GROUP_W), F32),
                   jax.ShapeDtypeStruct((bsz, 2, GLA_QK_W, GROUP_W), F32)],
        scratch_shapes=[pltpu.VMEM((2, GLA_QK_W, GROUP_W), F32)],
        compiler_params=_cp(("parallel", "arbitrary")),
        name="gla",
    )(qkvg, qkvg, la, qkvg, qkvg, la, s0)


def _row_shift(x, s, col, row_len):
    if s == 0:
        return x
    n = x.shape[0]
    y = pltpu.roll(x, (-s) % n, 0)
    valid = (col < row_len - s) if s > 0 else (col >= -s)
    return jnp.where(valid, y, 0.0)


def _row_conv(x, w, col, row_len):
    k = w.shape[0]
    acc = None
    for j in range(k):
        term = _row_shift(x, j - k // 2, col, row_len) * w[j:j + 1]
        acc = term if acc is None else acc + term
    return acc


def _col_conv(pad_ref, w, t0, n):
    k = w.shape[0]
    acc = None
    for j in range(k):
        start = pl.multiple_of(t0 + j * GRID_W, GRID_W)
        term = pad_ref[pl.ds(start, n), :] * w[j:j + 1]
        acc = term if acc is None else acc + term
    return acc


def _pool_diffs(x, col, row_len):
    colh = col[:, 0:LANE]
    colf = colh.astype(F32)
    first = lax.broadcasted_iota(jnp.int32, colh.shape, 1) < POOL_GW

    def count(win):
        return jnp.minimum(colf + win // 2, float(row_len)) - jnp.maximum(colf - win // 2, 0.0)

    means = []
    for hi in range(len(POOL_WINDOWS) // 2):
        wa, wb = POOL_WINDOWS[2 * hi], POOL_WINDOWS[2 * hi + 1]
        xh = x[:, hi * LANE:(hi + 1) * LANE]
        acc_a = acc_b = None
        for s in range(-(wb // 2), wb // 2):
            t = _row_shift(xh, s, colh, row_len)
            acc_b = t if acc_b is None else acc_b + t
            if -(wa // 2) <= s < wa // 2:
                acc_a = t if acc_a is None else acc_a + t
        means.append(jnp.where(first, acc_a / count(wa), acc_b / count(wb)))
    return jnp.concatenate(means, axis=-1) - x


def _local_tail(u, scb, conv_sc, pool, col, row_len, cb, lg, lb, pw, ps, out_ref):
    u = u + cb
    uc = u - jnp.mean(u, axis=-1, keepdims=True)
    ln = uc * lax.rsqrt(jnp.mean(uc * uc, axis=-1, keepdims=True) + NORM_EPS) * lg + lb
    out_ref[0, :, 0:256] = _silu(ln).astype(BF16)
    out_ref[0, :, 256:512] = (scb * conv_sc).astype(BF16)
    out_ref[0, :, 512:768] = (_dot(_pool_diffs(pool, col, row_len), pw) * ps).astype(BF16)


def _local_axial_kernel(uh_ref, uv_ref, scb_ref, sch_ref, scv_ref, pool_ref,
                        cw_ref, cb_ref, lg_ref, lb_ref, sw_ref, pw_ref, ps_ref,
                        out_ref, upad, spad):
    i = pl.program_id(1)
    blk = uh_ref.shape[1]
    seq = uv_ref.shape[1]
    pu = (CONF_WIDTH // 2) * GRID_W
    psc = (SC_WIDTH // 2) * GRID_W

    @pl.when(i == 0)
    def _():
        upad[0:pu, :] = jnp.zeros((pu, LANE), F32)
        upad[pu:pu + seq, :] = uv_ref[0]
        upad[pu + seq:pu + seq + pu, :] = jnp.zeros((pu, LANE), F32)
        spad[0:psc, :] = jnp.zeros((psc, LANE), F32)
        spad[psc:psc + seq, :] = scv_ref[0]
        spad[psc + seq:psc + seq + psc, :] = jnp.zeros((psc, LANE), F32)

    t0 = i * blk
    col = lax.broadcasted_iota(jnp.int32, (blk, LANE), 0) % GRID_W
    cw = cw_ref[...]
    sw = sw_ref[...]
    u = jnp.concatenate([_row_conv(uh_ref[0], cw[:, 0:128], col, GRID_W),
                         _col_conv(upad, cw[:, 128:256], t0, blk)], axis=-1)
    csc = jnp.concatenate([_row_conv(sch_ref[0], sw[:, 0:128], col, GRID_W),
                           _col_conv(spad, sw[:, 128:256], t0, blk)], axis=-1)
    col2 = jnp.concatenate([col, col], axis=-1)
    _local_tail(u, scb_ref[0], csc, pool_ref[0], col2, GRID_W, cb_ref[...], lg_ref[...],
                lb_ref[...], pw_ref[...], ps_ref[...], out_ref)


def _local_seq_kernel(u_ref, sc_ref, pool_ref, cw_ref, cb_ref, lg_ref, lb_ref, sw_ref,
                      pw_ref, ps_ref, out_ref):
    seq = u_ref.shape[1]
    col = lax.broadcasted_iota(jnp.int32, (seq, GROUP_W), 0)
    sc = sc_ref[0]
    u = _row_conv(u_ref[0], cw_ref[...], col, seq)
    csc = _row_conv(sc[:, 256:512], sw_ref[...], col, seq)
    _local_tail(u, sc[:, 0:256], csc, pool_ref[0], col, seq, cb_ref[...], lg_ref[...],
                lb_ref[...], pw_ref[...], ps_ref[...], out_ref)


def _local_params(p):
    return (p["conf_w"], p["conf_b"], p["conf_g"], p["conf_beta"], p["sc_w"], p["pool_bd"],
            p["pool_scale"])


def _local_axial(u, sc, pool, p, blk):
    bsz, seq, _ = u.shape
    cw, cb, lg, lb, sw, pw, ps = _local_params(p)
    blkspec = lambda w, j: pl.BlockSpec((1, blk, w), lambda b, i: (b, i, j))
    seqspec = lambda j: pl.BlockSpec((1, seq, LANE), lambda b, i: (b, 0, j))
    full = lambda a: pl.BlockSpec(a.shape, lambda b, i: (0,) * a.ndim)
    pu = (CONF_WIDTH // 2) * GRID_W
    psc = (SC_WIDTH // 2) * GRID_W
    return pl.pallas_call(
        _local_axial_kernel,
        grid=(bsz, seq // blk),
        in_specs=[blkspec(LANE, 0), seqspec(1), blkspec(256, 0), blkspec(LANE, 2), seqspec(3),
                  blkspec(256, 0), full(cw), full(cb), full(lg), full(lb), full(sw), full(pw),
                  full(ps)],
        out_specs=pl.BlockSpec((1, blk, 768), lambda b, i: (b, i, 0)),
        out_shape=jax.ShapeDtypeStruct((bsz, seq, 768), BF16),
        scratch_shapes=[pltpu.VMEM((seq + 2 * pu, LANE), F32),
                        pltpu.VMEM((seq + 2 * psc, LANE), F32)],
        compiler_params=_cp(("parallel", "arbitrary")),
        name="local_axial",
    )(u, u, sc, sc, sc, pool, cw, cb, lg, lb, sw, pw, ps)


def _local_seq(u, sc, pool, p):
    bsz, seq, _ = u.shape
    cw, cb, lg, lb, sw, pw, ps = _local_params(p)
    tok = lambda w: pl.BlockSpec((1, seq, w), lambda b: (b, 0, 0))
    full = lambda a: pl.BlockSpec(a.shape, lambda b: (0,) * a.ndim)
    return pl.pallas_call(
        _local_seq_kernel,
        grid=(bsz,),
        in_specs=[tok(256), tok(512), tok(256), full(cw), full(cb), full(lg), full(lb),
                  full(sw), full(pw), full(ps)],
        out_specs=tok(768),
        out_shape=jax.ShapeDtypeStruct((bsz, seq, 768), BF16),
        compiler_params=_cp(("parallel",)),
        name="local_seq",
    )(u, sc, pool, cw, cb, lg, lb, sw, pw, ps)


def _route(logits):
    lane = lax.broadcasted_iota(jnp.int32, logits.shape, 1)
    neg = -jnp.inf
    big = ROUTE_W

    def first_max(vals):
        mx = jnp.max(vals, axis=-1, keepdims=True)
        idx = jnp.min(jnp.where(vals == mx, lane, big), axis=-1, keepdims=True)
        return mx, idx

    lg = jnp.where(lane < N_GROUPS, logits, neg)
    gmx, grp = first_max(lg)
    p_grp = 1.0 / jnp.sum(jnp.exp(lg - gmx), axis=-1, keepdims=True)
    lo = N_GROUPS + grp * EXPERTS_PER_GROUP
    le = jnp.where((lane >= lo) & (lane < lo + EXPERTS_PER_GROUP), logits, neg)
    v1, i1 = first_max(le)
    v2, i2 = first_max(jnp.where(lane == i1, neg, le))
    e2 = jnp.exp(v2 - v1)
    w1 = p_grp / (1.0 + e2)
    w2 = p_grp * e2 / (1.0 + e2)
    rec = jnp.where(lane == 0, (i1 - N_GROUPS).astype(F32), 0.0)
    rec = jnp.where(lane == 1, (i2 - N_GROUPS).astype(F32), rec)
    rec = jnp.where(lane == 2, w1, rec)
    return jnp.where(lane == 3, w2, rec)


def _outproj_kernel(of_ref, ob_ref, g_ref, loc_ref, x_ref, mod_ref, gout_ref, hsum_ref,
                    wo_ref, g2_ref, wr_ref, br_ref, xo_ref, m_ref, route_ref):
    mod = mod_ref[0]
    o = of_ref[0] + ob_ref[0]
    o2 = o * o
    hi = o2.astype(BF16)
    lo = (o2 - hi.astype(F32)).astype(BF16)
    ms = (jnp.dot(hi, hsum_ref[...], preferred_element_type=F32)
          + jnp.dot(lo, hsum_ref[...], preferred_element_type=F32)) / GLA_DV
    y_gla = o * lax.rsqrt(ms + NORM_EPS) * gout_ref[...] * _silu(g_ref[0])
    wo = wo_ref[...]
    proj = _dot(y_gla, wo[0:256]) + _dot(loc_ref[0], wo[256:1024])
    x = x_ref[0] + mod[2:3] * proj
    xo_ref[0] = x
    m = _rms(x, g2_ref[...]) * (1.0 + mod[4:5]) + mod[3:4]
    m_ref[0] = m
    route_ref[0] = _route(_dot(m, wr_ref[...]) + br_ref[...])


def _outproj(o_f, o_b, qkvg, y_loc, x, mod, mod_row, p, tm):
    bsz, seq, d = x.shape
    row = (lambda b: b) if mod_row is None else (lambda b: mod_row)
    tok = lambda w, j=0: pl.BlockSpec((1, tm, w), lambda b, i: (b, i, j))
    full = lambda a: pl.BlockSpec(a.shape, lambda b, i: (0,) * a.ndim)
    consts = (p["g_out"], p["head_sum"], p["w_out"], p["g_norm2"], p["w_router"], p["b_router"])
    return pl.pallas_call(
        _outproj_kernel,
        grid=(bsz, seq // tm),
        in_specs=[tok(256), tok(256), tok(256, 2), tok(768), tok(d),
                  pl.BlockSpec((1, 6, d), lambda b, i: (row(b), 0, 0))] + [full(a) for a in consts],
        out_specs=[tok(d), tok(d), tok(ROUTE_W)],
        out_shape=[jax.ShapeDtypeStruct((bsz, seq, d), F32),
                   jax.ShapeDtypeStruct((bsz, seq, d), F32),
                   jax.ShapeDtypeStruct((bsz, seq, ROUTE_W), F32)],
        compiler_params=_cp(("parallel", "parallel")),
        name="outproj",
    )(o_f, o_b, qkvg, y_loc, x, mod, *consts)


RANK_ROWS = 512


def _rank_kernel(route_ref, rank_ref, cnt_ref, carry):
    i = pl.program_id(0)

    @pl.when(i == 0)
    def _():
        carry[...] = jnp.zeros_like(carry)

    rec = route_ref[...]
    tm = rec.shape[0]
    lane = lax.broadcasted_iota(jnp.int32, rec.shape, 1).astype(F32)
    oh0 = jnp.where(lane == rec[:, 0:1], 1.0, 0.0)
    oh1 = jnp.where(lane == rec[:, 1:2], 1.0, 0.0)
    ti = lax.broadcasted_iota(jnp.int32, (tm, tm), 0)
    tj = lax.broadcasted_iota(jnp.int32, (tm, tm), 1)
    before = jnp.where(tj < ti, 1.0, 0.0).astype(BF16)
    tot0 = jnp.sum(oh0, axis=0, keepdims=True)
    tot1 = jnp.sum(oh1, axis=0, keepdims=True)
    base = carry[...]
    pre0 = jnp.dot(before, oh0.astype(BF16), preferred_element_type=F32) + base
    pre1 = jnp.dot(before, oh1.astype(BF16), preferred_element_type=F32) + (base + tot0)
    r0 = jnp.sum(oh0 * pre0, axis=-1, keepdims=True)
    r1 = jnp.sum(oh1 * pre1, axis=-1, keepdims=True)
    lane_i = lax.broadcasted_iota(jnp.int32, rec.shape, 1)
    cols = jnp.where(lane_i == 0, r0, jnp.where(lane_i == 1, r1, 0.0))
    cols = jnp.where(lane_i == 2, rec[:, 0:1], jnp.where(lane_i == 3, rec[:, 1:2], cols))
    rank_ref[...] = cols.T[0:8, :].astype(jnp.int32)
    carry[...] = base + tot0 + tot1
    cnt_ref[...] = carry[...]


def _rank(route):
    n_tok = route.shape[0]
    tm = RANK_ROWS if n_tok % RANK_ROWS == 0 else 256
    return pl.pallas_call(
        _rank_kernel,
        grid=(n_tok // tm,),
        in_specs=[pl.BlockSpec((tm, ROUTE_W), lambda i: (i, 0))],
        out_specs=[pl.BlockSpec((8, tm), lambda i: (0, i)),
                   pl.BlockSpec((1, ROUTE_W), lambda i: (0, 0))],
        out_shape=[jax.ShapeDtypeStruct((8, n_tok), jnp.int32),
                   jax.ShapeDtypeStruct((1, ROUTE_W), F32)],
        scratch_shapes=[pltpu.VMEM((1, ROUTE_W), F32)],
        compiler_params=_cp(("arbitrary",)),
        name="rank",
    )(route)


def _plan(route):
    n_tok = route.shape[0]
    rank, cnt = _rank(route)
    counts = cnt[0, 0:N_EXPERTS].astype(jnp.int32)
    padded = (counts + MOE_BLOCK - 1) // MOE_BLOCK * MOE_BLOCK
    pad_end = jnp.cumsum(padded)
    pad_start = pad_end - padded
    onehot = rank[2:4][None] == jnp.arange(N_EXPERTS, dtype=jnp.int32)[:, None, None]
    dest = (jnp.sum(jnp.where(onehot, pad_start[:, None, None], 0), axis=0) + rank[0:2]).reshape(-1)
    n_blocks = 2 * n_tok // MOE_BLOCK + N_EXPERTS
    blk_start = jnp.arange(n_blocks, dtype=jnp.int32) * MOE_BLOCK
    nused = pad_end[-1] // MOE_BLOCK
    blk = jnp.minimum(jnp.arange(n_blocks, dtype=jnp.int32), nused - 1)
    blk_expert = jnp.sum((blk[:, None] * MOE_BLOCK >= pad_end[None, :]).astype(jnp.int32), axis=-1)
    fill_start = pad_start + counts
    fill_n = padded - counts
    ids = jnp.arange(N_EXPERTS, dtype=jnp.int32)
    later = (ids[None, :] > ids[:, None]) & (counts[None, :] > 0)
    next_expert = jnp.min(jnp.where(later, ids[None, :], N_EXPERTS), axis=1)
    next_expert = jnp.where(next_expert == N_EXPERTS, -1, next_expert).astype(jnp.int32)
    return dict(dest=dest.astype(jnp.int32), blk=blk.astype(jnp.int32), next_expert=next_expert,
                blk_expert=jnp.minimum(blk_expert, N_EXPERTS - 1).astype(jnp.int32),
                nused=nused.astype(jnp.int32).reshape(1), fill_start=fill_start.astype(jnp.int32),
                fill_n=fill_n.astype(jnp.int32), n_blocks=n_blocks)


DISPATCH_ROWS = 256


def _dispatch_kernel(dest_ref, fstart_ref, fn_ref, nused_ref, *refs, tr, n_x, n_c):
    if n_c:
        mx_hbm, mc_hbm, xs_hbm, buf, zbuf, in_sem, out_sem, fsem = refs
    else:
        mx_hbm, xs_hbm, buf, zbuf, in_sem, out_sem, fsem = refs
        mc_hbm = None
    i = pl.program_id(0)
    nsteps = n_x + n_c
    nslot = 3

    def start_load(step):
        slot = step % nslot

        @pl.when(step < n_x)
        def _():
            rows = pl.ds(pl.multiple_of(step * tr, tr), tr)
            pltpu.make_async_copy(mx_hbm.at[rows], buf.at[slot], in_sem.at[slot]).start()

        if n_c:
            @pl.when(step >= n_x)
            def _():
                rows = pl.ds(pl.multiple_of((step - n_x) * tr, tr), tr)
                pltpu.make_async_copy(mc_hbm.at[rows], buf.at[slot], in_sem.at[slot]).start()

    def wait_load(step):
        slot = step % nslot
        pltpu.make_async_copy(mx_hbm.at[pl.ds(0, tr)], buf.at[slot], in_sem.at[slot]).wait()

    def row_copy(step, r, k):
        slot = step % nslot
        dst = dest_ref[k * (nsteps * tr) + step * tr + r]
        return pltpu.make_async_copy(buf.at[slot, pl.ds(r, 1)], xs_hbm.at[pl.ds(dst, 1)],
                                     out_sem.at[slot])

    def wait_scatter(step):
        slot = step % nslot
        for _ in range(2):
            pltpu.make_async_copy(buf.at[slot], xs_hbm.at[pl.ds(0, tr)], out_sem.at[slot]).wait()

    @pl.when(i == 0)
    def _():
        zbuf[...] = jnp.zeros_like(zbuf)
        start_load(i)

    @pl.when(i >= 2)
    def _():
        wait_scatter(i - 2)

    @pl.when(i + 1 < nsteps)
    def _():
        start_load(i + 1)

    wait_load(i)

    for r in range(tr):
        row_copy(i, r, 0).start()
        row_copy(i, r, 1).start(priority=1)

    @pl.when(i == nsteps - 1)
    def _():
        if nsteps >= 2:
            wait_scatter(i - 1)
        wait_scatter(i)

        def fill_row(e, r):
            return pltpu.make_async_copy(zbuf.at[pl.ds(0, 1)],
                                         xs_hbm.at[pl.ds(fstart_ref[e] + r, 1)], fsem.at[0])

        def fill_piece(e, head, j):
            start = pl.multiple_of(fstart_ref[e] + head + 8 * j, 8)
            return pltpu.make_async_copy(zbuf.at[pl.ds(0, 8)], xs_hbm.at[pl.ds(start, 8)],
                                         fsem.at[0])

        def per_expert(start_not_wait):
            def body(e, carry):
                head = jnp.minimum((8 - fstart_ref[e] % 8) % 8, fn_ref[e])
                pieces = (fn_ref[e] - head) // 8
                if start_not_wait:
                    lax.fori_loop(0, head, lambda r, c: (fill_row(e, r).start(), c)[1], 0)
                    lax.fori_loop(0, pieces, lambda j, c: (fill_piece(e, head, j).start(), c)[1], 0)
                else:
                    lax.fori_loop(0, head, lambda r, c: (fill_row(e, r).wait(), c)[1], 0)
                    lax.fori_loop(0, pieces, lambda j, c: (fill_piece(e, head, j).wait(), c)[1], 0)
                return carry
            return body

        lax.fori_loop(0, N_EXPERTS, per_expert(True), 0)
        lax.fori_loop(0, N_EXPERTS, per_expert(False), 0)

        def tail_copy(b):
            start = pl.multiple_of(b * MOE_BLOCK, MOE_BLOCK)
            return pltpu.make_async_copy(zbuf, xs_hbm.at[pl.ds(start, MOE_BLOCK)], fsem.at[0])

        nblk = xs_hbm.shape[0] // MOE_BLOCK
        lax.fori_loop(nused_ref[0], nblk, lambda b, c: (tail_copy(b).start(), c)[1], 0)
        lax.fori_loop(nused_ref[0], nblk, lambda b, c: (tail_copy(b).wait(), c)[1], 0)


def _dispatch(m_x, m_c, plan):
    n_lat, d = m_x.shape
    n_ctx = 0 if m_c is None else m_c.shape[0]
    tr = DISPATCH_ROWS
    while n_lat % tr or n_ctx % tr:
        tr //= 2
    slots = plan["n_blocks"] * MOE_BLOCK
    n_x, n_c = n_lat // tr, n_ctx // tr
    srcs = (m_x,) if m_c is None else (m_x, m_c)
    grid_spec = pltpu.PrefetchScalarGridSpec(
        num_scalar_prefetch=4,
        grid=(n_x + n_c,),
        in_specs=[pl.BlockSpec(memory_space=pl.ANY)] * len(srcs),
        out_specs=pl.BlockSpec(memory_space=pl.ANY),
        scratch_shapes=[pltpu.VMEM((3, tr, d), F32), pltpu.VMEM((MOE_BLOCK, d), F32),
                        pltpu.SemaphoreType.DMA((3,)), pltpu.SemaphoreType.DMA((3,)),
                        pltpu.SemaphoreType.DMA((1,))],
    )
    return pl.pallas_call(
        functools.partial(_dispatch_kernel, tr=tr, n_x=n_x, n_c=n_c),
        grid_spec=grid_spec,
        out_shape=jax.ShapeDtypeStruct((slots, d), F32),
        compiler_params=_cp(("arbitrary",)),
        name="dispatch",
    )(plan["dest"], plan["fill_start"], plan["fill_n"], plan["nused"], *srcs)


def _expert_kernel(blk_ref, be_ref, nused_ref, next_ref, xs_ref, w1_hbm, w2_hbm, ys_ref,
                   w1f, w2f, w1b, w2b, wsem, *, layer):
    i = pl.program_id(0)
    used = i < nused_ref[0]
    e = be_ref[i]
    fresh = (i == 0) | (e != be_ref[jnp.maximum(i - 1, 0)])

    def weight_copies(ex):
        return (pltpu.make_async_copy(w1_hbm.at[layer, ex], w1f, wsem.at[0]),
                pltpu.make_async_copy(w2_hbm.at[layer, ex], w2f, wsem.at[1]))

    @pl.when(used & (i == 0))
    def _():
        for cp in weight_copies(e):
            cp.start()

    @pl.when(used & fresh)
    def _():
        for cp in weight_copies(e):
            cp.wait()
        w1b[...] = w1f[...].astype(BF16)
        w2b[...] = w2f[...].astype(BF16)
        nxt = next_ref[e]

        @pl.when(nxt >= 0)
        def _():
            for cp in weight_copies(nxt):
                cp.start()

    @pl.when(used)
    def _():
        h = _dot(xs_ref[...], w1b[...])
        act = _silu(h[:, :D_EXPERT]) * h[:, D_EXPERT:]
        ys_ref[...] = _dot(act, w2b[...])

    @pl.when(jnp.logical_not(used))
    def _():
        ys_ref[...] = jnp.zeros_like(ys_ref)


def _experts(xs, plan, w1, w2, layer):
    slots, d = xs.shape
    grid_spec = pltpu.PrefetchScalarGridSpec(
        num_scalar_prefetch=4,
        grid=(plan["n_blocks"],),
        in_specs=[pl.BlockSpec((MOE_BLOCK, d), lambda i, blk, *_: (blk[i], 0)),
                  pl.BlockSpec(memory_space=pl.ANY), pl.BlockSpec(memory_space=pl.ANY)],
        out_specs=pl.BlockSpec((MOE_BLOCK, d), lambda i, blk, *_: (i, 0)),
        scratch_shapes=[pltpu.VMEM((d, 2 * D_EXPERT), F32), pltpu.VMEM((D_EXPERT, d), F32),
                        pltpu.VMEM((d, 2 * D_EXPERT), BF16), pltpu.VMEM((D_EXPERT, d), BF16),
                        pltpu.SemaphoreType.DMA((2,))],
    )
    return pl.pallas_call(
        functools.partial(_expert_kernel, layer=layer),
        grid_spec=grid_spec,
        out_shape=jax.ShapeDtypeStruct((slots, d), F32),
        compiler_params=_cp(("arbitrary",)),
        name="experts",
    )(plan["blk"], plan["blk_expert"], plan["nused"], plan["next_expert"], xs, w1, w2)


COMBINE_ROWS = 256


def _combine_kernel(dest_ref, x_ref, route_ref, mod_ref, gf_ref, ys_hbm, o_ref, y0buf, y1buf, sem,
                    *, final, tok_off):
    i = pl.program_id(0)
    nsteps = pl.num_programs(0)
    tc = x_ref.shape[1]

    def row_copy(step, r, k, slot):
        a = k * (dest_ref.shape[0] // 2) + tok_off + step * tc + r
        buf = y0buf if k == 0 else y1buf
        return pltpu.make_async_copy(ys_hbm.at[pl.ds(dest_ref[a], 1)], buf.at[slot, pl.ds(r, 1)],
                                     sem.at[slot])

    def issue_step(step):
        for r in range(tc):
            row_copy(step, r, 0, step % 2).start()
            row_copy(step, r, 1, step % 2).start()

    @pl.when(i == 0)
    def _():
        issue_step(0)

    @pl.when(i + 1 < nsteps)
    def _():
        issue_step(i + 1)

    slot = i % 2
    pltpu.make_async_copy(ys_hbm.at[pl.ds(0, tc)], y0buf.at[slot], sem.at[slot]).wait()
    pltpu.make_async_copy(ys_hbm.at[pl.ds(0, tc)], y1buf.at[slot], sem.at[slot]).wait()
    rec = route_ref[...]
    y = rec[:, 2:3] * y0buf[slot] + rec[:, 3:4] * y1buf[slot]
    x = x_ref[0] + mod_ref[0][5:6] * y
    o_ref[0] = _rms(x, gf_ref[...]) if final else x


def _combine(x, ys, route, plan, tok_off, mod, mod_row, g_final, final):
    bsz, seq, d = x.shape
    tc = min(COMBINE_ROWS, seq)
    nt = seq // tc
    assert seq % tc == 0 and tok_off % tc == 0
    row = (lambda i: i // nt) if mod_row is None else (lambda i: mod_row)
    grid_spec = pltpu.PrefetchScalarGridSpec(
        num_scalar_prefetch=1,
        grid=(bsz * nt,),
        in_specs=[pl.BlockSpec((1, tc, d), lambda i, de: (i // nt, i % nt, 0)),
                  pl.BlockSpec((tc, ROUTE_W), lambda i, de: (tok_off // tc + i, 0)),
                  pl.BlockSpec((1, 6, d), lambda i, de: (row(i), 0, 0)),
                  pl.BlockSpec((1, d), lambda i, de: (0, 0)),
                  pl.BlockSpec(memory_space=pl.ANY)],
        out_specs=pl.BlockSpec((1, tc, d), lambda i, de: (i // nt, i % nt, 0)),
        scratch_shapes=[pltpu.VMEM((2, tc, d), F32), pltpu.VMEM((2, tc, d), F32),
                        pltpu.SemaphoreType.DMA((2,))],
    )
    return pl.pallas_call(
        functools.partial(_combine_kernel, final=final, tok_off=tok_off),
        grid_spec=grid_spec,
        out_shape=jax.ShapeDtypeStruct((bsz, seq, d), F32),
        compiler_params=_cp(("arbitrary",)),
        name="combine",
    )(plan["dest"], x, route, mod, g_final, ys)


def _layer_params(l, w_in, gla_w_gate_f, gla_b_gate_f, gla_w_gate_b, gla_b_gate_b, gla_g_out,
                  conf_w_dw, conf_b_dw, conf_ln_g, conf_ln_b, sc_w_dw, pool_w, pool_scale, w_out,
                  router_w_group, router_b_group, router_w_expert, router_b_expert,
                  expert_w_in, expert_w_out, g_norm1, g_norm2):
    d = D_MODEL
    idx = np.cumsum((0,) + IN_SPLITS)
    q, k, v, g, zf, zb, conf, scb, scc, scx, pool = [w_in[l][:, idx[j]:idx[j + 1]] for j in range(11)]
    pad = jnp.zeros((d, N_IN_PAD - idx[-1]), F32)
    w_in_p = jnp.concatenate([q, k, v, g, conf, scb, scc, scx, pool, zf, zb, pad], axis=1).astype(BF16)
    r = GLA_GATE_RANK
    w_gate = jnp.zeros((LANE, 2 * GLA_QK_W), F32)
    w_gate = w_gate.at[0:r, 0:GLA_QK_W].set(gla_w_gate_f[l]).at[r:2 * r, GLA_QK_W:].set(gla_w_gate_b[l])
    b_gate = jnp.concatenate([gla_b_gate_f[l], gla_b_gate_b[l]])[None]
    pool_bd = jnp.zeros((GROUP_W, GROUP_W), F32)
    for gi in range(len(POOL_WINDOWS)):
        s = slice(gi * POOL_GW, (gi + 1) * POOL_GW)
        pool_bd = pool_bd.at[s, s].set(pool_w[l, gi])
    head = np.arange(GROUP_W) // GLA_DV
    head_sum = jnp.asarray(head[:, None] == head[None, :], BF16)
    w_router = jnp.zeros((d, ROUTE_W), F32)
    w_router = w_router.at[:, 0:N_GROUPS].set(router_w_group[l])
    w_router = w_router.at[:, N_GROUPS:N_GROUPS + N_EXPERTS].set(router_w_expert[l])
    b_router = jnp.zeros((1, ROUTE_W), F32)
    b_router = b_router.at[0, 0:N_GROUPS].set(router_b_group[l])
    b_router = b_router.at[0, N_GROUPS:N_GROUPS + N_EXPERTS].set(router_b_expert[l])
    return dict(
        w_in=w_in_p, w_gate=w_gate.astype(BF16), b_gate=b_gate,
        g_norm1=g_norm1[l][None], g_norm2=g_norm2[l][None],
        g_out=gla_g_out[l][None], head_sum=head_sum,
        conf_w=conf_w_dw[l], conf_b=conf_b_dw[l][None], conf_g=conf_ln_g[l][None],
        conf_beta=conf_ln_b[l][None], sc_w=sc_w_dw[l], pool_bd=pool_bd.astype(BF16),
        pool_scale=pool_scale[l][None], w_out=w_out[l].astype(BF16),
        w_router=w_router.astype(BF16), b_router=b_router)


def kernel(x, c, ctx, c_ctx, w_mod, b_mod, g_norm1, g_norm2, w_in, gla_w_gate_f, gla_b_gate_f,
           gla_w_gate_b, gla_b_gate_b, gla_g_out, conf_w_dw, conf_b_dw, conf_ln_g, conf_ln_b,
           sc_w_dw, pool_w, pool_scale, w_out, router_w_group, router_b_group, router_w_expert,
           router_b_expert, expert_w_in, expert_w_out, g_final):
    bsz, seq, d = x.shape
    ctx_len = ctx.shape[1]
    depth = w_mod.shape[0]
    n_lat = bsz * seq
    n_ctx = bsz * ctx_len
    assert d == D_MODEL and seq % 1024 == 0 and ctx_len % GLA_GROUP == 0 and bsz <= 7
    tm = 1024
    tm_c = 256

    cv = jnp.zeros((8, d), F32).at[0:bsz].set(c).at[bsz].set(c_ctx)
    mods = _adaln(cv, w_mod, b_mod).reshape(depth, 8, 6, d)
    gf = g_final[None]
    zero_state = jnp.zeros((bsz, 2, GLA_QK_W, GROUP_W), F32)

    for l in range(depth):
        last = l == depth - 1
        p = _layer_params(l, w_in, gla_w_gate_f, gla_b_gate_f, gla_w_gate_b, gla_b_gate_b,
                          gla_g_out, conf_w_dw, conf_b_dw, conf_ln_g, conf_ln_b, sc_w_dw, pool_w,
                          pool_scale, w_out, router_w_group, router_b_group, router_w_expert,
                          router_b_expert, expert_w_in, expert_w_out, g_norm1, g_norm2)
        mod = mods[l]

        c_qkvg, c_la, c_u, c_sc, c_pool = _inproj(ctx, mod, bsz, p["g_norm1"], p["w_in"],
                                                  p["w_gate"], p["b_gate"], tm_c)
        c_of, c_ob, s_ctx = _gla(c_qkvg, c_la, zero_state, ctx_len)
        x_qkvg, x_la, x_u, x_sc, x_pool = _inproj(x, mod, None, p["g_norm1"], p["w_in"],
                                                  p["w_gate"], p["b_gate"], tm)
        x_of, x_ob, _ = _gla(x_qkvg, x_la, s_ctx, min(1024, seq))
        x_loc = _local_axial(x_u, x_sc, x_pool, p, min(2048, seq))
        x, m_x, r_x = _outproj(x_of, x_ob, x_qkvg, x_loc, x, mod, None, p, tm)

        if last:
            m_c = None
            route = r_x.reshape(n_lat, ROUTE_W)
        else:
            c_loc = _local_seq(c_u, c_sc, c_pool, p)
            ctx, m_c, r_c = _outproj(c_of, c_ob, c_qkvg, c_loc, ctx, mod, bsz, p, tm_c)
            m_c = m_c.reshape(n_ctx, d)
            route = jnp.concatenate([r_x.reshape(n_lat, ROUTE_W), r_c.reshape(n_ctx, ROUTE_W)], axis=0)

        plan = _plan(route)
        xs = _dispatch(m_x.reshape(n_lat, d), m_c, plan)
        ys = _experts(xs, plan, expert_w_in, expert_w_out, l)
        x = _combine(x, ys, route, plan, 0, mod, None, gf, last)
        if not last:
            ctx = _combine(ctx, ys, route, plan, n_lat, mod, bsz, gf, False)
    return x
```

```python
import functools

import numpy as np
import jax
import jax.numpy as jnp
from jax import lax
from jax.experimental import pallas as pl
from jax.experimental.pallas import tpu as pltpu

F32 = jnp.float32
BF16 = jnp.bfloat16

D_MODEL = 1024
GRID_W = 64
GROUP_W = 256
GLA_HEADS = 4
GLA_DV = 64
GLA_DK = 32
GLA_QK_W = 128
GLA_GATE_RANK = 16
GLA_GATE_NORM = 16.0
GLA_CHUNK = 64
GLA_GROUP = 256
CONF_WIDTH = 31
SC_WIDTH = 3
POOL_WINDOWS = (2, 4, 8, 16)
POOL_GW = 64
N_GROUPS = 4
EXPERTS_PER_GROUP = 8
N_EXPERTS = 32
D_EXPERT = 512
MOE_BLOCK = 512
NORM_EPS = 1e-6
IN_SPLITS = (128, 128, 256, 256, 16, 16, 512, 256, 256, 256, 256)
N_IN_PAD = 2432
LANE = 128
ROUTE_W = LANE

TOKEN_TILE = 1024
LOCAL_TILE = 2048

VMEM_LIMIT = 48 << 20


def _cp(sem, vmem=VMEM_LIMIT):
    return pltpu.CompilerParams(dimension_semantics=sem, vmem_limit_bytes=vmem)


def _sigmoid(x):
    return 1.0 / (1.0 + jnp.exp(-x))


def _silu(x):
    return x * _sigmoid(x)


def _log_sigmoid(x):
    return jnp.minimum(x, 0.0) - jnp.log1p(jnp.exp(-jnp.abs(x)))


def _dot(a, b):
    return jnp.dot(a.astype(BF16), b.astype(BF16), preferred_element_type=F32)


def _rms(x, g):
    return x * lax.rsqrt(jnp.mean(x * x, axis=-1, keepdims=True) + NORM_EPS) * g


def _adaln_kernel(cv_ref, w_ref, b_ref, o_ref):
    o_ref[0] = _dot(_silu(cv_ref[...]), w_ref[0]) + b_ref[0]


def _adaln(cv, w_mod, b_mod):
    depth, d, n = w_mod.shape
    tn = 1024
    return pl.pallas_call(
        _adaln_kernel,
        grid=(depth, n // tn),
        in_specs=[pl.BlockSpec((8, d), lambda l, j: (0, 0)),
                  pl.BlockSpec((1, d, tn), lambda l, j: (l, 0, j)),
                  pl.BlockSpec((1, 1, tn), lambda l, j: (l, 0, j))],
        out_specs=pl.BlockSpec((1, 8, tn), lambda l, j: (l, 0, j)),
        out_shape=jax.ShapeDtypeStruct((depth, 8, n), F32),
        compiler_params=_cp(("parallel", "parallel")),
        name="adaln",
    )(cv, w_mod, b_mod.reshape(depth, 1, n))


def _inproj_kernel(x_ref, mod_ref, g1_ref, w_ref, wg_ref, bg_ref,
                   qkvg_ref, la_ref, u_ref, sc_ref, pool_ref):
    x = x_ref[0]
    mod = mod_ref[0]
    m = _rms(x, g1_ref[...]) * (1.0 + mod[1:2]) + mod[0:1]
    h = _dot(m, w_ref[...])
    qkvg_ref[0] = h[:, 0:768]
    u_ref[0] = h[:, 768:1024] * _sigmoid(h[:, 1024:1280])
    sc_ref[0, :, 0:256] = h[:, 1280:1536]
    sc_ref[0, :, 256:512] = h[:, 1536:1792] * h[:, 1792:2048]
    pool_ref[0] = h[:, 2048:2304]
    z = _dot(h[:, 2304:2432], wg_ref[...]) + bg_ref[...]
    la_ref[0] = _log_sigmoid(z) / GLA_GATE_NORM


def _inproj(x, mod, mod_row, g1, w_in_p, w_gate, b_gate, tm):
    bsz, seq, d = x.shape
    row = (lambda b: b) if mod_row is None else (lambda b: mod_row)
    tok = lambda w: pl.BlockSpec((1, tm, w), lambda b, i: (b, i, 0))
    full = lambda a: pl.BlockSpec(a.shape, lambda b, i: (0,) * a.ndim)
    outs = (768, 256, 256, 512, 256)
    return pl.pallas_call(
        _inproj_kernel,
        grid=(bsz, seq // tm),
        in_specs=[tok(d),
                  pl.BlockSpec((1, 6, d), lambda b, i: (row(b), 0, 0)),
                  full(g1), full(w_in_p), full(w_gate), full(b_gate)],
        out_specs=[tok(w) for w in outs],
        out_shape=[jax.ShapeDtypeStruct((bsz, seq, w), F32) for w in outs],
        compiler_params=_cp(("parallel", "parallel")),
        name="inproj",
    )(x, mod, g1, w_in_p, w_gate, b_gate)


def _gla_direction(q, k, v, la, state, reverse):
    blk = q.shape[0]
    g = GLA_GROUP
    nchunk = g // GLA_CHUNK
    ti = lax.broadcasted_iota(jnp.int32, (g, g), 0)
    tj = lax.broadcasted_iota(jnp.int32, (g, g), 1)
    same = (ti // GLA_CHUNK) == (tj // GLA_CHUNK)
    cmask = same & ((tj >= ti) if reverse else (tj <= ti))
    tri = jnp.where(cmask, 1.0, 0.0).astype(BF16)
    qk_head = lax.broadcasted_iota(jnp.int32, (g, GLA_QK_W), 1) // GLA_DK
    v_head = lax.broadcasted_iota(jnp.int32, (g, GROUP_W), 1) // GLA_DV
    bd = (lax.broadcasted_iota(jnp.int32, (GLA_QK_W, GROUP_W), 0) // GLA_DK
          == lax.broadcasted_iota(jnp.int32, (GLA_QK_W, GROUP_W), 1) // GLA_DV)
    tok_chunk = lax.broadcasted_iota(jnp.int32, (GLA_QK_W, g), 1) // GLA_CHUNK
    scale = GLA_DK ** -0.5
    outs = [None] * (blk // g)
    groups = range(blk // g)
    for gi in (reversed(groups) if reverse else groups):
        sl = slice(gi * g, (gi + 1) * g)
        qg, kg, vg, lg = q[sl], k[sl], v[sl], la[sl]
        l0 = lg.astype(BF16)
        r0 = lg - l0.astype(F32)
        l1 = r0.astype(BF16)
        l2 = (r0 - l1.astype(F32)).astype(BF16)
        b3 = jnp.dot(tri, jnp.concatenate([l0, l1, l2], axis=1), preferred_element_type=F32)
        b = b3[:, 0:GLA_QK_W] + b3[:, GLA_QK_W:2 * GLA_QK_W] + b3[:, 2 * GLA_QK_W:3 * GLA_QK_W]
        last = 0 if reverse else GLA_CHUNK - 1
        blast = jnp.concatenate(
            [jnp.broadcast_to(b[c * GLA_CHUNK + last:c * GLA_CHUNK + last + 1], (GLA_CHUNK, GLA_QK_W))
             for c in range(nchunk)], axis=0)
        qd = (qg * scale) * jnp.exp(b)
        ki = (kg * jnp.exp(-b)).astype(BF16)
        kw_t = (kg * jnp.exp(blast - b)).T
        b_t = b.T
        vb = vg.astype(BF16)
        qs = jnp.concatenate([jnp.where(qk_head == h, qd, 0.0) for h in range(GLA_HEADS)],
                             axis=0).astype(BF16)
        s = lax.dot_general(qs, ki, (((1,), (1,)), ((), ())), preferred_element_type=F32)
        s = jnp.where(jnp.concatenate([cmask] * GLA_HEADS, axis=0), s, 0.0).astype(BF16)
        oh = jnp.dot(s, vb, preferred_element_type=F32)
        o = jnp.zeros((g, GROUP_W), F32)
        for h in range(GLA_HEADS):
            o = o + jnp.where(v_head == h, oh[h * g:(h + 1) * g], 0.0)
        kws = jnp.concatenate([jnp.where(tok_chunk == c, kw_t, 0.0) for c in range(nchunk)],
                              axis=0).astype(BF16)
        kv_all = jnp.dot(kws, vb, preferred_element_type=F32)
        qdb = qd.astype(BF16)
        inter = [None] * nchunk
        chunks = range(nchunk)
        for c in (reversed(chunks) if reverse else chunks):
            rows = slice(c * GLA_CHUNK, (c + 1) * GLA_CHUNK)
            inter[c] = jnp.dot(qdb[rows], state.astype(BF16), preferred_element_type=F32)
            col = c * GLA_CHUNK + last
            dec = jnp.exp(b_t[:, col:col + 1])
            kv = kv_all[c * GLA_QK_W:(c + 1) * GLA_QK_W]
            state = dec * state + jnp.where(bd, kv, 0.0)
        outs[gi] = o + jnp.concatenate(inter, axis=0)
    return jnp.concatenate(outs, axis=0) if len(outs) > 1 else outs[0], state


def _gla_kernel(qk_f, v_f, la_f, qk_b, v_b, la_b, s0_ref, of_ref, ob_ref, sfin_ref, st_ref):
    i = pl.program_id(1)

    @pl.when(i == 0)
    def _():
        st_ref[...] = s0_ref[0]

    qk = qk_f[0]
    o, s = _gla_direction(qk[:, 0:128], qk[:, 128:256], v_f[0], la_f[0], st_ref[0], False)
    of_ref[0] = o
    st_ref[0] = s
    qk = qk_b[0]
    o, s = _gla_direction(qk[:, 0:128], qk[:, 128:256], v_b[0], la_b[0], st_ref[1], True)
    ob_ref[0] = o
    st_ref[1] = s

    @pl.when(i == pl.num_programs(1) - 1)
    def _():
        sfin_ref[0] = st_ref[...]


def _gla(qkvg, la, s0, blk):
    bsz, seq, _ = qkvg.shape
    nb = seq // blk
    fwd = lambda j: pl.BlockSpec((1, blk, 256), lambda b, i: (b, i, j))
    bwd = lambda j: pl.BlockSpec((1, blk, 256), lambda b, i: (b, nb - 1 - i, j))
    la_f = pl.BlockSpec((1, blk, 128), lambda b, i: (b, i, 0))
    la_b = pl.BlockSpec((1, blk, 128), lambda b, i: (b, nb - 1 - i, 1))
    st = pl.BlockSpec((1, 2, GLA_QK_W, GROUP_W), lambda b, i: (b, 0, 0, 0))
    return pl.pallas_call(
        _gla_kernel,
        grid=(bsz, nb),
        in_specs=[fwd(0), fwd(1), la_f, bwd(0), bwd(1), la_b, st],
        out_specs=[pl.BlockSpec((1, blk, GROUP_W), lambda b, i: (b, i, 0)),
                   pl.BlockSpec((1, blk, GROUP_W), lambda b, i: (b, nb - 1 - i, 0)),
                   st],
        out_shape=[jax.ShapeDtypeStruct((bsz, seq, GROUP_W), F32),
                   jax.ShapeDtypeStruct((bsz, seq, GROUP_W), F32),
                   jax.ShapeDtypeStruct((bsz, 2, GLA_QK_W, GROUP_W), F32)],
        scratch_shapes=[pltpu.VMEM((2, GLA_QK_W, GROUP_W), F32)],
        compiler_params=_cp(("parallel", "arbitrary")),
        name="gla",
    )(qkvg, qkvg, la, qkvg, qkvg, la, s0)


def _row_shift(x, s, col, row_len):
    if s == 0:
        return x
    n = x.shape[0]
    y = pltpu.roll(x, (-s) % n, 0)
    valid = (col < row_len - s) if s > 0 else (col >= -s)
    return jnp.where(valid, y, 0.0)


def _row_conv(x, w, col, row_len):
    k = w.shape[0]
    acc = None
    for j in range(k):
        term = _row_shift(x, j - k // 2, col, row_len) * w[j:j + 1]
        acc = term if acc is None else acc + term
    return acc


def _col_conv(pad_ref, w, t0, n):
    k = w.shape[0]
    acc = None
    for j in range(k):
        start = pl.multiple_of(t0 + j * GRID_W, GRID_W)
        term = pad_ref[pl.ds(start, n), :] * w[j:j + 1]
        acc = term if acc is None else acc + term
    return acc


def _pool_diffs(x, col, row_len):
    colh = col[:, 0:LANE]
    colf = colh.astype(F32)
    first = lax.broadcasted_iota(jnp.int32, colh.shape, 1) < POOL_GW

    def count(win):
        return jnp.minimum(colf + win // 2, float(row_len)) - jnp.maximum(colf - win // 2, 0.0)

    means = []
    for hi in range(len(POOL_WINDOWS) // 2):
        wa, wb = POOL_WINDOWS[2 * hi], POOL_WINDOWS[2 * hi + 1]
        xh = x[:, hi * LANE:(hi + 1) * LANE]
        acc_a = acc_b = None
        for s in range(-(wb // 2), wb // 2):
            t = _row_shift(xh, s, colh, row_len)
            acc_b = t if acc_b is None else acc_b + t
            if -(wa // 2) <= s < wa // 2:
                acc_a = t if acc_a is None else acc_a + t
        means.append(jnp.where(first, acc_a / count(wa), acc_b / count(wb)))
    return jnp.concatenate(means, axis=-1) - x


def _local_tail(u, scb, conv_sc, pool, col, row_len, cb, lg, lb, pw, ps, out_ref):
    u = u + cb
    uc = u - jnp.mean(u, axis=-1, keepdims=True)
    ln = uc * lax.rsqrt(jnp.mean(uc * uc, axis=-1, keepdims=True) + NORM_EPS) * lg + lb
    out_ref[0, :, 0:256] = _silu(ln).astype(BF16)
    out_ref[0, :, 256:512] = (scb * conv_sc).astype(BF16)
    out_ref[0, :, 512:768] = (_dot(_pool_diffs(pool, col, row_len), pw) * ps).astype(BF16)


def _local_axial_kernel(uh_ref, uv_ref, scb_ref, sch_ref, scv_ref, pool_ref,
                        cw_ref, cb_ref, lg_ref, lb_ref, sw_ref, pw_ref, ps_ref,
                        out_ref, upad, spad):
    i = pl.program_id(1)
    blk = uh_ref.shape[1]
    seq = uv_ref.shape[1]
    pu = (CONF_WIDTH // 2) * GRID_W
    psc = (SC_WIDTH // 2) * GRID_W

    @pl.when(i == 0)
    def _():
        upad[0:pu, :] = jnp.zeros((pu, LANE), F32)
        upad[pu:pu + seq, :] = uv_ref[0]
        upad[pu + seq:pu + seq + pu, :] = jnp.zeros((pu, LANE), F32)
        spad[0:psc, :] = jnp.zeros((psc, LANE), F32)
        spad[psc:psc + seq, :] = scv_ref[0]
        spad[psc + seq:psc + seq + psc, :] = jnp.zeros((psc, LANE), F32)

    t0 = i * blk
    col = lax.broadcasted_iota(jnp.int32, (blk, LANE), 0) % GRID_W
    cw = cw_ref[...]
    sw = sw_ref[...]
    u = jnp.concatenate([_row_conv(uh_ref[0], cw[:, 0:128], col, GRID_W),
                         _col_conv(upad, cw[:, 128:256], t0, blk)], axis=-1)
    csc = jnp.concatenate([_row_conv(sch_ref[0], sw[:, 0:128], col, GRID_W),
                           _col_conv(spad, sw[:, 128:256], t0, blk)], axis=-1)
    col2 = jnp.concatenate([col, col], axis=-1)
    _local_tail(u, scb_ref[0], csc, pool_ref[0], col2, GRID_W, cb_ref[...], lg_ref[...],
                lb_ref[...], pw_ref[...], ps_ref[...], out_ref)


def _local_seq_kernel(u_ref, sc_ref, pool_ref, cw_ref, cb_ref, lg_ref, lb_ref, sw_ref,
                      pw_ref, ps_ref, out_ref):
    seq = u_ref.shape[1]
    col = lax.broadcasted_iota(jnp.int32, (seq, GROUP_W), 0)
    sc = sc_ref[0]
    u = _row_conv(u_ref[0], cw_ref[...], col, seq)
    csc = _row_conv(sc[:, 256:512], sw_ref[...], col, seq)
    _local_tail(u, sc[:, 0:256], csc, pool_ref[0], col, seq, cb_ref[...], lg_ref[...],
                lb_ref[...], pw_ref[...], ps_ref[...], out_ref)


def _local_params(p):
    return (p["conf_w"], p["conf_b"], p["conf_g"], p["conf_beta"], p["sc_w"], p["pool_bd"],
            p["pool_scale"])


def _local_axial(u, sc, pool, p, blk):
    bsz, seq, _ = u.shape
    cw, cb, lg, lb, sw, pw, ps = _local_params(p)
    blkspec = lambda w, j: pl.BlockSpec((1, blk, w), lambda b, i: (b, i, j))
    seqspec = lambda j: pl.BlockSpec((1, seq, LANE), lambda b, i: (b, 0, j))
    full = lambda a: pl.BlockSpec(a.shape, lambda b, i: (0,) * a.ndim)
    pu = (CONF_WIDTH // 2) * GRID_W
    psc = (SC_WIDTH // 2) * GRID_W
    return pl.pallas_call(
        _local_axial_kernel,
        grid=(bsz, seq // blk),
        in_specs=[blkspec(LANE, 0), seqspec(1), blkspec(256, 0), blkspec(LANE, 2), seqspec(3),
                  blkspec(256, 0), full(cw), full(cb), full(lg), full(lb), full(sw), full(pw),
                  full(ps)],
        out_specs=pl.BlockSpec((1, blk, 768), lambda b, i: (b, i, 0)),
        out_shape=jax.ShapeDtypeStruct((bsz, seq, 768), BF16),
        scratch_shapes=[pltpu.VMEM((seq + 2 * pu, LANE), F32),
                        pltpu.VMEM((seq + 2 * psc, LANE), F32)],
        compiler_params=_cp(("parallel", "arbitrary")),
        name="local_axial",
    )(u, u, sc, sc, sc, pool, cw, cb, lg, lb, sw, pw, ps)


def _local_seq(u, sc, pool, p):
    bsz, seq, _ = u.shape
    cw, cb, lg, lb, sw, pw, ps = _local_params(p)
    tok = lambda w: pl.BlockSpec((1, seq, w), lambda b: (b, 0, 0))
    full = lambda a: pl.BlockSpec(a.shape, lambda b: (0,) * a.ndim)
    return pl.pallas_call(
        _local_seq_kernel,
        grid=(bsz,),
        in_specs=[tok(256), tok(512), tok(256), full(cw), full(cb), full(lg), full(lb),
                  full(sw), full(pw), full(ps)],
        out_specs=tok(768),
        out_shape=jax.ShapeDtypeStruct((bsz, seq, 768), BF16),
        compiler_params=_cp(("parallel",)),
        name="local_seq",
    )(u, sc, pool, cw, cb, lg, lb, sw, pw, ps)


def _route(logits):
    lane = lax.broadcasted_iota(jnp.int32, logits.shape, 1)
    neg = -jnp.inf
    big = ROUTE_W

    def first_max(vals):
        mx = jnp.max(vals, axis=-1, keepdims=True)
        idx = jnp.min(jnp.where(vals == mx, lane, big), axis=-1, keepdims=True)
        return mx, idx

    lg = jnp.where(lane < N_GROUPS, logits, neg)
    gmx, grp = first_max(lg)
    p_grp = 1.0 / jnp.sum(jnp.exp(lg - gmx), axis=-1, keepdims=True)
    lo = N_GROUPS + grp * EXPERTS_PER_GROUP
    le = jnp.where((lane >= lo) & (lane < lo + EXPERTS_PER_GROUP), logits, neg)
    v1, i1 = first_max(le)
    v2, i2 = first_max(jnp.where(lane == i1, neg, le))
    e2 = jnp.exp(v2 - v1)
    w1 = p_grp / (1.0 + e2)
    w2 = p_grp * e2 / (1.0 + e2)
    rec = jnp.where(lane == 0, (i1 - N_GROUPS).astype(F32), 0.0)
    rec = jnp.where(lane == 1, (i2 - N_GROUPS).astype(F32), rec)
    rec = jnp.where(lane == 2, w1, rec)
    return jnp.where(lane == 3, w2, rec)


def _outproj_kernel(of_ref, ob_ref, g_ref, loc_ref, x_ref, mod_ref, gout_ref, hsum_ref,
                    wo_ref, g2_ref, wr_ref, br_ref, xo_ref, m_ref, route_ref):
    mod = mod_ref[0]
    o = of_ref[0] + ob_ref[0]
    o2 = o * o
    hi = o2.astype(BF16)
    lo = (o2 - hi.astype(F32)).astype(BF16)
    ms = (jnp.dot(hi, hsum_ref[...], preferred_element_type=F32)
          + jnp.dot(lo, hsum_ref[...], preferred_element_type=F32)) / GLA_DV
    y_gla = o * lax.rsqrt(ms + NORM_EPS) * gout_ref[...] * _silu(g_ref[0])
    wo = wo_ref[...]
    proj = _dot(y_gla, wo[0:256]) + _dot(loc_ref[0], wo[256:1024])
    x = x_ref[0] + mod[2:3] * proj
    xo_ref[0] = x
    m = _rms(x, g2_ref[...]) * (1.0 + mod[4:5]) + mod[3:4]
    m_ref[0] = m
    route_ref[0] = _route(_dot(m, wr_ref[...]) + br_ref[...])


def _outproj(o_f, o_b, qkvg, y_loc, x, mod, mod_row, p, tm):
    bsz, seq, d = x.shape
    row = (lambda b: b) if mod_row is None else (lambda b: mod_row)
    tok = lambda w, j=0: pl.BlockSpec((1, tm, w), lambda b, i: (b, i, j))
    full = lambda a: pl.BlockSpec(a.shape, lambda b, i: (0,) * a.ndim)
    consts = (p["g_out"], p["head_sum"], p["w_out"], p["g_norm2"], p["w_router"], p["b_router"])
    return pl.pallas_call(
        _outproj_kernel,
        grid=(bsz, seq // tm),
        in_specs=[tok(256), tok(256), tok(256, 2), tok(768), tok(d),
                  pl.BlockSpec((1, 6, d), lambda b, i: (row(b), 0, 0))] + [full(a) for a in consts],
        out_specs=[tok(d), tok(d), tok(ROUTE_W)],
        out_shape=[jax.ShapeDtypeStruct((bsz, seq, d), F32),
                   jax.ShapeDtypeStruct((bsz, seq, d), F32),
                   jax.ShapeDtypeStruct((bsz, seq, ROUTE_W), F32)],
        compiler_params=_cp(("parallel", "parallel")),
        name="outproj",
    )(o_f, o_b, qkvg, y_loc, x, mod, *consts)


RANK_ROWS = 512


def _rank_kernel(route_ref, rank_ref, cnt_ref, carry):
    i = pl.program_id(0)

    @pl.when(i == 0)
    def _():
        carry[...] = jnp.zeros_like(carry)

    rec = route_ref[...]
    tm = rec.shape[0]
    lane = lax.broadcasted_iota(jnp.int32, rec.shape, 1).astype(F32)
    oh0 = jnp.where(lane == rec[:, 0:1], 1.0, 0.0)
    oh1 = jnp.where(lane == rec[:, 1:2], 1.0, 0.0)
    ti = lax.broadcasted_iota(jnp.int32, (tm, tm), 0)
    tj = lax.broadcasted_iota(jnp.int32, (tm, tm), 1)
    before = jnp.where(tj < ti, 1.0, 0.0).astype(BF16)
    tot0 = jnp.sum(oh0, axis=0, keepdims=True)
    tot1 = jnp.sum(oh1, axis=0, keepdims=True)
    base = carry[...]
    pre0 = jnp.dot(before, oh0.astype(BF16), preferred_element_type=F32) + base
    pre1 = jnp.dot(before, oh1.astype(BF16), preferred_element_type=F32) + (base + tot0)
    r0 = jnp.sum(oh0 * pre0, axis=-1, keepdims=True)
    r1 = jnp.sum(oh1 * pre1, axis=-1, keepdims=True)
    lane_i = lax.broadcasted_iota(jnp.int32, rec.shape, 1)
    cols = jnp.where(lane_i == 0, r0, jnp.where(lane_i == 1, r1, 0.0))
    cols = jnp.where(lane_i == 2, rec[:, 0:1], jnp.where(lane_i == 3, rec[:, 1:2], cols))
    rank_ref[...] = cols.T[0:8, :].astype(jnp.int32)
    carry[...] = base + tot0 + tot1
    cnt_ref[...] = carry[...]


def _rank(route):
    n_tok = route.shape[0]
    tm = RANK_ROWS if n_tok % RANK_ROWS == 0 else 256
    return pl.pallas_call(
        _rank_kernel,
        grid=(n_tok // tm,),
        in_specs=[pl.BlockSpec((tm, ROUTE_W), lambda i: (i, 0))],
        out_specs=[pl.BlockSpec((8, tm), lambda i: (0, i)),
                   pl.BlockSpec((1, ROUTE_W), lambda i: (0, 0))],
        out_shape=[jax.ShapeDtypeStruct((8, n_tok), jnp.int32),
                   jax.ShapeDtypeStruct((1, ROUTE_W), F32)],
        scratch_shapes=[pltpu.VMEM((1, ROUTE_W), F32)],
        compiler_params=_cp(("arbitrary",)),
        name="rank",
    )(route)


def _plan(route):
    n_tok = route.shape[0]
    rank, cnt = _rank(route)
    counts = cnt[0, 0:N_EXPERTS].astype(jnp.int32)
    padded = (counts + MOE_BLOCK - 1) // MOE_BLOCK * MOE_BLOCK
    pad_end = jnp.cumsum(padded)
    pad_start = pad_end - padded
    onehot = rank[2:4][None] == jnp.arange(N_EXPERTS, dtype=jnp.int32)[:, None, None]
    dest = (jnp.sum(jnp.where(onehot, pad_start[:, None, None], 0), axis=0) + rank[0:2]).reshape(-1)
    n_blocks = 2 * n_tok // MOE_BLOCK + N_EXPERTS
    blk_start = jnp.arange(n_blocks, dtype=jnp.int32) * MOE_BLOCK
    nused = pad_end[-1] // MOE_BLOCK
    blk = jnp.minimum(jnp.arange(n_blocks, dtype=jnp.int32), nused - 1)
    blk_expert = jnp.sum((blk[:, None] * MOE_BLOCK >= pad_end[None, :]).astype(jnp.int32), axis=-1)
    fill_start = pad_start + counts
    fill_n = padded - counts
    ids = jnp.arange(N_EXPERTS, dtype=jnp.int32)
    later = (ids[None, :] > ids[:, None]) & (counts[None, :] > 0)
    next_expert = jnp.min(jnp.where(later, ids[None, :], N_EXPERTS), axis=1)
    next_expert = jnp.where(next_expert == N_EXPERTS, -1, next_expert).astype(jnp.int32)
    return dict(dest=dest.astype(jnp.int32), blk=blk.astype(jnp.int32), next_expert=next_expert,
                blk_expert=jnp.minimum(blk_expert, N_EXPERTS - 1).astype(jnp.int32),
                nused=nused.astype(jnp.int32).reshape(1), fill_start=fill_start.astype(jnp.int32),
                fill_n=fill_n.astype(jnp.int32), n_blocks=n_blocks)


DISPATCH_ROWS = 256


def _dispatch_kernel(dest_ref, fstart_ref, fn_ref, nused_ref, *refs, tr, n_x, n_c):
    if n_c:
        mx_hbm, mc_hbm, xs_hbm, buf, zbuf, in_sem, out_sem, fsem = refs
    else:
        mx_hbm, xs_hbm, buf, zbuf, in_sem, out_sem, fsem = refs
        mc_hbm = None
    i = pl.program_id(0)
    nsteps = n_x + n_c
    nslot = 3

    def start_load(step):
        slot = step % nslot

        @pl.when(step < n_x)
        def _():
            rows = pl.ds(pl.multiple_of(step * tr, tr), tr)
            pltpu.make_async_copy(mx_hbm.at[rows], buf.at[slot], in_sem.at[slot]).start()

        if n_c:
            @pl.when(step >= n_x)
            def _():
                rows = pl.ds(pl.multiple_of((step - n_x) * tr, tr), tr)
                pltpu.make_async_copy(mc_hbm.at[rows], buf.at[slot], in_sem.at[slot]).start()

    def wait_load(step):
        slot = step % nslot
        pltpu.make_async_copy(mx_hbm.at[pl.ds(0, tr)], buf.at[slot], in_sem.at[slot]).wait()

    def row_copy(step, r, k):
        slot = step % nslot
        dst = dest_ref[k * (nsteps * tr) + step * tr + r]
        return pltpu.make_async_copy(buf.at[slot, pl.ds(r, 1)], xs_hbm.at[pl.ds(dst, 1)],
                                     out_sem.at[slot])

    def wait_scatter(step):
        slot = step % nslot
        for _ in range(2):
            pltpu.make_async_copy(buf.at[slot], xs_hbm.at[pl.ds(0, tr)], out_sem.at[slot]).wait()

    @pl.when(i == 0)
    def _():
        zbuf[...] = jnp.zeros_like(zbuf)
        start_load(i)

    @pl.when(i >= 2)
    def _():
        wait_scatter(i - 2)

    @pl.when(i + 1 < nsteps)
    def _():
        start_load(i + 1)

    wait_load(i)

    for r in range(tr):
        row_copy(i, r, 0).start()
        row_copy(i, r, 1).start(priority=1)

    @pl.when(i == nsteps - 1)
    def _():
        if nsteps >= 2:
            wait_scatter(i - 1)
        wait_scatter(i)

        def fill_row(e, r):
            return pltpu.make_async_copy(zbuf.at[pl.ds(0, 1)],
                                         xs_hbm.at[pl.ds(fstart_ref[e] + r, 1)], fsem.at[0])

        def fill_piece(e, head, j):
            start = pl.multiple_of(fstart_ref[e] + head + 8 * j, 8)
            return pltpu.make_async_copy(zbuf.at[pl.ds(0, 8)], xs_hbm.at[pl.ds(start, 8)],
                                         fsem.at[0])

        def per_expert(start_not_wait):
            def body(e, carry):
                head = jnp.minimum((8 - fstart_ref[e] % 8) % 8, fn_ref[e])
                pieces = (fn_ref[e] - head) // 8
                if start_not_wait:
                    lax.fori_loop(0, head, lambda r, c: (fill_row(e, r).start(), c)[1], 0)
                    lax.fori_loop(0, pieces, lambda j, c: (fill_piece(e, head, j).start(), c)[1], 0)
                else:
                    lax.fori_loop(0, head, lambda r, c: (fill_row(e, r).wait(), c)[1], 0)
                    lax.fori_loop(0, pieces, lambda j, c: (fill_piece(e, head, j).wait(), c)[1], 0)
                return carry
            return body

        lax.fori_loop(0, N_EXPERTS, per_expert(True), 0)
        lax.fori_loop(0, N_EXPERTS, per_expert(False), 0)

        def tail_copy(b):
            start = pl.multiple_of(b * MOE_BLOCK, MOE_BLOCK)
            return pltpu.make_async_copy(zbuf, xs_hbm.at[pl.ds(start, MOE_BLOCK)], fsem.at[0])

        nblk = xs_hbm.shape[0] // MOE_BLOCK
        lax.fori_loop(nused_ref[0], nblk, lambda b, c: (tail_copy(b).start(), c)[1], 0)
        lax.fori_loop(nused_ref[0], nblk, lambda b, c: (tail_copy(b).wait(), c)[1], 0)


def _dispatch(m_x, m_c, plan):
    n_lat, d = m_x.shape
    n_ctx = 0 if m_c is None else m_c.shape[0]
    tr = DISPATCH_ROWS
    while n_lat % tr or n_ctx % tr:
        tr //= 2
    slots = plan["n_blocks"] * MOE_BLOCK
    n_x, n_c = n_lat // tr, n_ctx // tr
    srcs = (m_x,) if m_c is None else (m_x, m_c)
    grid_spec = pltpu.PrefetchScalarGridSpec(
        num_scalar_prefetch=4,
        grid=(n_x + n_c,),
        in_specs=[pl.BlockSpec(memory_space=pl.ANY)] * len(srcs),
        out_specs=pl.BlockSpec(memory_space=pl.ANY),
        scratch_shapes=[pltpu.VMEM((3, tr, d), F32), pltpu.VMEM((MOE_BLOCK, d), F32),
                        pltpu.SemaphoreType.DMA((3,)), pltpu.SemaphoreType.DMA((3,)),
                        pltpu.SemaphoreType.DMA((1,))],
    )
    return pl.pallas_call(
        functools.partial(_dispatch_kernel, tr=tr, n_x=n_x, n_c=n_c),
        grid_spec=grid_spec,
        out_shape=jax.ShapeDtypeStruct((slots, d), F32),
        compiler_params=_cp(("arbitrary",)),
        name="dispatch",
    )(plan["dest"], plan["fill_start"], plan["fill_n"], plan["nused"], *srcs)


def _expert_kernel(blk_ref, be_ref, nused_ref, next_ref, xs_ref, w1_hbm, w2_hbm, ys_ref,
                   w1f, w2f, w1b, w2b, wsem, *, layer):
    i = pl.program_id(0)
    used = i < nused_ref[0]
    e = be_ref[i]
    fresh = (i == 0) | (e != be_ref[jnp.maximum(i - 1, 0)])

    def weight_copies(ex):
        return (pltpu.make_async_copy(w1_hbm.at[layer, ex], w1f, wsem.at[0]),
                pltpu.make_async_copy(w2_hbm.at[layer, ex], w2f, wsem.at[1]))

    @pl.when(used & (i == 0))
    def _():
        for cp in weight_copies(e):
            cp.start()

    @pl.when(used & fresh)
    def _():
        for cp in weight_copies(e):
            cp.wait()
        w1b[...] = w1f[...].astype(BF16)
        w2b[...] = w2f[...].astype(BF16)
        nxt = next_ref[e]

        @pl.when(nxt >= 0)
        def _():
            for cp in weight_copies(nxt):
                cp.start()

    @pl.when(used)
    def _():
        h = _dot(xs_ref[...], w1b[...])
        act = _silu(h[:, :D_EXPERT]) * h[:, D_EXPERT:]
        ys_ref[...] = _dot(act, w2b[...])

    @pl.when(jnp.logical_not(used))
    def _():
        ys_ref[...] = jnp.zeros_like(ys_ref)


def _experts(xs, plan, w1, w2, layer):
    slots, d = xs.shape
    grid_spec = pltpu.PrefetchScalarGridSpec(
        num_scalar_prefetch=4,
        grid=(plan["n_blocks"],),
        in_specs=[pl.BlockSpec((MOE_BLOCK, d), lambda i, blk, *_: (blk[i], 0)),
                  pl.BlockSpec(memory_space=pl.ANY), pl.BlockSpec(memory_space=pl.ANY)],
        out_specs=pl.BlockSpec((MOE_BLOCK, d), lambda i, blk, *_: (i, 0)),
        scratch_shapes=[pltpu.VMEM((d, 2 * D_EXPERT), F32), pltpu.VMEM((D_EXPERT, d), F32),
                        pltpu.VMEM((d, 2 * D_EXPERT), BF16), pltpu.VMEM((D_EXPERT, d), BF16),
                        pltpu.SemaphoreType.DMA((2,))],
    )
    return pl.pallas_call(
        functools.partial(_expert_kernel, layer=layer),
        grid_spec=grid_spec,
        out_shape=jax.ShapeDtypeStruct((slots, d), F32),
        compiler_params=_cp(("arbitrary",)),
        name="experts",
    )(plan["blk"], plan["blk_expert"], plan["nused"], plan["next_expert"], xs, w1, w2)


COMBINE_ROWS = 256


def _combine_kernel(dest_ref, x_ref, route_ref, mod_ref, gf_ref, ys_hbm, o_ref, y0buf, y1buf, sem,
                    *, final, tok_off):
    i = pl.program_id(0)
    nsteps = pl.num_programs(0)
    tc = x_ref.shape[1]

    def row_copy(step, r, k, slot):
        a = k * (dest_ref.shape[0] // 2) + tok_off + step * tc + r
        buf = y0buf if k == 0 else y1buf
        return pltpu.make_async_copy(ys_hbm.at[pl.ds(dest_ref[a], 1)], buf.at[slot, pl.ds(r, 1)],
                                     sem.at[slot])

    def issue_step(step):
        for r in range(tc):
            row_copy(step, r, 0, step % 2).start()
            row_copy(step, r, 1, step % 2).start()

    @pl.when(i == 0)
    def _():
        issue_step(0)

    @pl.when(i + 1 < nsteps)
    def _():
        issue_step(i + 1)

    slot = i % 2
    pltpu.make_async_copy(ys_hbm.at[pl.ds(0, tc)], y0buf.at[slot], sem.at[slot]).wait()
    pltpu.make_async_copy(ys_hbm.at[pl.ds(0, tc)], y1buf.at[slot], sem.at[slot]).wait()
    rec = route_ref[...]
    y = rec[:, 2:3] * y0buf[slot] + rec[:, 3:4] * y1buf[slot]
    x = x_ref[0] + mod_ref[0][5:6] * y
    o_ref[0] = _rms(x, gf_ref[...]) if final else x


def _combine(x, ys, route, plan, tok_off, mod, mod_row, g_final, final):
    bsz, seq, d = x.shape
    tc = min(COMBINE_ROWS, seq)
    nt = seq // tc
    assert seq % tc == 0 and tok_off % tc == 0
    row = (lambda i: i // nt) if mod_row is None else (lambda i: mod_row)
    grid_spec = pltpu.PrefetchScalarGridSpec(
        num_scalar_prefetch=1,
        grid=(bsz * nt,),
        in_specs=[pl.BlockSpec((1, tc, d), lambda i, de: (i // nt, i % nt, 0)),
                  pl.BlockSpec((tc, ROUTE_W), lambda i, de: (tok_off // tc + i, 0)),
                  pl.BlockSpec((1, 6, d), lambda i, de: (row(i), 0, 0)),
                  pl.BlockSpec((1, d), lambda i, de: (0, 0)),
                  pl.BlockSpec(memory_space=pl.ANY)],
        out_specs=pl.BlockSpec((1, tc, d), lambda i, de: (i // nt, i % nt, 0)),
        scratch_shapes=[pltpu.VMEM((2, tc, d), F32), pltpu.VMEM((2, tc, d), F32),
                        pltpu.SemaphoreType.DMA((2,))],
    )
    return pl.pallas_call(
        functools.partial(_combine_kernel, final=final, tok_off=tok_off),
        grid_spec=grid_spec,
        out_shape=jax.ShapeDtypeStruct((bsz, seq, d), F32),
        compiler_params=_cp(("arbitrary",)),
        name="combine",
    )(plan["dest"], x, route, mod, g_final, ys)


def _layer_params(l, w_in, gla_w_gate_f, gla_b_gate_f, gla_w_gate_b, gla_b_gate_b, gla_g_out,
                  conf_w_dw, conf_b_dw, conf_ln_g, conf_ln_b, sc_w_dw, pool_w, pool_scale, w_out,
                  router_w_group, router_b_group, router_w_expert, router_b_expert,
                  g_norm1, g_norm2):
    d = D_MODEL
    idx = np.cumsum((0,) + IN_SPLITS)
    q, k, v, g, zf, zb, conf, scb, scc, scx, pool = [w_in[l][:, idx[j]:idx[j + 1]] for j in range(11)]
    pad = jnp.zeros((d, N_IN_PAD - idx[-1]), F32)
    w_in_p = jnp.concatenate([q, k, v, g, conf, scb, scc, scx, pool, zf, zb, pad], axis=1).astype(BF16)
    r = GLA_GATE_RANK
    w_gate = jnp.zeros((LANE, 2 * GLA_QK_W), F32)
    w_gate = w_gate.at[0:r, 0:GLA_QK_W].set(gla_w_gate_f[l]).at[r:2 * r, GLA_QK_W:].set(gla_w_gate_b[l])
    b_gate = jnp.concatenate([gla_b_gate_f[l], gla_b_gate_b[l]])[None]
    pool_bd = jnp.zeros((GROUP_W, GROUP_W), F32)
    for gi in range(len(POOL_WINDOWS)):
        s = slice(gi * POOL_GW, (gi + 1) * POOL_GW)
        pool_bd = pool_bd.at[s, s].set(pool_w[l, gi])
    head = np.arange(GROUP_W) // GLA_DV
    head_sum = jnp.asarray(head[:, None] == head[None, :], BF16)
    w_router = jnp.zeros((d, ROUTE_W), F32)
    w_router = w_router.at[:, 0:N_GROUPS].set(router_w_group[l])
    w_router = w_router.at[:, N_GROUPS:N_GROUPS + N_EXPERTS].set(router_w_expert[l])
    b_router = jnp.zeros((1, ROUTE_W), F32)
    b_router = b_router.at[0, 0:N_GROUPS].set(router_b_group[l])
    b_router = b_router.at[0, N_GROUPS:N_GROUPS + N_EXPERTS].set(router_b_expert[l])
    return dict(
        w_in=w_in_p, w_gate=w_gate.astype(BF16), b_gate=b_gate,
        g_norm1=g_norm1[l][None], g_norm2=g_norm2[l][None],
        g_out=gla_g_out[l][None], head_sum=head_sum,
        conf_w=conf_w_dw[l], conf_b=conf_b_dw[l][None], conf_g=conf_ln_g[l][None],
        conf_beta=conf_ln_b[l][None], sc_w=sc_w_dw[l], pool_bd=pool_bd.astype(BF16),
        pool_scale=pool_scale[l][None], w_out=w_out[l].astype(BF16),
        w_router=w_router.astype(BF16), b_router=b_router)


def kernel(x, c, ctx, c_ctx, w_mod, b_mod, g_norm1, g_norm2, w_in, gla_w_gate_f, gla_b_gate_f,
           gla_w_gate_b, gla_b_gate_b, gla_g_out, conf_w_dw, conf_b_dw, conf_ln_g, conf_ln_b,
           sc_w_dw, pool_w, pool_scale, w_out, router_w_group, router_b_group, router_w_expert,
           router_b_expert, expert_w_in, expert_w_out, g_final):
    bsz, seq, d = x.shape
    ctx_len = ctx.shape[1]
    depth = w_mod.shape[0]
    n_lat = bsz * seq
    n_ctx = bsz * ctx_len
    assert d == D_MODEL and seq % TOKEN_TILE == 0 and ctx_len % GLA_GROUP == 0 and bsz <= 7
    tm = TOKEN_TILE
    tm_c = GLA_GROUP

    cv = jnp.zeros((8, d), F32).at[0:bsz].set(c).at[bsz].set(c_ctx)
    mods = _adaln(cv, w_mod, b_mod).reshape(depth, 8, 6, d)
    gf = g_final[None]
    zero_state = jnp.zeros((bsz, 2, GLA_QK_W, GROUP_W), F32)

    for l in range(depth):
        last = l == depth - 1
        p = _layer_params(l, w_in, gla_w_gate_f, gla_b_gate_f, gla_w_gate_b, gla_b_gate_b,
                          gla_g_out, conf_w_dw, conf_b_dw, conf_ln_g, conf_ln_b, sc_w_dw, pool_w,
                          pool_scale, w_out, router_w_group, router_b_group, router_w_expert,
                          router_b_expert, g_norm1, g_norm2)
        mod = mods[l]

        c_qkvg, c_la, c_u, c_sc, c_pool = _inproj(ctx, mod, bsz, p["g_norm1"], p["w_in"],
                                                  p["w_gate"], p["b_gate"], tm_c)
        c_of, c_ob, s_ctx = _gla(c_qkvg, c_la, zero_state, ctx_len)
        x_qkvg, x_la, x_u, x_sc, x_pool = _inproj(x, mod, None, p["g_norm1"], p["w_in"],
                                                  p["w_gate"], p["b_gate"], tm)
        x_of, x_ob, _ = _gla(x_qkvg, x_la, s_ctx, TOKEN_TILE)
        x_loc = _local_axial(x_u, x_sc, x_pool, p, min(LOCAL_TILE, seq))
        x, m_x, r_x = _outproj(x_of, x_ob, x_qkvg, x_loc, x, mod, None, p, tm)

        if last:
            m_c = None
            route = r_x.reshape(n_lat, ROUTE_W)
        else:
            c_loc = _local_seq(c_u, c_sc, c_pool, p)
            ctx, m_c, r_c = _outproj(c_of, c_ob, c_qkvg, c_loc, ctx, mod, bsz, p, tm_c)
            m_c = m_c.reshape(n_ctx, d)
            route = jnp.concatenate([r_x.reshape(n_lat, ROUTE_W), r_c.reshape(n_ctx, ROUTE_W)], axis=0)

        plan = _plan(route)
        xs = _dispatch(m_x.reshape(n_lat, d), m_c, plan)
        ys = _experts(xs, plan, expert_w_in, expert_w_out, l)
        x = _combine(x, ys, route, plan, 0, mod, None, gf, last)
        if not last:
            ctx = _combine(ctx, ys, route, plan, n_lat, mod, bsz, gf, False)
    return x
```

```python
import functools

import numpy as np
import jax
import jax.numpy as jnp
from jax import lax
from jax.experimental import pallas as pl
from jax.experimental.pallas import tpu as pltpu

F32 = jnp.float32
BF16 = jnp.bfloat16

D_MODEL = 1024
GRID_W = 64
GROUP_W = 256
GLA_HEADS = 4
GLA_DV = 64
GLA_DK = 32
GLA_QK_W = 128
GLA_GATE_RANK = 16
GLA_GATE_NORM = 16.0
GLA_CHUNK = 64
GLA_GROUP = 256
CONF_WIDTH = 31
SC_WIDTH = 3
POOL_WINDOWS = (2, 4, 8, 16)
POOL_GW = 64
N_GROUPS = 4
EXPERTS_PER_GROUP = 8
N_EXPERTS = 32
D_EXPERT = 512
MOE_BLOCK = 512
NORM_EPS = 1e-6
IN_SPLITS = (128, 128, 256, 256, 16, 16, 512, 256, 256, 256, 256)
N_IN_PAD = 2432
LANE = 128
ROUTE_W = LANE

TOKEN_TILE = 1024
LOCAL_TILE = 2048

VMEM_LIMIT = 48 << 20


def _cp(sem, vmem=VMEM_LIMIT):
    return pltpu.CompilerParams(dimension_semantics=sem, vmem_limit_bytes=vmem)


def _sigmoid(x):
    return 1.0 / (1.0 + jnp.exp(-x))


def _silu(x):
    return x * _sigmoid(x)


def _log_sigmoid(x):
    return jnp.minimum(x, 0.0) - jnp.log1p(jnp.exp(-jnp.abs(x)))


def _dot(a, b):
    return jnp.dot(a.astype(BF16), b.astype(BF16), preferred_element_type=F32)


def _rms(x, g):
    return x * lax.rsqrt(jnp.mean(x * x, axis=-1, keepdims=True) + NORM_EPS) * g


def _adaln_kernel(cv_ref, w_ref, b_ref, o_ref):
    o_ref[0] = _dot(_silu(cv_ref[...]), w_ref[0]) + b_ref[0]


def _adaln(cv, w_mod, b_mod):
    depth, d, n = w_mod.shape
    tn = 1024
    return pl.pallas_call(
        _adaln_kernel,
        grid=(depth, n // tn),
        in_specs=[pl.BlockSpec((8, d), lambda l, j: (0, 0)),
                  pl.BlockSpec((1, d, tn), lambda l, j: (l, 0, j)),
                  pl.BlockSpec((1, 1, tn), lambda l, j: (l, 0, j))],
        out_specs=pl.BlockSpec((1, 8, tn), lambda l, j: (l, 0, j)),
        out_shape=jax.ShapeDtypeStruct((depth, 8, n), F32),
        compiler_params=_cp(("parallel", "parallel")),
        name="adaln",
    )(cv, w_mod, b_mod.reshape(depth, 1, n))


def _inproj_kernel(x_ref, mod_ref, g1_ref, w_ref, wg_ref, bg_ref,
                   qkvg_ref, la_ref, u_ref, sc_ref, pool_ref):
    x = x_ref[0]
    mod = mod_ref[0]
    m = _rms(x, g1_ref[...]) * (1.0 + mod[1:2]) + mod[0:1]
    h = _dot(m, w_ref[...])
    qkvg_ref[0] = h[:, 0:768]
    u_ref[0] = h[:, 768:1024] * _sigmoid(h[:, 1024:1280])
    sc_ref[0, :, 0:256] = h[:, 1280:1536]
    sc_ref[0, :, 256:512] = h[:, 1536:1792] * h[:, 1792:2048]
    pool_ref[0] = h[:, 2048:2304]
    z = _dot(h[:, 2304:2432], wg_ref[...]) + bg_ref[...]
    la_ref[0] = _log_sigmoid(z) / GLA_GATE_NORM


def _inproj(x, mod, mod_row, g1, w_in_p, w_gate, b_gate, tm):
    bsz, seq, d = x.shape
    row = (lambda b: b) if mod_row is None else (lambda b: mod_row)
    tok = lambda w: pl.BlockSpec((1, tm, w), lambda b, i: (b, i, 0))
    full = lambda a: pl.BlockSpec(a.shape, lambda b, i: (0,) * a.ndim)
    outs = (768, 256, 256, 512, 256)
    return pl.pallas_call(
        _inproj_kernel,
        grid=(bsz, seq // tm),
        in_specs=[tok(d),
                  pl.BlockSpec((1, 6, d), lambda b, i: (row(b), 0, 0)),
                  full(g1), full(w_in_p), full(w_gate), full(b_gate)],
        out_specs=[tok(w) for w in outs],
        out_shape=[jax.ShapeDtypeStruct((bsz, seq, w), F32) for w in outs],
        compiler_params=_cp(("parallel", "parallel")),
        name="inproj",
    )(x, mod, g1, w_in_p, w_gate, b_gate)


def _gla_direction(q, k, v, la, state, reverse):
    blk = q.shape[0]
    g = GLA_GROUP
    nchunk = g // GLA_CHUNK
    ti = lax.broadcasted_iota(jnp.int32, (g, g), 0)
    tj = lax.broadcasted_iota(jnp.int32, (g, g), 1)
    same = (ti // GLA_CHUNK) == (tj // GLA_CHUNK)
    cmask = same & ((tj >= ti) if reverse else (tj <= ti))
    tri = jnp.where(cmask, 1.0, 0.0).astype(BF16)
    qk_head = lax.broadcasted_iota(jnp.int32, (g, GLA_QK_W), 1) // GLA_DK
    v_head = lax.broadcasted_iota(jnp.int32, (g, GROUP_W), 1) // GLA_DV
    bd = (lax.broadcasted_iota(jnp.int32, (GLA_QK_W, GROUP_W), 0) // GLA_DK
          == lax.broadcasted_iota(jnp.int32, (GLA_QK_W, GROUP_W), 1) // GLA_DV)
    tok_chunk = lax.broadcasted_iota(jnp.int32, (GLA_QK_W, g), 1) // GLA_CHUNK
    scale = GLA_DK ** -0.5
    outs = [None] * (blk // g)
    groups = range(blk // g)
    for gi in (reversed(groups) if reverse else groups):
        sl = slice(gi * g, (gi + 1) * g)
        qg, kg, vg, lg = q[sl], k[sl], v[sl], la[sl]
        l0 = lg.astype(BF16)
        r0 = lg - l0.astype(F32)
        l1 = r0.astype(BF16)
        l2 = (r0 - l1.astype(F32)).astype(BF16)
        b3 = jnp.dot(tri, jnp.concatenate([l0, l1, l2], axis=1), preferred_element_type=F32)
        b = b3[:, 0:GLA_QK_W] + b3[:, GLA_QK_W:2 * GLA_QK_W] + b3[:, 2 * GLA_QK_W:3 * GLA_QK_W]
        last = 0 if reverse else GLA_CHUNK - 1
        blast = jnp.concatenate(
            [jnp.broadcast_to(b[c * GLA_CHUNK + last:c * GLA_CHUNK + last + 1], (GLA_CHUNK, GLA_QK_W))
             for c in range(nchunk)], axis=0)
        qd = (qg * scale) * jnp.exp(b)
        ki = (kg * jnp.exp(-b)).astype(BF16)
        kw_t = (kg * jnp.exp(blast - b)).T
        b_t = b.T
        vb = vg.astype(BF16)
        qs = jnp.concatenate([jnp.where(qk_head == h, qd, 0.0) for h in range(GLA_HEADS)],
                             axis=0).astype(BF16)
        s = lax.dot_general(qs, ki, (((1,), (1,)), ((), ())), preferred_element_type=F32)
        s = jnp.where(jnp.concatenate([cmask] * GLA_HEADS, axis=0), s, 0.0).astype(BF16)
        oh = jnp.dot(s, vb, preferred_element_type=F32)
        o = jnp.zeros((g, GROUP_W), F32)
        for h in range(GLA_HEADS):
            o = o + jnp.where(v_head == h, oh[h * g:(h + 1) * g], 0.0)
        kws = jnp.concatenate([jnp.where(tok_chunk == c, kw_t, 0.0) for c in range(nchunk)],
                              axis=0).astype(BF16)
        kv_all = jnp.dot(kws, vb, preferred_element_type=F32)
        qdb = qd.astype(BF16)
        inter = [None] * nchunk
        chunks = range(nchunk)
        for c in (reversed(chunks) if reverse else chunks):
            rows = slice(c * GLA_CHUNK, (c + 1) * GLA_CHUNK)
            inter[c] = jnp.dot(qdb[rows], state.astype(BF16), preferred_element_type=F32)
            col = c * GLA_CHUNK + last
            dec = jnp.exp(b_t[:, col:col + 1])
            kv = kv_all[c * GLA_QK_W:(c + 1) * GLA_QK_W]
            state = dec * state + jnp.where(bd, kv, 0.0)
        outs[gi] = o + jnp.concatenate(inter, axis=0)
    return jnp.concatenate(outs, axis=0) if len(outs) > 1 else outs[0], state


def _gla_kernel(qk_f, v_f, la_f, qk_b, v_b, la_b, s0_ref, of_ref, ob_ref, sfin_ref, st_ref):
    i = pl.program_id(1)

    @pl.when(i == 0)
    def _():
        st_ref[...] = s0_ref[0]

    qk = qk_f[0]
    o, s = _gla_direction(qk[:, 0:128], qk[:, 128:256], v_f[0], la_f[0], st_ref[0], False)
    of_ref[0] = o
    st_ref[0] = s
    qk = qk_b[0]
    o, s = _gla_direction(qk[:, 0:128], qk[:, 128:256], v_b[0], la_b[0], st_ref[1], True)
    ob_ref[0] = o
    st_ref[1] = s

    @pl.when(i == pl.num_programs(1) - 1)
    def _():
        sfin_ref[0] = st_ref[...]


def _gla(qkvg, la, s0, blk):
    bsz, seq, _ = qkvg.shape
    nb = seq // blk
    fwd = lambda j: pl.BlockSpec((1, blk, 256), lambda b, i: (b, i, j))
    bwd = lambda j: pl.BlockSpec((1, blk, 256), lambda b, i: (b, nb - 1 - i, j))
    la_f = pl.BlockSpec((1, blk, 128), lambda b, i: (b, i, 0))
    la_b = pl.BlockSpec((1, blk, 128), lambda b, i: (b, nb - 1 - i, 1))
    st = pl.BlockSpec((1, 2, GLA_QK_W, GROUP_W), lambda b, i: (b, 0, 0, 0))
    return pl.pallas_call(
        _gla_kernel,
        grid=(bsz, nb),
        in_specs=[fwd(0), fwd(1), la_f, bwd(0), bwd(1), la_b, st],
        out_specs=[pl.BlockSpec((1, blk, GROUP_W), lambda b, i: (b, i, 0)),
                   pl.BlockSpec((1, blk, GROUP_W), lambda b, i: (b, nb - 1 - i, 0)),
                   st],
        out_shape=[jax.ShapeDtypeStruct((bsz, seq, GROUP_W), F32),
                   jax.ShapeDtypeStruct((bsz, seq, GROUP_W), F32),
                   jax.ShapeDtypeStruct((bsz, 2, GLA_QK_W, GROUP_W), F32)],
        scratch_shapes=[pltpu.VMEM((2, GLA_QK_W, GROUP_W), F32)],
        compiler_params=_cp(("parallel", "arbitrary")),
        name="gla",
    )(qkvg, qkvg, la, qkvg, qkvg, la, s0)


def _row_shift(x, s, col, row_len):
    if s == 0:
        return x
    n = x.shape[0]
    y = pltpu.roll(x, (-s) % n, 0)
    valid = (col < row_len - s) if s > 0 else (col >= -s)
    return jnp.where(valid, y, 0.0)


def _row_conv(x, w, col, row_len):
    k = w.shape[0]
    acc = None
    for j in range(k):
        term = _row_shift(x, j - k // 2, col, row_len) * w[j:j + 1]
        acc = term if acc is None else acc + term
    return acc


def _col_conv(pad_ref, w, t0, n):
    k = w.shape[0]
    acc = None
    for j in range(k):
        start = pl.multiple_of(t0 + j * GRID_W, GRID_W)
        term = pad_ref[pl.ds(start, n), :] * w[j:j + 1]
        acc = term if acc is None else acc + term
    return acc


def _pool_diffs(x, col, row_len):
    colh = col[:, 0:LANE]
    colf = colh.astype(F32)
    first = lax.broadcasted_iota(jnp.int32, colh.shape, 1) < POOL_GW

    def count(win):
        return jnp.minimum(colf + win // 2, float(row_len)) - jnp.maximum(colf - win // 2, 0.0)

    means = []
    for hi in range(len(POOL_WINDOWS) // 2):
        wa, wb = POOL_WINDOWS[2 * hi], POOL_WINDOWS[2 * hi + 1]
        xh = x[:, hi * LANE:(hi + 1) * LANE]
        acc_a = acc_b = None
        for s in range(-(wb // 2), wb // 2):
            t = _row_shift(xh, s, colh, row_len)
            acc_b = t if acc_b is None else acc_b + t
            if -(wa // 2) <= s < wa // 2:
                acc_a = t if acc_a is None else acc_a + t
        means.append(jnp.where(first, acc_a / count(wa), acc_b / count(wb)))
    return jnp.concatenate(means, axis=-1) - x


def _local_tail(u, scb, conv_sc, pool, col, row_len, cb, lg, lb, pw, ps, out_ref):
    u = u + cb
    uc = u - jnp.mean(u, axis=-1, keepdims=True)
    ln = uc * lax.rsqrt(jnp.mean(uc * uc, axis=-1, keepdims=True) + NORM_EPS) * lg + lb
    out_ref[0, :, 0:256] = _silu(ln).astype(BF16)
    out_ref[0, :, 256:512] = (scb * conv_sc).astype(BF16)
    out_ref[0, :, 512:768] = (_dot(_pool_diffs(pool, col, row_len), pw) * ps).astype(BF16)


def _local_axial_kernel(uh_ref, uv_ref, scb_ref, sch_ref, scv_ref, pool_ref,
                        cw_ref, cb_ref, lg_ref, lb_ref, sw_ref, pw_ref, ps_ref,
                        out_ref, upad, spad):
    i = pl.program_id(1)
    blk = uh_ref.shape[1]
    seq = uv_ref.shape[1]
    pu = (CONF_WIDTH // 2) * GRID_W
    psc = (SC_WIDTH // 2) * GRID_W

    @pl.when(i == 0)
    def _():
        upad[0:pu, :] = jnp.zeros((pu, LANE), F32)
        upad[pu:pu + seq, :] = uv_ref[0]
        upad[pu + seq:pu + seq + pu, :] = jnp.zeros((pu, LANE), F32)
        spad[0:psc, :] = jnp.zeros((psc, LANE), F32)
        spad[psc:psc + seq, :] = scv_ref[0]
        spad[psc + seq:psc + seq + psc, :] = jnp.zeros((psc, LANE), F32)

    t0 = i * blk
    col = lax.broadcasted_iota(jnp.int32, (blk, LANE), 0) % GRID_W
    cw = cw_ref[...]
    sw = sw_ref[...]
    u = jnp.concatenate([_row_conv(uh_ref[0], cw[:, 0:128], col, GRID_W),
                         _col_conv(upad, cw[:, 128:256], t0, blk)], axis=-1)
    csc = jnp.concatenate([_row_conv(sch_ref[0], sw[:, 0:128], col, GRID_W),
                           _col_conv(spad, sw[:, 128:256], t0, blk)], axis=-1)
    col2 = jnp.concatenate([col, col], axis=-1)
    _local_tail(u, scb_ref[0], csc, pool_ref[0], col2, GRID_W, cb_ref[...], lg_ref[...],
                lb_ref[...], pw_ref[...], ps_ref[...], out_ref)


def _local_seq_kernel(u_ref, sc_ref, pool_ref, cw_ref, cb_ref, lg_ref, lb_ref, sw_ref,
                      pw_ref, ps_ref, out_ref):
    seq = u_ref.shape[1]
    col = lax.broadcasted_iota(jnp.int32, (seq, GROUP_W), 0)
    sc = sc_ref[0]
    u = _row_conv(u_ref[0], cw_ref[...], col, seq)
    csc = _row_conv(sc[:, 256:512], sw_ref[...], col, seq)
    _local_tail(u, sc[:, 0:256], csc, pool_ref[0], col, seq, cb_ref[...], lg_ref[...],
                lb_ref[...], pw_ref[...], ps_ref[...], out_ref)


def _local_params(p):
    return (p["conf_w"], p["conf_b"], p["conf_g"], p["conf_beta"], p["sc_w"], p["pool_bd"],
            p["pool_scale"])


def _local_axial(u, sc, pool, p, blk):
    bsz, seq, _ = u.shape
    cw, cb, lg, lb, sw, pw, ps = _local_params(p)
    blkspec = lambda w, j: pl.BlockSpec((1, blk, w), lambda b, i: (b, i, j))
    seqspec = lambda j: pl.BlockSpec((1, seq, LANE), lambda b, i: (b, 0, j))
    full = lambda a: pl.BlockSpec(a.shape, lambda b, i: (0,) * a.ndim)
    pu = (CONF_WIDTH // 2) * GRID_W
    psc = (SC_WIDTH // 2) * GRID_W
    return pl.pallas_call(
        _local_axial_kernel,
        grid=(bsz, seq // blk),
        in_specs=[blkspec(LANE, 0), seqspec(1), blkspec(256, 0), blkspec(LANE, 2), seqspec(3),
                  blkspec(256, 0), full(cw), full(cb), full(lg), full(lb), full(sw), full(pw),
                  full(ps)],
        out_specs=pl.BlockSpec((1, blk, 768), lambda b, i: (b, i, 0)),
        out_shape=jax.ShapeDtypeStruct((bsz, seq, 768), BF16),
        scratch_shapes=[pltpu.VMEM((seq + 2 * pu, LANE), F32),
                        pltpu.VMEM((seq + 2 * psc, LANE), F32)],
        compiler_params=_cp(("parallel", "arbitrary")),
        name="local_axial",
    )(u, u, sc, sc, sc, pool, cw, cb, lg, lb, sw, pw, ps)


def _local_seq(u, sc, pool, p):
    bsz, seq, _ = u.shape
    cw, cb, lg, lb, sw, pw, ps = _local_params(p)
    tok = lambda w: pl.BlockSpec((1, seq, w), lambda b: (b, 0, 0))
    full = lambda a: pl.BlockSpec(a.shape, lambda b: (0,) * a.ndim)
    return pl.pallas_call(
        _local_seq_kernel,
        grid=(bsz,),
        in_specs=[tok(256), tok(512), tok(256), full(cw), full(cb), full(lg), full(lb),
                  full(sw), full(pw), full(ps)],
        out_specs=tok(768),
        out_shape=jax.ShapeDtypeStruct((bsz, seq, 768), BF16),
        compiler_params=_cp(("parallel",)),
        name="local_seq",
    )(u, sc, pool, cw, cb, lg, lb, sw, pw, ps)


def _route(logits):
    lane = lax.broadcasted_iota(jnp.int32, logits.shape, 1)
    neg = -jnp.inf
    big = ROUTE_W

    def first_max(vals):
        mx = jnp.max(vals, axis=-1, keepdims=True)
        idx = jnp.min(jnp.where(vals == mx, lane, big), axis=-1, keepdims=True)
        return mx, idx

    lg = jnp.where(lane < N_GROUPS, logits, neg)
    gmx, grp = first_max(lg)
    p_grp = 1.0 / jnp.sum(jnp.exp(lg - gmx), axis=-1, keepdims=True)
    lo = N_GROUPS + grp * EXPERTS_PER_GROUP
    le = jnp.where((lane >= lo) & (lane < lo + EXPERTS_PER_GROUP), logits, neg)
    v1, i1 = first_max(le)
    v2, i2 = first_max(jnp.where(lane == i1, neg, le))
    e2 = jnp.exp(v2 - v1)
    w1 = p_grp / (1.0 + e2)
    w2 = p_grp * e2 / (1.0 + e2)
    rec = jnp.where(lane == 0, (i1 - N_GROUPS).astype(F32), 0.0)
    rec = jnp.where(lane == 1, (i2 - N_GROUPS).astype(F32), rec)
    rec = jnp.where(lane == 2, w1, rec)
    return jnp.where(lane == 3, w2, rec)


def _outproj_kernel(of_ref, ob_ref, g_ref, loc_ref, x_ref, mod_ref, gout_ref, hsum_ref,
                    wo_ref, g2_ref, wr_ref, br_ref, xo_ref, m_ref, route_ref):
    mod = mod_ref[0]
    o = of_ref[0] + ob_ref[0]
    o2 = o * o
    hi = o2.astype(BF16)
    lo = (o2 - hi.astype(F32)).astype(BF16)
    ms = (jnp.dot(hi, hsum_ref[...], preferred_element_type=F32)
          + jnp.dot(lo, hsum_ref[...], preferred_element_type=F32)) / GLA_DV
    y_gla = o * lax.rsqrt(ms + NORM_EPS) * gout_ref[...] * _silu(g_ref[0])
    wo = wo_ref[...]
    proj = _dot(y_gla, wo[0:256]) + _dot(loc_ref[0], wo[256:1024])
    x = x_ref[0] + mod[2:3] * proj
    xo_ref[0] = x
    m = _rms(x, g2_ref[...]) * (1.0 + mod[4:5]) + mod[3:4]
    m_ref[0] = m
    route_ref[0] = _route(_dot(m, wr_ref[...]) + br_ref[...])


def _outproj(o_f, o_b, qkvg, y_loc, x, mod, mod_row, p, tm):
    bsz, seq, d = x.shape
    row = (lambda b: b) if mod_row is None else (lambda b: mod_row)
    tok = lambda w, j=0: pl.BlockSpec((1, tm, w), lambda b, i: (b, i, j))
    full = lambda a: pl.BlockSpec(a.shape, lambda b, i: (0,) * a.ndim)
    consts = (p["g_out"], p["head_sum"], p["w_out"], p["g_norm2"], p["w_router"], p["b_router"])
    return pl.pallas_call(
        _outproj_kernel,
        grid=(bsz, seq // tm),
        in_specs=[tok(256), tok(256), tok(256, 2), tok(768), tok(d),
                  pl.BlockSpec((1, 6, d), lambda b, i: (row(b), 0, 0))] + [full(a) for a in consts],
        out_specs=[tok(d), tok(d), tok(ROUTE_W)],
        out_shape=[jax.ShapeDtypeStruct((bsz, seq, d), F32),
                   jax.ShapeDtypeStruct((bsz, seq, d), F32),
                   jax.ShapeDtypeStruct((bsz, seq, ROUTE_W), F32)],
        compiler_params=_cp(("parallel", "parallel")),
        name="outproj",
    )(o_f, o_b, qkvg, y_loc, x, mod, *consts)


RANK_ROWS = 512


def _rank_kernel(route_ref, rank_ref, cnt_ref, carry):
    i = pl.program_id(0)

    @pl.when(i == 0)
    def _():
        carry[...] = jnp.zeros_like(carry)

    rec = route_ref[...]
    tm = rec.shape[0]
    lane = lax.broadcasted_iota(jnp.int32, rec.shape, 1).astype(F32)
    oh0 = jnp.where(lane == rec[:, 0:1], 1.0, 0.0)
    oh1 = jnp.where(lane == rec[:, 1:2], 1.0, 0.0)
    ti = lax.broadcasted_iota(jnp.int32, (tm, tm), 0)
    tj = lax.broadcasted_iota(jnp.int32, (tm, tm), 1)
    before = jnp.where(tj < ti, 1.0, 0.0).astype(BF16)
    tot0 = jnp.sum(oh0, axis=0, keepdims=True)
    tot1 = jnp.sum(oh1, axis=0, keepdims=True)
    base = carry[...]
    pre0 = jnp.dot(before, oh0.astype(BF16), preferred_element_type=F32) + base
    pre1 = jnp.dot(before, oh1.astype(BF16), preferred_element_type=F32) + (base + tot0)
    r0 = jnp.sum(oh0 * pre0, axis=-1, keepdims=True)
    r1 = jnp.sum(oh1 * pre1, axis=-1, keepdims=True)
    lane_i = lax.broadcasted_iota(jnp.int32, rec.shape, 1)
    cols = jnp.where(lane_i == 0, r0, jnp.where(lane_i == 1, r1, 0.0))
    cols = jnp.where(lane_i == 2, rec[:, 0:1], jnp.where(lane_i == 3, rec[:, 1:2], cols))
    rank_ref[...] = cols.T[0:8, :].astype(jnp.int32)
    carry[...] = base + tot0 + tot1
    cnt_ref[...] = carry[...]


def _rank(route):
    n_tok = route.shape[0]
    tm = RANK_ROWS if n_tok % RANK_ROWS == 0 else 256
    return pl.pallas_call(
        _rank_kernel,
        grid=(n_tok // tm,),
        in_specs=[pl.BlockSpec((tm, ROUTE_W), lambda i: (i, 0))],
        out_specs=[pl.BlockSpec((8, tm), lambda i: (0, i)),
                   pl.BlockSpec((1, ROUTE_W), lambda i: (0, 0))],
        out_shape=[jax.ShapeDtypeStruct((8, n_tok), jnp.int32),
                   jax.ShapeDtypeStruct((1, ROUTE_W), F32)],
        scratch_shapes=[pltpu.VMEM((1, ROUTE_W), F32)],
        compiler_params=_cp(("arbitrary",)),
        name="rank",
    )(route)


def _plan(route):
    n_tok = route.shape[0]
    rank, cnt = _rank(route)
    counts = cnt[0, 0:N_EXPERTS].astype(jnp.int32)
    padded = (counts + MOE_BLOCK - 1) // MOE_BLOCK * MOE_BLOCK
    pad_end = jnp.cumsum(padded)
    pad_start = pad_end - padded
    onehot = rank[2:4][None] == jnp.arange(N_EXPERTS, dtype=jnp.int32)[:, None, None]
    dest = (jnp.sum(jnp.where(onehot, pad_start[:, None, None], 0), axis=0) + rank[0:2]).reshape(-1)
    n_blocks = 2 * n_tok // MOE_BLOCK + N_EXPERTS
    blk_start = jnp.arange(n_blocks, dtype=jnp.int32) * MOE_BLOCK
    nused = pad_end[-1] // MOE_BLOCK
    blk = jnp.minimum(jnp.arange(n_blocks, dtype=jnp.int32), nused - 1)
    blk_expert = jnp.sum((blk[:, None] * MOE_BLOCK >= pad_end[None, :]).astype(jnp.int32), axis=-1)
    fill_start = pad_start + counts
    fill_n = padded - counts
    ids = jnp.arange(N_EXPERTS, dtype=jnp.int32)
    later = (ids[None, :] > ids[:, None]) & (counts[None, :] > 0)
    next_expert = jnp.min(jnp.where(later, ids[None, :], N_EXPERTS), axis=1)
    next_expert = jnp.where(next_expert == N_EXPERTS, -1, next_expert).astype(jnp.int32)
    return dict(dest=dest.astype(jnp.int32), blk=blk.astype(jnp.int32), next_expert=next_expert,
                blk_expert=jnp.minimum(blk_expert, N_EXPERTS - 1).astype(jnp.int32),
                nused=nused.astype(jnp.int32).reshape(1), fill_start=fill_start.astype(jnp.int32),
                fill_n=fill_n.astype(jnp.int32), n_blocks=n_blocks)


DISPATCH_ROWS = 256


def _dispatch_kernel(dest_ref, fstart_ref, fn_ref, nused_ref, *refs, tr, n_x, n_c):
    if n_c:
        mx_hbm, mc_hbm, xs_hbm, buf, zbuf, in_sem, out_sem, fsem = refs
    else:
        mx_hbm, xs_hbm, buf, zbuf, in_sem, out_sem, fsem = refs
        mc_hbm = None
    i = pl.program_id(0)
    nsteps = n_x + n_c
    nslot = 3

    def start_load(step):
        slot = step % nslot

        @pl.when(step < n_x)
        def _():
            rows = pl.ds(pl.multiple_of(step * tr, tr), tr)
            pltpu.make_async_copy(mx_hbm.at[rows], buf.at[slot], in_sem.at[slot]).start()

        if n_c:
            @pl.when(step >= n_x)
            def _():
                rows = pl.ds(pl.multiple_of((step - n_x) * tr, tr), tr)
                pltpu.make_async_copy(mc_hbm.at[rows], buf.at[slot], in_sem.at[slot]).start()

    def wait_load(step):
        slot = step % nslot
        pltpu.make_async_copy(mx_hbm.at[pl.ds(0, tr)], buf.at[slot], in_sem.at[slot]).wait()

    def row_copy(step, r, k):
        slot = step % nslot
        dst = dest_ref[k * (nsteps * tr) + step * tr + r]
        return pltpu.make_async_copy(buf.at[slot, pl.ds(r, 1)], xs_hbm.at[pl.ds(dst, 1)],
                                     out_sem.at[slot])

    def wait_scatter(step):
        slot = step % nslot
        for _ in range(2):
            pltpu.make_async_copy(buf.at[slot], xs_hbm.at[pl.ds(0, tr)], out_sem.at[slot]).wait()

    @pl.when(i == 0)
    def _():
        zbuf[...] = jnp.zeros_like(zbuf)
        start_load(i)

    @pl.when(i >= 2)
    def _():
        wait_scatter(i - 2)

    @pl.when(i + 1 < nsteps)
    def _():
        start_load(i + 1)

    wait_load(i)

    for r in range(tr):
        row_copy(i, r, 0).start()
        row_copy(i, r, 1).start(priority=1)

    @pl.when(i == nsteps - 1)
    def _():
        if nsteps >= 2:
            wait_scatter(i - 1)
        wait_scatter(i)

        def fill_row(e, r):
            return pltpu.make_async_copy(zbuf.at[pl.ds(0, 1)],
                                         xs_hbm.at[pl.ds(fstart_ref[e] + r, 1)], fsem.at[0])

        def fill_piece(e, head, j):
            start = pl.multiple_of(fstart_ref[e] + head + 8 * j, 8)
            return pltpu.make_async_copy(zbuf.at[pl.ds(0, 8)], xs_hbm.at[pl.ds(start, 8)],
                                         fsem.at[0])

        def per_expert(start_not_wait):
            def body(e, carry):
                head = jnp.minimum((8 - fstart_ref[e] % 8) % 8, fn_ref[e])
                pieces = (fn_ref[e] - head) // 8
                if start_not_wait:
                    lax.fori_loop(0, head, lambda r, c: (fill_row(e, r).start(), c)[1], 0)
                    lax.fori_loop(0, pieces, lambda j, c: (fill_piece(e, head, j).start(), c)[1], 0)
                else:
                    lax.fori_loop(0, head, lambda r, c: (fill_row(e, r).wait(), c)[1], 0)
                    lax.fori_loop(0, pieces, lambda j, c: (fill_piece(e, head, j).wait(), c)[1], 0)
                return carry
            return body

        lax.fori_loop(0, N_EXPERTS, per_expert(True), 0)
        lax.fori_loop(0, N_EXPERTS, per_expert(False), 0)

        def tail_copy(b):
            start = pl.multiple_of(b * MOE_BLOCK, MOE_BLOCK)
            return pltpu.make_async_copy(zbuf, xs_hbm.at[pl.ds(start, MOE_BLOCK)], fsem.at[0])

        nblk = xs_hbm.shape[0] // MOE_BLOCK
        lax.fori_loop(nused_ref[0], nblk, lambda b, c: (tail_copy(b).start(), c)[1], 0)
        lax.fori_loop(nused_ref[0], nblk, lambda b, c: (tail_copy(b).wait(), c)[1], 0)


def _dispatch(m_x, m_c, plan):
    n_lat, d = m_x.shape
    n_ctx = 0 if m_c is None else m_c.shape[0]
    tr = DISPATCH_ROWS
    while n_lat % tr or n_ctx % tr:
        tr //= 2
    slots = plan["n_blocks"] * MOE_BLOCK
    n_x, n_c = n_lat // tr, n_ctx // tr
    srcs = (m_x,) if m_c is None else (m_x, m_c)
    grid_spec = pltpu.PrefetchScalarGridSpec(
        num_scalar_prefetch=4,
        grid=(n_x + n_c,),
        in_specs=[pl.BlockSpec(memory_space=pl.ANY)] * len(srcs),
        out_specs=pl.BlockSpec(memory_space=pl.ANY),
        scratch_shapes=[pltpu.VMEM((3, tr, d), F32), pltpu.VMEM((MOE_BLOCK, d), F32),
                        pltpu.SemaphoreType.DMA((3,)), pltpu.SemaphoreType.DMA((3,)),
                        pltpu.SemaphoreType.DMA((1,))],
    )
    return pl.pallas_call(
        functools.partial(_dispatch_kernel, tr=tr, n_x=n_x, n_c=n_c),
        grid_spec=grid_spec,
        out_shape=jax.ShapeDtypeStruct((slots, d), F32),
        compiler_params=_cp(("arbitrary",)),
        name="dispatch",
    )(plan["dest"], plan["fill_start"], plan["fill_n"], plan["nused"], *srcs)


XS_SLOTS = 3


def _expert_kernel(be_ref, nused_ref, next_ref, xs_hbm, w1_hbm, w2_hbm, ys_ref,
                   xbuf, w1f, w2f, w1b, w2b, xsem, wsem, *, layer):
    i = pl.program_id(0)
    nused = nused_ref[0]
    used = i < nused
    e = be_ref[i]
    fresh = (i == 0) | (e != be_ref[jnp.maximum(i - 1, 0)])

    def weight_copies(ex):
        return (pltpu.make_async_copy(w1_hbm.at[layer, ex], w1f, wsem.at[0]),
                pltpu.make_async_copy(w2_hbm.at[layer, ex], w2f, wsem.at[1]))

    def block_copy(b):
        rows = pl.ds(pl.multiple_of(b * MOE_BLOCK, MOE_BLOCK), MOE_BLOCK)
        return pltpu.make_async_copy(xs_hbm.at[rows], xbuf.at[b % XS_SLOTS], xsem.at[b % XS_SLOTS])

    @pl.when(used & (i == 0))
    def _():
        for cp in weight_copies(e):
            cp.start()
        block_copy(i).start()

        @pl.when(nused > 1)
        def _():
            block_copy(i + 1).start()

    @pl.when(i + 2 < nused)
    def _():
        block_copy(i + 2).start()

    @pl.when(used & fresh)
    def _():
        for cp in weight_copies(e):
            cp.wait()
        w1b[...] = w1f[...].astype(BF16)
        w2b[...] = w2f[...].astype(BF16)
        nxt = next_ref[e]

        @pl.when(nxt >= 0)
        def _():
            for cp in weight_copies(nxt):
                cp.start()

    @pl.when(used)
    def _():
        block_copy(i).wait()
        h = _dot(xbuf[i % XS_SLOTS], w1b[...])
        act = _silu(h[:, :D_EXPERT]) * h[:, D_EXPERT:]
        ys_ref[...] = _dot(act, w2b[...])

    @pl.when(jnp.logical_not(used))
    def _():
        ys_ref[...] = jnp.zeros_like(ys_ref)


def _experts(xs, plan, w1, w2, layer):
    slots, d = xs.shape
    grid_spec = pltpu.PrefetchScalarGridSpec(
        num_scalar_prefetch=3,
        grid=(plan["n_blocks"],),
        in_specs=[pl.BlockSpec(memory_space=pl.ANY), pl.BlockSpec(memory_space=pl.ANY),
                  pl.BlockSpec(memory_space=pl.ANY)],
        out_specs=pl.BlockSpec((MOE_BLOCK, d), lambda i, *_: (i, 0)),
        scratch_shapes=[pltpu.VMEM((XS_SLOTS, MOE_BLOCK, d), F32),
                        pltpu.VMEM((d, 2 * D_EXPERT), F32), pltpu.VMEM((D_EXPERT, d), F32),
                        pltpu.VMEM((d, 2 * D_EXPERT), BF16), pltpu.VMEM((D_EXPERT, d), BF16),
                        pltpu.SemaphoreType.DMA((XS_SLOTS,)), pltpu.SemaphoreType.DMA((2,))],
    )
    return pl.pallas_call(
        functools.partial(_expert_kernel, layer=layer),
        grid_spec=grid_spec,
        out_shape=jax.ShapeDtypeStruct((slots, d), F32),
        compiler_params=_cp(("arbitrary",)),
        name="experts",
    )(plan["blk_expert"], plan["nused"], plan["next_expert"], xs, w1, w2)


COMBINE_ROWS = 256


def _combine_kernel(dest_ref, x_ref, route_ref, mod_ref, gf_ref, ys_hbm, o_ref, y0buf, y1buf, sem,
                    *, final, tok_off):
    i = pl.program_id(0)
    nsteps = pl.num_programs(0)
    tc = x_ref.shape[1]

    def row_copy(step, r, k, slot):
        a = k * (dest_ref.shape[0] // 2) + tok_off + step * tc + r
        buf = y0buf if k == 0 else y1buf
        return pltpu.make_async_copy(ys_hbm.at[pl.ds(dest_ref[a], 1)], buf.at[slot, pl.ds(r, 1)],
                                     sem.at[slot])

    def issue_step(step):
        for r in range(tc):
            row_copy(step, r, 0, step % 2).start()
            row_copy(step, r, 1, step % 2).start()

    @pl.when(i == 0)
    def _():
        issue_step(0)

    @pl.when(i + 1 < nsteps)
    def _():
        issue_step(i + 1)

    slot = i % 2
    pltpu.make_async_copy(ys_hbm.at[pl.ds(0, tc)], y0buf.at[slot], sem.at[slot]).wait()
    pltpu.make_async_copy(ys_hbm.at[pl.ds(0, tc)], y1buf.at[slot], sem.at[slot]).wait()
    rec = route_ref[...]
    y = rec[:, 2:3] * y0buf[slot] + rec[:, 3:4] * y1buf[slot]
    x = x_ref[0] + mod_ref[0][5:6] * y
    o_ref[0] = _rms(x, gf_ref[...]) if final else x


def _combine(x, ys, route, plan, tok_off, mod, mod_row, g_final, final):
    bsz, seq, d = x.shape
    tc = min(COMBINE_ROWS, seq)
    nt = seq // tc
    assert seq % tc == 0 and tok_off % tc == 0
    row = (lambda i: i // nt) if mod_row is None else (lambda i: mod_row)
    grid_spec = pltpu.PrefetchScalarGridSpec(
        num_scalar_prefetch=1,
        grid=(bsz * nt,),
        in_specs=[pl.BlockSpec((1, tc, d), lambda i, de: (i // nt, i % nt, 0)),
                  pl.BlockSpec((tc, ROUTE_W), lambda i, de: (tok_off // tc + i, 0)),
                  pl.BlockSpec((1, 6, d), lambda i, de: (row(i), 0, 0)),
                  pl.BlockSpec((1, d), lambda i, de: (0, 0)),
                  pl.BlockSpec(memory_space=pl.ANY)],
        out_specs=pl.BlockSpec((1, tc, d), lambda i, de: (i // nt, i % nt, 0)),
        scratch_shapes=[pltpu.VMEM((2, tc, d), F32), pltpu.VMEM((2, tc, d), F32),
                        pltpu.SemaphoreType.DMA((2,))],
    )
    return pl.pallas_call(
        functools.partial(_combine_kernel, final=final, tok_off=tok_off),
        grid_spec=grid_spec,
        out_shape=jax.ShapeDtypeStruct((bsz, seq, d), F32),
        compiler_params=_cp(("arbitrary",)),
        name="combine",
    )(plan["dest"], x, route, mod, g_final, ys)


def _layer_params(l, w_in, gla_w_gate_f, gla_b_gate_f, gla_w_gate_b, gla_b_gate_b, gla_g_out,
                  conf_w_dw, conf_b_dw, conf_ln_g, conf_ln_b, sc_w_dw, pool_w, pool_scale, w_out,
                  router_w_group, router_b_group, router_w_expert, router_b_expert,
                  g_norm1, g_norm2):
    d = D_MODEL
    idx = np.cumsum((0,) + IN_SPLITS)
    q, k, v, g, zf, zb, conf, scb, scc, scx, pool = [w_in[l][:, idx[j]:idx[j + 1]] for j in range(11)]
    pad = jnp.zeros((d, N_IN_PAD - idx[-1]), F32)
    w_in_p = jnp.concatenate([q, k, v, g, conf, scb, scc, scx, pool, zf, zb, pad], axis=1).astype(BF16)
    r = GLA_GATE_RANK
    w_gate = jnp.zeros((LANE, 2 * GLA_QK_W), F32)
    w_gate = w_gate.at[0:r, 0:GLA_QK_W].set(gla_w_gate_f[l]).at[r:2 * r, GLA_QK_W:].set(gla_w_gate_b[l])
    b_gate = jnp.concatenate([gla_b_gate_f[l], gla_b_gate_b[l]])[None]
    pool_bd = jnp.zeros((GROUP_W, GROUP_W), F32)
    for gi in range(len(POOL_WINDOWS)):
        s = slice(gi * POOL_GW, (gi + 1) * POOL_GW)
        pool_bd = pool_bd.at[s, s].set(pool_w[l, gi])
    head = np.arange(GROUP_W) // GLA_DV
    head_sum = jnp.asarray(head[:, None] == head[None, :], BF16)
    w_router = jnp.zeros((d, ROUTE_W), F32)
    w_router = w_router.at[:, 0:N_GROUPS].set(router_w_group[l])
    w_router = w_router.at[:, N_GROUPS:N_GROUPS + N_EXPERTS].set(router_w_expert[l])
    b_router = jnp.zeros((1, ROUTE_W), F32)
    b_router = b_router.at[0, 0:N_GROUPS].set(router_b_group[l])
    b_router = b_router.at[0, N_GROUPS:N_GROUPS + N_EXPERTS].set(router_b_expert[l])
    return dict(
        w_in=w_in_p, w_gate=w_gate.astype(BF16), b_gate=b_gate,
        g_norm1=g_norm1[l][None], g_norm2=g_norm2[l][None],
        g_out=gla_g_out[l][None], head_sum=head_sum,
        conf_w=conf_w_dw[l], conf_b=conf_b_dw[l][None], conf_g=conf_ln_g[l][None],
        conf_beta=conf_ln_b[l][None], sc_w=sc_w_dw[l], pool_bd=pool_bd.astype(BF16),
        pool_scale=pool_scale[l][None], w_out=w_out[l].astype(BF16),
        w_router=w_router.astype(BF16), b_router=b_router)


def kernel(x, c, ctx, c_ctx, w_mod, b_mod, g_norm1, g_norm2, w_in, gla_w_gate_f, gla_b_gate_f,
           gla_w_gate_b, gla_b_gate_b, gla_g_out, conf_w_dw, conf_b_dw, conf_ln_g, conf_ln_b,
           sc_w_dw, pool_w, pool_scale, w_out, router_w_group, router_b_group, router_w_expert,
           router_b_expert, expert_w_in, expert_w_out, g_final):
    bsz, seq, d = x.shape
    ctx_len = ctx.shape[1]
    depth = w_mod.shape[0]
    n_lat = bsz * seq
    n_ctx = bsz * ctx_len
    assert d == D_MODEL and seq % TOKEN_TILE == 0 and ctx_len % GLA_GROUP == 0 and bsz <= 7
    tm = TOKEN_TILE
    tm_c = GLA_GROUP

    cv = jnp.zeros((8, d), F32).at[0:bsz].set(c).at[bsz].set(c_ctx)
    mods = _adaln(cv, w_mod, b_mod).reshape(depth, 8, 6, d)
    gf = g_final[None]
    zero_state = jnp.zeros((bsz, 2, GLA_QK_W, GROUP_W), F32)

    for l in range(depth):
        last = l == depth - 1
        p = _layer_params(l, w_in, gla_w_gate_f, gla_b_gate_f, gla_w_gate_b, gla_b_gate_b,
                          gla_g_out, conf_w_dw, conf_b_dw, conf_ln_g, conf_ln_b, sc_w_dw, pool_w,
                          pool_scale, w_out, router_w_group, router_b_group, router_w_expert,
                          router_b_expert, g_norm1, g_norm2)
        mod = mods[l]

        c_qkvg, c_la, c_u, c_sc, c_pool = _inproj(ctx, mod, bsz, p["g_norm1"], p["w_in"],
                                                  p["w_gate"], p["b_gate"], tm_c)
        c_of, c_ob, s_ctx = _gla(c_qkvg, c_la, zero_state, ctx_len)
        x_qkvg, x_la, x_u, x_sc, x_pool = _inproj(x, mod, None, p["g_norm1"], p["w_in"],
                                                  p["w_gate"], p["b_gate"], tm)
        x_of, x_ob, _ = _gla(x_qkvg, x_la, s_ctx, TOKEN_TILE)
        x_loc = _local_axial(x_u, x_sc, x_pool, p, min(LOCAL_TILE, seq))
        x, m_x, r_x = _outproj(x_of, x_ob, x_qkvg, x_loc, x, mod, None, p, tm)

        if last:
            m_c = None
            route = r_x.reshape(n_lat, ROUTE_W)
        else:
            c_loc = _local_seq(c_u, c_sc, c_pool, p)
            ctx, m_c, r_c = _outproj(c_of, c_ob, c_qkvg, c_loc, ctx, mod, bsz, p, tm_c)
            m_c = m_c.reshape(n_ctx, d)
            route = jnp.concatenate([r_x.reshape(n_lat, ROUTE_W), r_c.reshape(n_ctx, ROUTE_W)], axis=0)

        plan = _plan(route)
        xs = _dispatch(m_x.reshape(n_lat, d), m_c, plan)
        ys = _experts(xs, plan, expert_w_in, expert_w_out, l)
        x = _combine(x, ys, route, plan, 0, mod, None, gf, last)
        if not last:
            ctx = _combine(ctx, ys, route, plan, n_lat, mod, bsz, gf, False)
    return x
```

```python
import functools

import numpy as np
import jax
import jax.numpy as jnp
from jax import lax
from jax.experimental import pallas as pl
from jax.experimental.pallas import tpu as pltpu

F32 = jnp.float32
BF16 = jnp.bfloat16

D_MODEL = 1024
GRID_W = 64
GROUP_W = 256
GLA_HEADS = 4
GLA_DV = 64
GLA_DK = 32
GLA_QK_W = 128
GLA_GATE_RANK = 16
GLA_GATE_NORM = 16.0
GLA_CHUNK = 64
GLA_GROUP = 256
CONF_WIDTH = 31
SC_WIDTH = 3
POOL_WINDOWS = (2, 4, 8, 16)
POOL_GW = 64
N_GROUPS = 4
EXPERTS_PER_GROUP = 8
N_EXPERTS = 32
D_EXPERT = 512
MOE_BLOCK = 512
NORM_EPS = 1e-6
IN_SPLITS = (128, 128, 256, 256, 16, 16, 512, 256, 256, 256, 256)
N_IN_PAD = 2432
LANE = 128
ROUTE_W = LANE

TOKEN_TILE = 1024
LOCAL_TILE = 2048

VMEM_LIMIT = 48 << 20


def _cp(sem, vmem=VMEM_LIMIT):
    return pltpu.CompilerParams(dimension_semantics=sem, vmem_limit_bytes=vmem)


def _sigmoid(x):
    return 1.0 / (1.0 + jnp.exp(-x))


def _silu(x):
    return x * _sigmoid(x)


def _log_sigmoid(x):
    return jnp.minimum(x, 0.0) - jnp.log1p(jnp.exp(-jnp.abs(x)))


def _dot(a, b):
    return jnp.dot(a.astype(BF16), b.astype(BF16), preferred_element_type=F32)


def _rms(x, g):
    return x * lax.rsqrt(jnp.mean(x * x, axis=-1, keepdims=True) + NORM_EPS) * g


def _adaln_kernel(cv_ref, w_ref, b_ref, o_ref):
    o_ref[0] = _dot(_silu(cv_ref[...]), w_ref[0]) + b_ref[0]


def _adaln(cv, w_mod, b_mod):
    depth, d, n = w_mod.shape
    tn = 1024
    return pl.pallas_call(
        _adaln_kernel,
        grid=(depth, n // tn),
        in_specs=[pl.BlockSpec((8, d), lambda l, j: (0, 0)),
                  pl.BlockSpec((1, d, tn), lambda l, j: (l, 0, j)),
                  pl.BlockSpec((1, 1, tn), lambda l, j: (l, 0, j))],
        out_specs=pl.BlockSpec((1, 8, tn), lambda l, j: (l, 0, j)),
        out_shape=jax.ShapeDtypeStruct((depth, 8, n), F32),
        compiler_params=_cp(("parallel", "parallel")),
        name="adaln",
    )(cv, w_mod, b_mod.reshape(depth, 1, n))


def _inproj_kernel(x_ref, mod_ref, g1_ref, w_ref, wg_ref, bg_ref,
                   qkvg_ref, la_ref, u_ref, sc_ref, pool_ref):
    x = x_ref[0]
    mod = mod_ref[0]
    m = _rms(x, g1_ref[...]) * (1.0 + mod[1:2]) + mod[0:1]
    h = _dot(m, w_ref[...])
    qkvg_ref[0] = h[:, 0:768]
    u_ref[0] = h[:, 768:1024] * _sigmoid(h[:, 1024:1280])
    sc_ref[0, :, 0:256] = h[:, 1280:1536]
    sc_ref[0, :, 256:512] = h[:, 1536:1792] * h[:, 1792:2048]
    pool_ref[0] = h[:, 2048:2304]
    z = _dot(h[:, 2304:2432], wg_ref[...]) + bg_ref[...]
    la_ref[0] = _log_sigmoid(z) / GLA_GATE_NORM


def _inproj(x, mod, mod_row, g1, w_in_p, w_gate, b_gate, tm):
    bsz, seq, d = x.shape
    row = (lambda b: b) if mod_row is None else (lambda b: mod_row)
    tok = lambda w: pl.BlockSpec((1, tm, w), lambda b, i: (b, i, 0))
    full = lambda a: pl.BlockSpec(a.shape, lambda b, i: (0,) * a.ndim)
    outs = (768, 256, 256, 512, 256)
    return pl.pallas_call(
        _inproj_kernel,
        grid=(bsz, seq // tm),
        in_specs=[tok(d),
                  pl.BlockSpec((1, 6, d), lambda b, i: (row(b), 0, 0)),
                  full(g1), full(w_in_p), full(w_gate), full(b_gate)],
        out_specs=[tok(w) for w in outs],
        out_shape=[jax.ShapeDtypeStruct((bsz, seq, w), F32) for w in outs],
        compiler_params=_cp(("parallel", "parallel")),
        name="inproj",
    )(x, mod, g1, w_in_p, w_gate, b_gate)


def _gla_direction(q, k, v, la, state, reverse):
    blk = q.shape[0]
    g = GLA_GROUP
    nchunk = g // GLA_CHUNK
    ti = lax.broadcasted_iota(jnp.int32, (g, g), 0)
    tj = lax.broadcasted_iota(jnp.int32, (g, g), 1)
    same = (ti // GLA_CHUNK) == (tj // GLA_CHUNK)
    cmask = same & ((tj >= ti) if reverse else (tj <= ti))
    tri = jnp.where(cmask, 1.0, 0.0).astype(BF16)
    qk_head = lax.broadcasted_iota(jnp.int32, (g, GLA_QK_W), 1) // GLA_DK
    v_head = lax.broadcasted_iota(jnp.int32, (g, GROUP_W), 1) // GLA_DV
    bd = (lax.broadcasted_iota(jnp.int32, (GLA_QK_W, GROUP_W), 0) // GLA_DK
          == lax.broadcasted_iota(jnp.int32, (GLA_QK_W, GROUP_W), 1) // GLA_DV)
    tok_chunk = lax.broadcasted_iota(jnp.int32, (GLA_QK_W, g), 1) // GLA_CHUNK
    scale = GLA_DK ** -0.5
    outs = [None] * (blk // g)
    groups = range(blk // g)
    for gi in (reversed(groups) if reverse else groups):
        sl = slice(gi * g, (gi + 1) * g)
        qg, kg, vg, lg = q[sl], k[sl], v[sl], la[sl]
        l0 = lg.astype(BF16)
        r0 = lg - l0.astype(F32)
        l1 = r0.astype(BF16)
        l2 = (r0 - l1.astype(F32)).astype(BF16)
        b3 = jnp.dot(tri, jnp.concatenate([l0, l1, l2], axis=1), preferred_element_type=F32)
        b = b3[:, 0:GLA_QK_W] + b3[:, GLA_QK_W:2 * GLA_QK_W] + b3[:, 2 * GLA_QK_W:3 * GLA_QK_W]
        last = 0 if reverse else GLA_CHUNK - 1
        blast = jnp.concatenate(
            [jnp.broadcast_to(b[c * GLA_CHUNK + last:c * GLA_CHUNK + last + 1], (GLA_CHUNK, GLA_QK_W))
             for c in range(nchunk)], axis=0)
        qd = (qg * scale) * jnp.exp(b)
        ki = (kg * jnp.exp(-b)).astype(BF16)
        kw_t = (kg * jnp.exp(blast - b)).T
        b_t = b.T
        vb = vg.astype(BF16)
        qs = jnp.concatenate([jnp.where(qk_head == h, qd, 0.0) for h in range(GLA_HEADS)],
                             axis=0).astype(BF16)
        s = lax.dot_general(qs, ki, (((1,), (1,)), ((), ())), preferred_element_type=F32)
        s = jnp.where(jnp.concatenate([cmask] * GLA_HEADS, axis=0), s, 0.0).astype(BF16)
        oh = jnp.dot(s, vb, preferred_element_type=F32)
        o = jnp.zeros((g, GROUP_W), F32)
        for h in range(GLA_HEADS):
            o = o + jnp.where(v_head == h, oh[h * g:(h + 1) * g], 0.0)
        kws = jnp.concatenate([jnp.where(tok_chunk == c, kw_t, 0.0) for c in range(nchunk)],
                              axis=0).astype(BF16)
        kv_all = jnp.dot(kws, vb, preferred_element_type=F32)
        qdb = qd.astype(BF16)
        inter = [None] * nchunk
        chunks = range(nchunk)
        for c in (reversed(chunks) if reverse else chunks):
            rows = slice(c * GLA_CHUNK, (c + 1) * GLA_CHUNK)
            inter[c] = jnp.dot(qdb[rows], state.astype(BF16), preferred_element_type=F32)
            col = c * GLA_CHUNK + last
            dec = jnp.exp(b_t[:, col:col + 1])
            kv = kv_all[c * GLA_QK_W:(c + 1) * GLA_QK_W]
            state = dec * state + jnp.where(bd, kv, 0.0)
        outs[gi] = o + jnp.concatenate(inter, axis=0)
    return jnp.concatenate(outs, axis=0) if len(outs) > 1 else outs[0], state


def _gla_kernel(qk_f, v_f, la_f, qk_b, v_b, la_b, s0_ref, of_ref, ob_ref, sfin_ref, st_ref):
    i = pl.program_id(1)

    @pl.when(i == 0)
    def _():
        st_ref[...] = s0_ref[0]

    qk = qk_f[0]
    o, s = _gla_direction(qk[:, 0:128], qk[:, 128:256], v_f[0], la_f[0], st_ref[0], False)
    of_ref[0] = o
    st_ref[0] = s
    qk = qk_b[0]
    o, s = _gla_direction(qk[:, 0:128], qk[:, 128:256], v_b[0], la_b[0], st_ref[1], True)
    ob_ref[0] = o
    st_ref[1] = s

    @pl.when(i == pl.num_programs(1) - 1)
    def _():
        sfin_ref[0] = st_ref[...]


def _gla(qkvg, la, s0, blk):
    bsz, seq, _ = qkvg.shape
    nb = seq // blk
    fwd = lambda j: pl.BlockSpec((1, blk, 256), lambda b, i: (b, i, j))
    bwd = lambda j: pl.BlockSpec((1, blk, 256), lambda b, i: (b, nb - 1 - i, j))
    la_f = pl.BlockSpec((1, blk, 128), lambda b, i: (b, i, 0))
    la_b = pl.BlockSpec((1, blk, 128), lambda b, i: (b, nb - 1 - i, 1))
    st = pl.BlockSpec((1, 2, GLA_QK_W, GROUP_W), lambda b, i: (b, 0, 0, 0))
    return pl.pallas_call(
        _gla_kernel,
        grid=(bsz, nb),
        in_specs=[fwd(0), fwd(1), la_f, bwd(0), bwd(1), la_b, st],
        out_specs=[pl.BlockSpec((1, blk, GROUP_W), lambda b, i: (b, i, 0)),
                   pl.BlockSpec((1, blk, GROUP_W), lambda b, i: (b, nb - 1 - i, 0)),
                   st],
        out_shape=[jax.ShapeDtypeStruct((bsz, seq, GROUP_W), F32),
                   jax.ShapeDtypeStruct((bsz, seq, GROUP_W), F32),
                   jax.ShapeDtypeStruct((bsz, 2, GLA_QK_W, GROUP_W), F32)],
        scratch_shapes=[pltpu.VMEM((2, GLA_QK_W, GROUP_W), F32)],
        compiler_params=_cp(("parallel", "arbitrary")),
        name="gla",
    )(qkvg, qkvg, la, qkvg, qkvg, la, s0)


def _row_shift(x, s, col, row_len):
    if s == 0:
        return x
    n = x.shape[0]
    y = pltpu.roll(x, (-s) % n, 0)
    valid = (col < row_len - s) if s > 0 else (col >= -s)
    return jnp.where(valid, y, 0.0)


def _row_conv(x, w, col, row_len):
    k = w.shape[0]
    acc = None
    for j in range(k):
        term = _row_shift(x, j - k // 2, col, row_len) * w[j:j + 1]
        acc = term if acc is None else acc + term
    return acc


def _col_conv(pad_ref, w, t0, n):
    k = w.shape[0]
    acc = None
    for j in range(k):
        start = pl.multiple_of(t0 + j * GRID_W, GRID_W)
        term = pad_ref[pl.ds(start, n), :] * w[j:j + 1]
        acc = term if acc is None else acc + term
    return acc


def _pool_diffs(x, col, row_len):
    colh = col[:, 0:LANE]
    colf = colh.astype(F32)
    first = lax.broadcasted_iota(jnp.int32, colh.shape, 1) < POOL_GW

    def count(win):
        return jnp.minimum(colf + win // 2, float(row_len)) - jnp.maximum(colf - win // 2, 0.0)

    means = []
    for hi in range(len(POOL_WINDOWS) // 2):
        wa, wb = POOL_WINDOWS[2 * hi], POOL_WINDOWS[2 * hi + 1]
        xh = x[:, hi * LANE:(hi + 1) * LANE]
        acc_a = acc_b = None
        for s in range(-(wb // 2), wb // 2):
            t = _row_shift(xh, s, colh, row_len)
            acc_b = t if acc_b is None else acc_b + t
            if -(wa // 2) <= s < wa // 2:
                acc_a = t if acc_a is None else acc_a + t
        means.append(jnp.where(first, acc_a / count(wa), acc_b / count(wb)))
    return jnp.concatenate(means, axis=-1) - x


def _local_tail(u, scb, conv_sc, pool, col, row_len, cb, lg, lb, pw, ps, out_ref):
    u = u + cb
    uc = u - jnp.mean(u, axis=-1, keepdims=True)
    ln = uc * lax.rsqrt(jnp.mean(uc * uc, axis=-1, keepdims=True) + NORM_EPS) * lg + lb
    out_ref[0, :, 0:256] = _silu(ln).astype(BF16)
    out_ref[0, :, 256:512] = (scb * conv_sc).astype(BF16)
    out_ref[0, :, 512:768] = (_dot(_pool_diffs(pool, col, row_len), pw) * ps).astype(BF16)


def _local_axial_kernel(uh_ref, uv_ref, scb_ref, sch_ref, scv_ref, pool_ref,
                        cw_ref, cb_ref, lg_ref, lb_ref, sw_ref, pw_ref, ps_ref,
                        out_ref, upad, spad):
    i = pl.program_id(1)
    blk = uh_ref.shape[1]
    seq = uv_ref.shape[1]
    pu = (CONF_WIDTH // 2) * GRID_W
    psc = (SC_WIDTH // 2) * GRID_W

    @pl.when(i == 0)
    def _():
        upad[0:pu, :] = jnp.zeros((pu, LANE), F32)
        upad[pu:pu + seq, :] = uv_ref[0]
        upad[pu + seq:pu + seq + pu, :] = jnp.zeros((pu, LANE), F32)
        spad[0:psc, :] = jnp.zeros((psc, LANE), F32)
        spad[psc:psc + seq, :] = scv_ref[0]
        spad[psc + seq:psc + seq + psc, :] = jnp.zeros((psc, LANE), F32)

    t0 = i * blk
    col = lax.broadcasted_iota(jnp.int32, (blk, LANE), 0) % GRID_W
    cw = cw_ref[...]
    sw = sw_ref[...]
    u = jnp.concatenate([_row_conv(uh_ref[0], cw[:, 0:128], col, GRID_W),
                         _col_conv(upad, cw[:, 128:256], t0, blk)], axis=-1)
    csc = jnp.concatenate([_row_conv(sch_ref[0], sw[:, 0:128], col, GRID_W),
                           _col_conv(spad, sw[:, 128:256], t0, blk)], axis=-1)
    col2 = jnp.concatenate([col, col], axis=-1)
    _local_tail(u, scb_ref[0], csc, pool_ref[0], col2, GRID_W, cb_ref[...], lg_ref[...],
                lb_ref[...], pw_ref[...], ps_ref[...], out_ref)


def _local_seq_kernel(u_ref, sc_ref, pool_ref, cw_ref, cb_ref, lg_ref, lb_ref, sw_ref,
                      pw_ref, ps_ref, out_ref):
    seq = u_ref.shape[1]
    col = lax.broadcasted_iota(jnp.int32, (seq, GROUP_W), 0)
    sc = sc_ref[0]
    u = _row_conv(u_ref[0], cw_ref[...], col, seq)
    csc = _row_conv(sc[:, 256:512], sw_ref[...], col, seq)
    _local_tail(u, sc[:, 0:256], csc, pool_ref[0], col, seq, cb_ref[...], lg_ref[...],
                lb_ref[...], pw_ref[...], ps_ref[...], out_ref)


def _local_params(p):
    return (p["conf_w"], p["conf_b"], p["conf_g"], p["conf_beta"], p["sc_w"], p["pool_bd"],
            p["pool_scale"])


def _local_axial(u, sc, pool, p, blk):
    bsz, seq, _ = u.shape
    cw, cb, lg, lb, sw, pw, ps = _local_params(p)
    blkspec = lambda w, j: pl.BlockSpec((1, blk, w), lambda b, i: (b, i, j))
    seqspec = lambda j: pl.BlockSpec((1, seq, LANE), lambda b, i: (b, 0, j))
    full = lambda a: pl.BlockSpec(a.shape, lambda b, i: (0,) * a.ndim)
    pu = (CONF_WIDTH // 2) * GRID_W
    psc = (SC_WIDTH // 2) * GRID_W
    return pl.pallas_call(
        _local_axial_kernel,
        grid=(bsz, seq // blk),
        in_specs=[blkspec(LANE, 0), seqspec(1), blkspec(256, 0), blkspec(LANE, 2), seqspec(3),
                  blkspec(256, 0), full(cw), full(cb), full(lg), full(lb), full(sw), full(pw),
                  full(ps)],
        out_specs=pl.BlockSpec((1, blk, 768), lambda b, i: (b, i, 0)),
        out_shape=jax.ShapeDtypeStruct((bsz, seq, 768), BF16),
        scratch_shapes=[pltpu.VMEM((seq + 2 * pu, LANE), F32),
                        pltpu.VMEM((seq + 2 * psc, LANE), F32)],
        compiler_params=_cp(("parallel", "arbitrary")),
        name="local_axial",
    )(u, u, sc, sc, sc, pool, cw, cb, lg, lb, sw, pw, ps)


def _local_seq(u, sc, pool, p):
    bsz, seq, _ = u.shape
    cw, cb, lg, lb, sw, pw, ps = _local_params(p)
    tok = lambda w: pl.BlockSpec((1, seq, w), lambda b: (b, 0, 0))
    full = lambda a: pl.BlockSpec(a.shape, lambda b: (0,) * a.ndim)
    return pl.pallas_call(
        _local_seq_kernel,
        grid=(bsz,),
        in_specs=[tok(256), tok(512), tok(256), full(cw), full(cb), full(lg), full(lb),
                  full(sw), full(pw), full(ps)],
        out_specs=tok(768),
        out_shape=jax.ShapeDtypeStruct((bsz, seq, 768), BF16),
        compiler_params=_cp(("parallel",)),
        name="local_seq",
    )(u, sc, pool, cw, cb, lg, lb, sw, pw, ps)


def _route(logits):
    lane = lax.broadcasted_iota(jnp.int32, logits.shape, 1)
    neg = -jnp.inf
    big = ROUTE_W

    def first_max(vals):
        mx = jnp.max(vals, axis=-1, keepdims=True)
        idx = jnp.min(jnp.where(vals == mx, lane, big), axis=-1, keepdims=True)
        return mx, idx

    lg = jnp.where(lane < N_GROUPS, logits, neg)
    gmx, grp = first_max(lg)
    p_grp = 1.0 / jnp.sum(jnp.exp(lg - gmx), axis=-1, keepdims=True)
    lo = N_GROUPS + grp * EXPERTS_PER_GROUP
    le = jnp.where((lane >= lo) & (lane < lo + EXPERTS_PER_GROUP), logits, neg)
    v1, i1 = first_max(le)
    v2, i2 = first_max(jnp.where(lane == i1, neg, le))
    e2 = jnp.exp(v2 - v1)
    w1 = p_grp / (1.0 + e2)
    w2 = p_grp * e2 / (1.0 + e2)
    rec = jnp.where(lane == 0, (i1 - N_GROUPS).astype(F32), 0.0)
    rec = jnp.where(lane == 1, (i2 - N_GROUPS).astype(F32), rec)
    rec = jnp.where(lane == 2, w1, rec)
    return jnp.where(lane == 3, w2, rec)


def _outproj_kernel(of_ref, ob_ref, g_ref, loc_ref, x_ref, mod_ref, gout_ref, hsum_ref,
                    wo_ref, g2_ref, wr_ref, br_ref, xo_ref, m_ref, route_ref):
    mod = mod_ref[0]
    o = of_ref[0] + ob_ref[0]
    o2 = o * o
    hi = o2.astype(BF16)
    lo = (o2 - hi.astype(F32)).astype(BF16)
    ms = (jnp.dot(hi, hsum_ref[...], preferred_element_type=F32)
          + jnp.dot(lo, hsum_ref[...], preferred_element_type=F32)) / GLA_DV
    y_gla = o * lax.rsqrt(ms + NORM_EPS) * gout_ref[...] * _silu(g_ref[0])
    wo = wo_ref[...]
    proj = _dot(y_gla, wo[0:256]) + _dot(loc_ref[0], wo[256:1024])
    x = x_ref[0] + mod[2:3] * proj
    xo_ref[0] = x
    m = _rms(x, g2_ref[...]) * (1.0 + mod[4:5]) + mod[3:4]
    m_ref[0] = m
    route_ref[0] = _route(_dot(m, wr_ref[...]) + br_ref[...])


def _outproj(o_f, o_b, qkvg, y_loc, x, mod, mod_row, p, tm):
    bsz, seq, d = x.shape
    row = (lambda b: b) if mod_row is None else (lambda b: mod_row)
    tok = lambda w, j=0: pl.BlockSpec((1, tm, w), lambda b, i: (b, i, j))
    full = lambda a: pl.BlockSpec(a.shape, lambda b, i: (0,) * a.ndim)
    consts = (p["g_out"], p["head_sum"], p["w_out"], p["g_norm2"], p["w_router"], p["b_router"])
    return pl.pallas_call(
        _outproj_kernel,
        grid=(bsz, seq // tm),
        in_specs=[tok(256), tok(256), tok(256, 2), tok(768), tok(d),
                  pl.BlockSpec((1, 6, d), lambda b, i: (row(b), 0, 0))] + [full(a) for a in consts],
        out_specs=[tok(d), tok(d), tok(ROUTE_W)],
        out_shape=[jax.ShapeDtypeStruct((bsz, seq, d), F32),
                   jax.ShapeDtypeStruct((bsz, seq, d), F32),
                   jax.ShapeDtypeStruct((bsz, seq, ROUTE_W), F32)],
        compiler_params=_cp(("parallel", "parallel")),
        name="outproj",
    )(o_f, o_b, qkvg, y_loc, x, mod, *consts)


FUSED_TILE = 512
FUSED_VMEM_LIMIT = 56 << 20


def _outproj_local_kernel(of_ref, ob_ref, g_ref, x_ref, mod_ref,
                          uh_ref, uv_ref, scb_ref, sch_ref, scv_ref, pool_ref,
                          cw_ref, cb_ref, lg_ref, lb_ref, sw_ref, pw_ref, ps_ref,
                          gout_ref, hsum_ref, wo_ref, g2_ref, wr_ref, br_ref,
                          xo_ref, m_ref, route_ref, upad, spad, loc):
    _local_axial_kernel(uh_ref, uv_ref, scb_ref, sch_ref, scv_ref, pool_ref, cw_ref, cb_ref, lg_ref,
                        lb_ref, sw_ref, pw_ref, ps_ref, loc, upad, spad)
    _outproj_kernel(of_ref, ob_ref, g_ref, loc, x_ref, mod_ref, gout_ref, hsum_ref, wo_ref, g2_ref,
                    wr_ref, br_ref, xo_ref, m_ref, route_ref)


def _outproj_local(o_f, o_b, qkvg, u, sc, pool, x, mod, p, blk):
    bsz, seq, d = x.shape
    cw, cb, lg, lb, sw, pw, ps = _local_params(p)
    consts = (cw, cb, lg, lb, sw, pw, ps, p["g_out"], p["head_sum"], p["w_out"], p["g_norm2"],
              p["w_router"], p["b_router"])
    tok = lambda w, j=0: pl.BlockSpec((1, blk, w), lambda b, i: (b, i, j))
    seqspec = lambda j: pl.BlockSpec((1, seq, LANE), lambda b, i: (b, 0, j),
                                     pipeline_mode=pl.Buffered(1))
    full = lambda a: pl.BlockSpec(a.shape, lambda b, i: (0,) * a.ndim)
    pu = (CONF_WIDTH // 2) * GRID_W
    psc = (SC_WIDTH // 2) * GRID_W
    return pl.pallas_call(
        _outproj_local_kernel,
        grid=(bsz, seq // blk),
        in_specs=[tok(256), tok(256), tok(256, 2), tok(d),
                  pl.BlockSpec((1, 6, d), lambda b, i: (b, 0, 0)),
                  tok(LANE, 0), seqspec(1), tok(256, 0), tok(LANE, 2), seqspec(3), tok(256)]
                 + [full(a) for a in consts],
        out_specs=[tok(d), tok(d), tok(ROUTE_W)],
        out_shape=[jax.ShapeDtypeStruct((bsz, seq, d), F32),
                   jax.ShapeDtypeStruct((bsz, seq, d), F32),
                   jax.ShapeDtypeStruct((bsz, seq, ROUTE_W), F32)],
        scratch_shapes=[pltpu.VMEM((seq + 2 * pu, LANE), F32),
                        pltpu.VMEM((seq + 2 * psc, LANE), F32),
                        pltpu.VMEM((1, blk, 768), BF16)],
        compiler_params=_cp(("parallel", "arbitrary"), FUSED_VMEM_LIMIT),
        name="outproj_local",
    )(o_f, o_b, qkvg, x, mod, u, u, sc, sc, sc, pool, *consts)


RANK_ROWS = 512


def _rank_kernel(route_ref, rank_ref, cnt_ref, carry):
    i = pl.program_id(0)

    @pl.when(i == 0)
    def _():
        carry[...] = jnp.zeros_like(carry)

    rec = route_ref[...]
    tm = rec.shape[0]
    lane = lax.broadcasted_iota(jnp.int32, rec.shape, 1).astype(F32)
    oh0 = jnp.where(lane == rec[:, 0:1], 1.0, 0.0)
    oh1 = jnp.where(lane == rec[:, 1:2], 1.0, 0.0)
    ti = lax.broadcasted_iota(jnp.int32, (tm, tm), 0)
    tj = lax.broadcasted_iota(jnp.int32, (tm, tm), 1)
    before = jnp.where(tj < ti, 1.0, 0.0).astype(BF16)
    tot0 = jnp.sum(oh0, axis=0, keepdims=True)
    tot1 = jnp.sum(oh1, axis=0, keepdims=True)
    base = carry[...]
    pre0 = jnp.dot(before, oh0.astype(BF16), preferred_element_type=F32) + base
    pre1 = jnp.dot(before, oh1.astype(BF16), preferred_element_type=F32) + (base + tot0)
    r0 = jnp.sum(oh0 * pre0, axis=-1, keepdims=True)
    r1 = jnp.sum(oh1 * pre1, axis=-1, keepdims=True)
    lane_i = lax.broadcasted_iota(jnp.int32, rec.shape, 1)
    cols = jnp.where(lane_i == 0, r0, jnp.where(lane_i == 1, r1, 0.0))
    cols = jnp.where(lane_i == 2, rec[:, 0:1], jnp.where(lane_i == 3, rec[:, 1:2], cols))
    rank_ref[...] = cols.T[0:8, :].astype(jnp.int32)
    carry[...] = base + tot0 + tot1
    cnt_ref[...] = carry[...]


def _rank(route):
    n_tok = route.shape[0]
    tm = RANK_ROWS if n_tok % RANK_ROWS == 0 else 256
    return pl.pallas_call(
        _rank_kernel,
        grid=(n_tok // tm,),
        in_specs=[pl.BlockSpec((tm, ROUTE_W), lambda i: (i, 0))],
        out_specs=[pl.BlockSpec((8, tm), lambda i: (0, i)),
                   pl.BlockSpec((1, ROUTE_W), lambda i: (0, 0))],
        out_shape=[jax.ShapeDtypeStruct((8, n_tok), jnp.int32),
                   jax.ShapeDtypeStruct((1, ROUTE_W), F32)],
        scratch_shapes=[pltpu.VMEM((1, ROUTE_W), F32)],
        compiler_params=_cp(("arbitrary",)),
        name="rank",
    )(route)


def _plan(route):
    n_tok = route.shape[0]
    rank, cnt = _rank(route)
    counts = cnt[0, 0:N_EXPERTS].astype(jnp.int32)
    padded = (counts + MOE_BLOCK - 1) // MOE_BLOCK * MOE_BLOCK
    pad_end = jnp.cumsum(padded)
    pad_start = pad_end - padded
    onehot = rank[2:4][None] == jnp.arange(N_EXPERTS, dtype=jnp.int32)[:, None, None]
    dest = (jnp.sum(jnp.where(onehot, pad_start[:, None, None], 0), axis=0) + rank[0:2]).reshape(-1)
    n_blocks = 2 * n_tok // MOE_BLOCK + N_EXPERTS
    blk_start = jnp.arange(n_blocks, dtype=jnp.int32) * MOE_BLOCK
    nused = pad_end[-1] // MOE_BLOCK
    blk = jnp.minimum(jnp.arange(n_blocks, dtype=jnp.int32), nused - 1)
    blk_expert = jnp.sum((blk[:, None] * MOE_BLOCK >= pad_end[None, :]).astype(jnp.int32), axis=-1)
    fill_start = pad_start + counts
    fill_n = padded - counts
    ids = jnp.arange(N_EXPERTS, dtype=jnp.int32)
    later = (ids[None, :] > ids[:, None]) & (counts[None, :] > 0)
    next_expert = jnp.min(jnp.where(later, ids[None, :], N_EXPERTS), axis=1)
    next_expert = jnp.where(next_expert == N_EXPERTS, -1, next_expert).astype(jnp.int32)
    return dict(dest=dest.astype(jnp.int32), blk=blk.astype(jnp.int32), next_expert=next_expert,
                blk_expert=jnp.minimum(blk_expert, N_EXPERTS - 1).astype(jnp.int32),
                nused=nused.astype(jnp.int32).reshape(1), fill_start=fill_start.astype(jnp.int32),
                fill_n=fill_n.astype(jnp.int32), n_blocks=n_blocks)


DISPATCH_ROWS = 256


def _dispatch_kernel(dest_ref, fstart_ref, fn_ref, nused_ref, *refs, tr, n_x, n_c):
    if n_c:
        mx_hbm, mc_hbm, xs_hbm, buf, zbuf, in_sem, out_sem, fsem = refs
    else:
        mx_hbm, xs_hbm, buf, zbuf, in_sem, out_sem, fsem = refs
        mc_hbm = None
    i = pl.program_id(0)
    nsteps = n_x + n_c
    nslot = 3

    def start_load(step):
        slot = step % nslot

        @pl.when(step < n_x)
        def _():
            rows = pl.ds(pl.multiple_of(step * tr, tr), tr)
            pltpu.make_async_copy(mx_hbm.at[rows], buf.at[slot], in_sem.at[slot]).start()

        if n_c:
            @pl.when(step >= n_x)
            def _():
                rows = pl.ds(pl.multiple_of((step - n_x) * tr, tr), tr)
                pltpu.make_async_copy(mc_hbm.at[rows], buf.at[slot], in_sem.at[slot]).start()

    def wait_load(step):
        slot = step % nslot
        pltpu.make_async_copy(mx_hbm.at[pl.ds(0, tr)], buf.at[slot], in_sem.at[slot]).wait()

    def row_copy(step, r, k):
        slot = step % nslot
        dst = dest_ref[k * (nsteps * tr) + step * tr + r]
        return pltpu.make_async_copy(buf.at[slot, pl.ds(r, 1)], xs_hbm.at[pl.ds(dst, 1)],
                                     out_sem.at[slot])

    def wait_scatter(step):
        slot = step % nslot
        for _ in range(2):
            pltpu.make_async_copy(buf.at[slot], xs_hbm.at[pl.ds(0, tr)], out_sem.at[slot]).wait()

    @pl.when(i == 0)
    def _():
        zbuf[...] = jnp.zeros_like(zbuf)
        start_load(i)

    @pl.when(i >= 2)
    def _():
        wait_scatter(i - 2)

    @pl.when(i + 1 < nsteps)
    def _():
        start_load(i + 1)

    wait_load(i)

    for r in range(tr):
        row_copy(i, r, 0).start()
        row_copy(i, r, 1).start(priority=1)

    @pl.when(i == nsteps - 1)
    def _():
        if nsteps >= 2:
            wait_scatter(i - 1)
        wait_scatter(i)

        def fill_row(e, r):
            return pltpu.make_async_copy(zbuf.at[pl.ds(0, 1)],
                                         xs_hbm.at[pl.ds(fstart_ref[e] + r, 1)], fsem.at[0])

        def fill_piece(e, head, j):
            start = pl.multiple_of(fstart_ref[e] + head + 8 * j, 8)
            return pltpu.make_async_copy(zbuf.at[pl.ds(0, 8)], xs_hbm.at[pl.ds(start, 8)],
                                         fsem.at[0])

        def per_expert(start_not_wait):
            def body(e, carry):
                head = jnp.minimum((8 - fstart_ref[e] % 8) % 8, fn_ref[e])
                pieces = (fn_ref[e] - head) // 8
                if start_not_wait:
                    lax.fori_loop(0, head, lambda r, c: (fill_row(e, r).start(), c)[1], 0)
                    lax.fori_loop(0, pieces, lambda j, c: (fill_piece(e, head, j).start(), c)[1], 0)
                else:
                    lax.fori_loop(0, head, lambda r, c: (fill_row(e, r).wait(), c)[1], 0)
                    lax.fori_loop(0, pieces, lambda j, c: (fill_piece(e, head, j).wait(), c)[1], 0)
                return carry
            return body

        lax.fori_loop(0, N_EXPERTS, per_expert(True), 0)
        lax.fori_loop(0, N_EXPERTS, per_expert(False), 0)

        def tail_copy(b):
            start = pl.multiple_of(b * MOE_BLOCK, MOE_BLOCK)
            return pltpu.make_async_copy(zbuf, xs_hbm.at[pl.ds(start, MOE_BLOCK)], fsem.at[0])

        nblk = xs_hbm.shape[0] // MOE_BLOCK
        lax.fori_loop(nused_ref[0], nblk, lambda b, c: (tail_copy(b).start(), c)[1], 0)
        lax.fori_loop(nused_ref[0], nblk, lambda b, c: (tail_copy(b).wait(), c)[1], 0)


def _dispatch(m_x, m_c, plan):
    n_lat, d = m_x.shape
    n_ctx = 0 if m_c is None else m_c.shape[0]
    tr = DISPATCH_ROWS
    while n_lat % tr or n_ctx % tr:
        tr //= 2
    slots = plan["n_blocks"] * MOE_BLOCK
    n_x, n_c = n_lat // tr, n_ctx // tr
    srcs = (m_x,) if m_c is None else (m_x, m_c)
    grid_spec = pltpu.PrefetchScalarGridSpec(
        num_scalar_prefetch=4,
        grid=(n_x + n_c,),
        in_specs=[pl.BlockSpec(memory_space=pl.ANY)] * len(srcs),
        out_specs=pl.BlockSpec(memory_space=pl.ANY),
        scratch_shapes=[pltpu.VMEM((3, tr, d), F32), pltpu.VMEM((MOE_BLOCK, d), F32),
                        pltpu.SemaphoreType.DMA((3,)), pltpu.SemaphoreType.DMA((3,)),
                        pltpu.SemaphoreType.DMA((1,))],
    )
    return pl.pallas_call(
        functools.partial(_dispatch_kernel, tr=tr, n_x=n_x, n_c=n_c),
        grid_spec=grid_spec,
        out_shape=jax.ShapeDtypeStruct((slots, d), F32),
        compiler_params=_cp(("arbitrary",)),
        name="dispatch",
    )(plan["dest"], plan["fill_start"], plan["fill_n"], plan["nused"], *srcs)


XS_SLOTS = 3


def _expert_kernel(be_ref, nused_ref, next_ref, xs_hbm, w1_hbm, w2_hbm, ys_ref,
                   xbuf, w1f, w2f, w1b, w2b, xsem, wsem, *, layer):
    i = pl.program_id(0)
    nused = nused_ref[0]
    used = i < nused
    e = be_ref[i]
    fresh = (i == 0) | (e != be_ref[jnp.maximum(i - 1, 0)])

    def weight_copies(ex):
        return (pltpu.make_async_copy(w1_hbm.at[layer, ex], w1f, wsem.at[0]),
                pltpu.make_async_copy(w2_hbm.at[layer, ex], w2f, wsem.at[1]))

    def block_copy(b):
        rows = pl.ds(pl.multiple_of(b * MOE_BLOCK, MOE_BLOCK), MOE_BLOCK)
        return pltpu.make_async_copy(xs_hbm.at[rows], xbuf.at[b % XS_SLOTS], xsem.at[b % XS_SLOTS])

    @pl.when(used & (i == 0))
    def _():
        for cp in weight_copies(e):
            cp.start()
        block_copy(i).start()

        @pl.when(nused > 1)
        def _():
            block_copy(i + 1).start()

    @pl.when(i + 2 < nused)
    def _():
        block_copy(i + 2).start()

    @pl.when(used & fresh)
    def _():
        for cp in weight_copies(e):
            cp.wait()
        w1b[...] = w1f[...].astype(BF16)
        w2b[...] = w2f[...].astype(BF16)
        nxt = next_ref[e]

        @pl.when(nxt >= 0)
        def _():
            for cp in weight_copies(nxt):
                cp.start()

    @pl.when(used)
    def _():
        block_copy(i).wait()
        h = _dot(xbuf[i % XS_SLOTS], w1b[...])
        act = _silu(h[:, :D_EXPERT]) * h[:, D_EXPERT:]
        ys_ref[...] = _dot(act, w2b[...])

    @pl.when(jnp.logical_not(used))
    def _():
        ys_ref[...] = jnp.zeros_like(ys_ref)


def _experts(xs, plan, w1, w2, layer):
    slots, d = xs.shape
    grid_spec = pltpu.PrefetchScalarGridSpec(
        num_scalar_prefetch=3,
        grid=(plan["n_blocks"],),
        in_specs=[pl.BlockSpec(memory_space=pl.ANY), pl.BlockSpec(memory_space=pl.ANY),
                  pl.BlockSpec(memory_space=pl.ANY)],
        out_specs=pl.BlockSpec((MOE_BLOCK, d), lambda i, *_: (i, 0)),
        scratch_shapes=[pltpu.VMEM((XS_SLOTS, MOE_BLOCK, d), F32),
                        pltpu.VMEM((d, 2 * D_EXPERT), F32), pltpu.VMEM((D_EXPERT, d), F32),
                        pltpu.VMEM((d, 2 * D_EXPERT), BF16), pltpu.VMEM((D_EXPERT, d), BF16),
                        pltpu.SemaphoreType.DMA((XS_SLOTS,)), pltpu.SemaphoreType.DMA((2,))],
    )
    return pl.pallas_call(
        functools.partial(_expert_kernel, layer=layer),
        grid_spec=grid_spec,
        out_shape=jax.ShapeDtypeStruct((slots, d), F32),
        compiler_params=_cp(("arbitrary",)),
        name="experts",
    )(plan["blk_expert"], plan["nused"], plan["next_expert"], xs, w1, w2)


COMBINE_ROWS = 256


def _combine_kernel(dest_ref, x_ref, route_ref, mod_ref, gf_ref, ys_hbm, o_ref, y0buf, y1buf, sem,
                    *, final, tok_off):
    i = pl.program_id(0)
    nsteps = pl.num_programs(0)
    tc = x_ref.shape[1]

    def row_copy(step, r, k, slot):
        a = k * (dest_ref.shape[0] // 2) + tok_off + step * tc + r
        buf = y0buf if k == 0 else y1buf
        return pltpu.make_async_copy(ys_hbm.at[pl.ds(dest_ref[a], 1)], buf.at[slot, pl.ds(r, 1)],
                                     sem.at[slot])

    def issue_step(step):
        for r in range(tc):
            row_copy(step, r, 0, step % 2).start()
            row_copy(step, r, 1, step % 2).start()

    @pl.when(i == 0)
    def _():
        issue_step(0)

    @pl.when(i + 1 < nsteps)
    def _():
        issue_step(i + 1)

    slot = i % 2
    pltpu.make_async_copy(ys_hbm.at[pl.ds(0, tc)], y0buf.at[slot], sem.at[slot]).wait()
    pltpu.make_async_copy(ys_hbm.at[pl.ds(0, tc)], y1buf.at[slot], sem.at[slot]).wait()
    rec = route_ref[...]
    y = rec[:, 2:3] * y0buf[slot] + rec[:, 3:4] * y1buf[slot]
    x = x_ref[0] + mod_ref[0][5:6] * y
    o_ref[0] = _rms(x, gf_ref[...]) if final else x


def _combine(x, ys, route, plan, tok_off, mod, mod_row, g_final, final):
    bsz, seq, d = x.shape
    tc = min(COMBINE_ROWS, seq)
    nt = seq // tc
    assert seq % tc == 0 and tok_off % tc == 0
    row = (lambda i: i // nt) if mod_row is None else (lambda i: mod_row)
    grid_spec = pltpu.PrefetchScalarGridSpec(
        num_scalar_prefetch=1,
        grid=(bsz * nt,),
        in_specs=[pl.BlockSpec((1, tc, d), lambda i, de: (i // nt, i % nt, 0)),
                  pl.BlockSpec((tc, ROUTE_W), lambda i, de: (tok_off // tc + i, 0)),
                  pl.BlockSpec((1, 6, d), lambda i, de: (row(i), 0, 0)),
                  pl.BlockSpec((1, d), lambda i, de: (0, 0)),
                  pl.BlockSpec(memory_space=pl.ANY)],
        out_specs=pl.BlockSpec((1, tc, d), lambda i, de: (i // nt, i % nt, 0)),
        scratch_shapes=[pltpu.VMEM((2, tc, d), F32), pltpu.VMEM((2, tc, d), F32),
                        pltpu.SemaphoreType.DMA((2,))],
    )
    return pl.pallas_call(
        functools.partial(_combine_kernel, final=final, tok_off=tok_off),
        grid_spec=grid_spec,
        out_shape=jax.ShapeDtypeStruct((bsz, seq, d), F32),
        compiler_params=_cp(("arbitrary",)),
        name="combine",
    )(plan["dest"], x, route, mod, g_final, ys)


def _layer_params(l, w_in, gla_w_gate_f, gla_b_gate_f, gla_w_gate_b, gla_b_gate_b, gla_g_out,
                  conf_w_dw, conf_b_dw, conf_ln_g, conf_ln_b, sc_w_dw, pool_w, pool_scale, w_out,
                  router_w_group, router_b_group, router_w_expert, router_b_expert,
                  g_norm1, g_norm2):
    d = D_MODEL
    idx = np.cumsum((0,) + IN_SPLITS)
    q, k, v, g, zf, zb, conf, scb, scc, scx, pool = [w_in[l][:, idx[j]:idx[j + 1]] for j in range(11)]
    pad = jnp.zeros((d, N_IN_PAD - idx[-1]), F32)
    w_in_p = jnp.concatenate([q, k, v, g, conf, scb, scc, scx, pool, zf, zb, pad], axis=1).astype(BF16)
    r = GLA_GATE_RANK
    w_gate = jnp.zeros((LANE, 2 * GLA_QK_W), F32)
    w_gate = w_gate.at[0:r, 0:GLA_QK_W].set(gla_w_gate_f[l]).at[r:2 * r, GLA_QK_W:].set(gla_w_gate_b[l])
    b_gate = jnp.concatenate([gla_b_gate_f[l], gla_b_gate_b[l]])[None]
    pool_bd = jnp.zeros((GROUP_W, GROUP_W), F32)
    for gi in range(len(POOL_WINDOWS)):
        s = slice(gi * POOL_GW, (gi + 1) * POOL_GW)
        pool_bd = pool_bd.at[s, s].set(pool_w[l, gi])
    head = np.arange(GROUP_W) // GLA_DV
    head_sum = jnp.asarray(head[:, None] == head[None, :], BF16)
    w_router = jnp.zeros((d, ROUTE_W), F32)
    w_router = w_router.at[:, 0:N_GROUPS].set(router_w_group[l])
    w_router = w_router.at[:, N_GROUPS:N_GROUPS + N_EXPERTS].set(router_w_expert[l])
    b_router = jnp.zeros((1, ROUTE_W), F32)
    b_router = b_router.at[0, 0:N_GROUPS].set(router_b_group[l])
    b_router = b_router.at[0, N_GROUPS:N_GROUPS + N_EXPERTS].set(router_b_expert[l])
    return dict(
        w_in=w_in_p, w_gate=w_gate.astype(BF16), b_gate=b_gate,
        g_norm1=g_norm1[l][None], g_norm2=g_norm2[l][None],
        g_out=gla_g_out[l][None], head_sum=head_sum,
        conf_w=conf_w_dw[l], conf_b=conf_b_dw[l][None], conf_g=conf_ln_g[l][None],
        conf_beta=conf_ln_b[l][None], sc_w=sc_w_dw[l], pool_bd=pool_bd.astype(BF16),
        pool_scale=pool_scale[l][None], w_out=w_out[l].astype(BF16),
        w_router=w_router.astype(BF16), b_router=b_router)


def kernel(x, c, ctx, c_ctx, w_mod, b_mod, g_norm1, g_norm2, w_in, gla_w_gate_f, gla_b_gate_f,
           gla_w_gate_b, gla_b_gate_b, gla_g_out, conf_w_dw, conf_b_dw, conf_ln_g, conf_ln_b,
           sc_w_dw, pool_w, pool_scale, w_out, router_w_group, router_b_group, router_w_expert,
           router_b_expert, expert_w_in, expert_w_out, g_final):
    bsz, seq, d = x.shape
    ctx_len = ctx.shape[1]
    depth = w_mod.shape[0]
    n_lat = bsz * seq
    n_ctx = bsz * ctx_len
    assert d == D_MODEL and seq % TOKEN_TILE == 0 and ctx_len % GLA_GROUP == 0 and bsz <= 7
    tm = TOKEN_TILE
    tm_c = GLA_GROUP

    cv = jnp.zeros((8, d), F32).at[0:bsz].set(c).at[bsz].set(c_ctx)
    mods = _adaln(cv, w_mod, b_mod).reshape(depth, 8, 6, d)
    gf = g_final[None]
    zero_state = jnp.zeros((bsz, 2, GLA_QK_W, GROUP_W), F32)

    for l in range(depth):
        last = l == depth - 1
        p = _layer_params(l, w_in, gla_w_gate_f, gla_b_gate_f, gla_w_gate_b, gla_b_gate_b,
                          gla_g_out, conf_w_dw, conf_b_dw, conf_ln_g, conf_ln_b, sc_w_dw, pool_w,
                          pool_scale, w_out, router_w_group, router_b_group, router_w_expert,
                          router_b_expert, g_norm1, g_norm2)
        mod = mods[l]

        c_qkvg, c_la, c_u, c_sc, c_pool = _inproj(ctx, mod, bsz, p["g_norm1"], p["w_in"],
                                                  p["w_gate"], p["b_gate"], tm_c)
        c_of, c_ob, s_ctx = _gla(c_qkvg, c_la, zero_state, ctx_len)
        x_qkvg, x_la, x_u, x_sc, x_pool = _inproj(x, mod, None, p["g_norm1"], p["w_in"],
                                                  p["w_gate"], p["b_gate"], tm)
        x_of, x_ob, _ = _gla(x_qkvg, x_la, s_ctx, TOKEN_TILE)
        x, m_x, r_x = _outproj_local(x_of, x_ob, x_qkvg, x_u, x_sc, x_pool, x, mod, p, FUSED_TILE)

        if last:
            m_c = None
            route = r_x.reshape(n_lat, ROUTE_W)
        else:
            c_loc = _local_seq(c_u, c_sc, c_pool, p)
            ctx, m_c, r_c = _outproj(c_of, c_ob, c_qkvg, c_loc, ctx, mod, bsz, p, tm_c)
            m_c = m_c.reshape(n_ctx, d)
            route = jnp.concatenate([r_x.reshape(n_lat, ROUTE_W), r_c.reshape(n_ctx, ROUTE_W)], axis=0)

        plan = _plan(route)
        xs = _dispatch(m_x.reshape(n_lat, d), m_c, plan)
        ys = _experts(xs, plan, expert_w_in, expert_w_out, l)
        x = _combine(x, ys, route, plan, 0, mod, None, gf, last)
        if not last:
            ctx = _combine(ctx, ys, route, plan, n_lat, mod, bsz, gf, False)
    return x
```
